```python
import math
import jax, jax.numpy as jnp
from jax import lax
import numpy as np

D_MODEL = 1024
BATCH = 8
SEQ = 4096
DEPTH = 2

A_HEADS = 8
A_HEAD_DIM = 64
A_WIDTH = A_HEADS * A_HEAD_DIM
MOBA_BLOCK = 256
MOBA_TOPK = 3
MOBA_QCHUNK = 16
B_HEADS = 8
B_HEAD_DIM = 64
B_WIDTH = B_HEADS * B_HEAD_DIM
IDX_HEADS = 4
IDX_DIM = 64
DSA_TOPK_MAX = 256
DSA_QCHUNK = 128
REL_BUCKETS = 32
REL_MAX_DIST = 128
N_HEADS_TOTAL = A_HEADS + B_HEADS
EPS = 1e-6

SPLIT_SIZES = (A_WIDTH, A_WIDTH, A_WIDTH, A_WIDTH,
               B_WIDTH, B_HEAD_DIM, B_HEAD_DIM, B_WIDTH,
               IDX_HEADS * IDX_DIM, IDX_DIM, IDX_HEADS,
               D_MODEL, D_MODEL)
D_IN = 4 * A_WIDTH + 2 * B_WIDTH + 2 * B_HEAD_DIM + IDX_HEADS * IDX_DIM + IDX_DIM + IDX_HEADS + 2 * D_MODEL

kernel_name = "hybrid_moba_dsa_gated_block"


def _split_points():
    pts, acc = [], 0
    for s in SPLIT_SIZES[:-1]:
        acc += s
        pts.append(acc)
    return pts


def rmsnorm(x, g):
    x32 = x.astype(jnp.float32)
    r = lax.rsqrt(jnp.mean(x32 * x32, axis=-1, keepdims=True) + EPS)
    return (x32 * r).astype(x.dtype) * g


def rel_bucket(n):
    n = jnp.maximum(n, 0)
    max_exact = REL_BUCKETS // 2
    nf = jnp.maximum(n, 1).astype(jnp.float32)
    large = max_exact + (jnp.log(nf / max_exact) / math.log(REL_MAX_DIST / max_exact)
                         * (REL_BUCKETS - max_exact)).astype(jnp.int32)
    large = jnp.minimum(large, REL_BUCKETS - 1)
    return jnp.where(n < max_exact, n, large)


def moba_attention(q, k, v, bias_table):
    B, S, H, Dh = q.shape
    nb = -(-S // MOBA_BLOCK)
    sp = nb * MOBA_BLOCK
    ntop = min(MOBA_TOPK, nb)
    pad = ((0, 0), (0, sp - S), (0, 0), (0, 0))
    kb = jnp.pad(k, pad).reshape(B, nb, MOBA_BLOCK, H, Dh)
    vb = jnp.pad(v, pad).reshape(B, nb, MOBA_BLOCK, H, Dh)
    kmean = jnp.mean(kb.astype(jnp.float32), axis=2).astype(k.dtype)
    kbh = kb.transpose(0, 3, 1, 2, 4)
    vbh = vb.transpose(0, 3, 1, 2, 4)
    scale = Dh ** -0.5
    nc = S // MOBA_QCHUNK
    qc_all = q.reshape(B, nc, MOBA_QCHUNK, H, Dh).transpose(1, 0, 2, 3, 4)
    bi = jnp.arange(B)[:, None, None, None]
    hi = jnp.arange(H)[None, :, None, None]
    offs = jnp.arange(MOBA_BLOCK)

    def chunk(args):
        ci, qi = args
        pos = ci * MOBA_QCHUNK + jnp.arange(MOBA_QCHUNK)
        qblk = (ci * MOBA_QCHUNK) // MOBA_BLOCK
        qh = qi.transpose(0, 2, 1, 3)
        gate = jnp.einsum('bhqd,bnhd->bhqn', qh, kmean).astype(jnp.float32)
        gate = jnp.where(jnp.arange(nb) < qblk, gate, -jnp.inf)
        _, sel = lax.top_k(gate, ntop)
        valid = sel < qblk
        kg = kbh[bi, hi, sel]
        vg = vbh[bi, hi, sel]
        keypos = sel[..., None] * MOBA_BLOCK + offs
        dist = pos[None, None, :, None, None] - keypos
        b_past = bias_table[rel_bucket(dist), jnp.arange(H)[None, :, None, None, None]]
        lp = (jnp.einsum('bhqd,bhqjkd->bhqjk', qh, kg) * scale).astype(jnp.float32) + b_past.astype(jnp.float32)
        lp = jnp.where(valid[..., None], lp, -jnp.inf)
        kown = lax.dynamic_index_in_dim(kbh, qblk, axis=2, keepdims=False)
        vown = lax.dynamic_index_in_dim(vbh, qblk, axis=2, keepdims=False)
        dist_o = pos[:, None] - (qblk * MOBA_BLOCK + offs)[None, :]
        b_own = bias_table[rel_bucket(dist_o)].transpose(2, 0, 1)
        lo = (jnp.einsum('bhqd,bhkd->bhqk', qh, kown) * scale).astype(jnp.float32) + b_own.astype(jnp.float32)
        lo = jnp.where(dist_o >= 0, lo, -jnp.inf)
        logits = jnp.concatenate([lp.reshape(B, H, MOBA_QCHUNK, ntop * MOBA_BLOCK), lo], axis=-1)
        p = jax.nn.softmax(logits, axis=-1).astype(v.dtype)
        pp = p[..., :ntop * MOBA_BLOCK].reshape(B, H, MOBA_QCHUNK, ntop, MOBA_BLOCK)
        po = p[..., ntop * MOBA_BLOCK:]
        return (jnp.einsum('bhqjk,bhqjkd->bqhd', pp, vg)
                + jnp.einsum('bhqk,bhkd->bqhd', po, vown))

    out = lax.map(chunk, (jnp.arange(nc), qc_all))
    return out.transpose(1, 0, 2, 3, 4).reshape(B, S, H, Dh)


def dsa_attention(q, k, v, q_idx, k_idx, w_idx, bias_table):
    B, S, H, Dh = q.shape
    topk = min(DSA_TOPK_MAX, S // 4)
    nc = S // DSA_QCHUNK
    scale = Dh ** -0.5
    idx_scale = IDX_DIM ** -0.5
    w_scale = IDX_HEADS ** -0.5
    qc_all = q.reshape(B, nc, DSA_QCHUNK, H, Dh).transpose(1, 0, 2, 3, 4)
    qic_all = q_idx.reshape(B, nc, DSA_QCHUNK, IDX_HEADS, IDX_DIM).transpose(1, 0, 2, 3, 4)
    wic_all = w_idx.reshape(B, nc, DSA_QCHUNK, IDX_HEADS).transpose(1, 0, 2, 3)
    bi = jnp.arange(B)[:, None, None]
    key_pos = jnp.arange(S)

    def chunk(args):
        ci, qc, qic, wic = args
        pos = ci * DSA_QCHUNK + jnp.arange(DSA_QCHUNK)
        rel = jax.nn.relu(jnp.einsum('bqhd,bsd->bqhs', qic, k_idx) * idx_scale)
        score = jnp.einsum('bqhs,bqh->bqs', rel, wic * w_scale).astype(jnp.float32)
        score = jnp.where(key_pos[None, None, :] <= pos[None, :, None], score, -jnp.inf)
        _, sel = lax.top_k(score, topk)
        valid = sel <= pos[None, :, None]
        kg = k[bi, sel]
        vg = v[bi, sel]
        dist = pos[None, :, None] - sel
        bias = bias_table[rel_bucket(dist)].transpose(0, 3, 1, 2)
        logits = (jnp.einsum('bqhd,bqkd->bhqk', qc, kg) * scale).astype(jnp.float32) + bias.astype(jnp.float32)
        logits = jnp.where(valid[:, None], logits, -jnp.inf)
        p = jax.nn.softmax(logits, axis=-1).astype(v.dtype)
        return jnp.einsum('bhqk,bqkd->bqhd', p, vg)

    out = lax.map(chunk, (jnp.arange(nc), qc_all, qic_all, wic_all))
    return out.transpose(1, 0, 2, 3, 4).reshape(B, S, H, Dh)


def hybrid_layer(x, g_norm, w_in, qn_a, kn_a, qn_b, kn_b, w_br_a, w_br_b, w_out, rel_bias):
    B, S, _ = x.shape
    h = rmsnorm(x, g_norm)
    proj = jnp.einsum('bsd,de->bse', h, w_in)
    (qa, ka, va, ga, qb, kb, vb, gb, qi, ki, wi, ma, mb) = jnp.split(proj, _split_points(), axis=-1)
    qa = rmsnorm(qa.reshape(B, S, A_HEADS, A_HEAD_DIM), qn_a)
    ka = rmsnorm(ka.reshape(B, S, A_HEADS, A_HEAD_DIM), kn_a)
    va = va.reshape(B, S, A_HEADS, A_HEAD_DIM)
    ya = moba_attention(qa, ka, va, rel_bias[:, :A_HEADS]).reshape(B, S, A_WIDTH) * jax.nn.silu(ga)
    qb = rmsnorm(qb.reshape(B, S, B_HEADS, B_HEAD_DIM), qn_b)
    kb = rmsnorm(kb, kn_b)
    qi = qi.reshape(B, S, IDX_HEADS, IDX_DIM)
    yb = dsa_attention(qb, kb, vb, qi, ki, wi, rel_bias[:, A_HEADS:]).reshape(B, S, B_WIDTH) * jax.nn.silu(gb)
    merged = (jax.nn.sigmoid(ma) * jnp.einsum('bsw,wd->bsd', ya, w_br_a)
              + jax.nn.sigmoid(mb) * jnp.einsum('bsw,wd->bsd', yb, w_br_b))
    return x + jnp.einsum('bsd,de->bse', merged, w_out)


def setup_inputs(seed: int = 0) -> dict:
    key = jax.random.key(seed)
    ks = jax.random.split(key, 12)
    f32 = jnp.float32
    x = jax.random.normal(ks[0], (BATCH, SEQ, D_MODEL), f32)
    norm_g = 1.0 + 0.02 * jax.random.normal(ks[1], (DEPTH, D_MODEL), f32)
    w_in = jax.random.normal(ks[2], (DEPTH, D_MODEL, D_IN), f32) * D_MODEL ** -0.5
    q_norm_a = 1.0 + 0.02 * jax.random.normal(ks[3], (DEPTH, A_HEAD_DIM), f32)
    k_norm_a = 1.0 + 0.02 * jax.random.normal(ks[4], (DEPTH, A_HEAD_DIM), f32)
    q_norm_b = 1.0 + 0.02 * jax.random.normal(ks[5], (DEPTH, B_HEAD_DIM), f32)
    k_norm_b = 1.0 + 0.02 * jax.random.normal(ks[6], (DEPTH, B_HEAD_DIM), f32)
    w_branch_a = jax.random.normal(ks[7], (DEPTH, A_WIDTH, D_MODEL), f32) * A_WIDTH ** -0.5
    w_branch_b = jax.random.normal(ks[8], (DEPTH, B_WIDTH, D_MODEL), f32) * B_WIDTH ** -0.5
    w_out = jax.random.normal(ks[9], (DEPTH, D_MODEL, D_MODEL), f32) * D_MODEL ** -0.5
    rel_bias = 0.5 * jax.random.normal(ks[10], (REL_BUCKETS, N_HEADS_TOTAL), f32)
    return {"x": x, "norm_g": norm_g, "w_in": w_in,
            "q_norm_a": q_norm_a, "k_norm_a": k_norm_a,
            "q_norm_b": q_norm_b, "k_norm_b": k_norm_b,
            "w_branch_a": w_branch_a, "w_branch_b": w_branch_b,
            "w_out": w_out, "rel_bias": rel_bias}


def reference(x, norm_g, w_in, q_norm_a, k_norm_a, q_norm_b, k_norm_b,
              w_branch_a, w_branch_b, w_out, rel_bias):
    h = x
    for l in range(DEPTH):
        h = hybrid_layer(h, norm_g[l], w_in[l], q_norm_a[l], k_norm_a[l], q_norm_b[l], k_norm_b[l],
                         w_branch_a[l], w_branch_b[l], w_out[l], rel_bias)
    return h
```

```python
import functools
import math

import numpy as np
import jax
import jax.numpy as jnp
from jax import lax
from jax.experimental import pallas as pl
from jax.experimental.pallas import tpu as pltpu

D_MODEL = 1024
HEADS = 8
HEAD_DIM = 64
WIDTH = HEADS * HEAD_DIM
BLK = 256
MOBA_TOPK = 3
IDX_HEADS = 4
IDX_DIM = 64
DSA_TOPK_MAX = 256
REL_BUCKETS = 32
REL_MAX_DIST = 128
EPS = 1e-6
NEG = -1e30
M_INIT = -1e20
INT_MIN = -2 ** 31
LANES = 128
VMEM_LIMIT = 56 * 1024 * 1024

F32 = jnp.float32
BF16 = jnp.bfloat16

C_QA, C_KA, C_QB, C_KB2 = 0, 512, 1024, 1536
C_VB2, C_KI4, C_VA, C_QI = 1664, 1792, 2048, 2560
C_GA, C_GB, C_MA, C_MB, C_WI = 2816, 3328, 3840, 4864, 5888
NP_COLS = 6016
NORM_COLS = 1664


def _bucket_starts():
    n = np.arange(0, 2 * REL_MAX_DIST + 2)
    max_exact = REL_BUCKETS // 2
    nf = np.maximum(n, 1).astype(np.float64)
    v = np.log(nf / max_exact) / math.log(REL_MAX_DIST / max_exact) * (REL_BUCKETS - max_exact)
    large = np.minimum(max_exact + np.trunc(v).astype(np.int64), REL_BUCKETS - 1)
    bucket = np.where(n < max_exact, n, large)
    frac = np.abs(v - np.round(v))[(n > max_exact) & (n < REL_MAX_DIST)]
    assert frac.min() > 1e-3
    return [int(np.argmax(bucket >= k)) for k in range(REL_BUCKETS)]


_STARTS = _bucket_starts()


def _dot_t(a, b):
    return lax.dot_general(a, b, (((1,), (1,)), ((), ())), preferred_element_type=F32)


def _dot(a, b):
    return jnp.dot(a, b, preferred_element_type=F32)


def _inproj_kernel(x_ref, gn_ref, w_ref, gh_ref,
                   qa_ref, ka_ref, qb_ref, kv_ref, va_ref, qi_ref,
                   sga_ref, sgb_ref, sma_ref, smb_ref, wi_ref, kmean_ref):
    x = x_ref[...]
    r = lax.rsqrt(jnp.mean(x * x, axis=-1, keepdims=True) + EPS)
    h = ((x * r) * gn_ref[...]).astype(BF16)

    gr = lax.broadcasted_iota(jnp.int32, (BLK, BLK), 0) // HEAD_DIM
    gc = lax.broadcasted_iota(jnp.int32, (BLK, BLK), 1) // HEAD_DIM
    group = jnp.where(gr == gc, 1.0, 0.0).astype(BF16)

    def proj(c0, n):
        return _dot(h, w_ref[:, c0:c0 + n])

    def headnorm(y, c0):
        n = y.shape[-1]
        sq = y * y
        hi = sq.astype(BF16)
        lo = (sq - hi.astype(F32)).astype(BF16)
        g = group[:n, :n]
        ss = _dot(hi, g) + _dot(lo, g)
        return (y * lax.rsqrt(ss * (1.0 / HEAD_DIM) + EPS)) * gh_ref[:, c0:c0 + n]

    scale = HEAD_DIM ** -0.5
    for c in range(0, WIDTH, BLK):
        qa_ref[:, c:c + BLK] = (headnorm(proj(C_QA + c, BLK), C_QA + c) * scale).astype(BF16)
        kn = headnorm(proj(C_KA + c, BLK), C_KA + c)
        ka_ref[:, c:c + BLK] = kn.astype(BF16)
        for t in range(kn.shape[0] // BLK):
            kmean_ref[t, :, c:c + BLK] = jnp.mean(kn[t * BLK:(t + 1) * BLK], axis=0, keepdims=True)
        qb_ref[:, c:c + BLK] = (headnorm(proj(C_QB + c, BLK), C_QB + c) * scale).astype(BF16)
    kv_ref[:, 0:128] = headnorm(proj(C_KB2, 128), C_KB2).astype(BF16)
    kv_ref[:, 128:256] = proj(C_VB2, 128).astype(BF16)
    kv_ref[:, 256:512] = proj(C_KI4, 256).astype(BF16)
    for c in range(0, WIDTH, BLK):
        va_ref[:, c:c + BLK] = proj(C_VA + c, BLK).astype(BF16)
        sga_ref[:, c:c + BLK] = jax.nn.silu(proj(C_GA + c, BLK))
        sgb_ref[:, c:c + BLK] = jax.nn.silu(proj(C_GB + c, BLK))
    qi_ref[...] = (proj(C_QI, 256) * (IDX_DIM ** -0.5)).astype(BF16)
    for c in range(0, D_MODEL, BLK):
        sma_ref[:, c:c + BLK] = jax.nn.sigmoid(proj(C_MA + c, BLK))
        smb_ref[:, c:c + BLK] = jax.nn.sigmoid(proj(C_MB + c, BLK))
    wi_ref[...] = proj(C_WI, 128) * (IDX_HEADS ** -0.5)


def _inproj(x2, gn, wp, gh, tm):
    m = x2.shape[0]
    row = lambda n: pl.BlockSpec((tm, n), lambda i: (i, 0))
    full = lambda a: pl.BlockSpec(a.shape, lambda i: (0,) * a.ndim)
    out_shapes = [
        jax.ShapeDtypeStruct((m, WIDTH), BF16),
        jax.ShapeDtypeStruct((m, WIDTH), BF16),
        jax.ShapeDtypeStruct((m, WIDTH), BF16),
        jax.ShapeDtypeStruct((m, WIDTH), BF16),
        jax.ShapeDtypeStruct((m, WIDTH), BF16),
        jax.ShapeDtypeStruct((m, 256), BF16),
        jax.ShapeDtypeStruct((m, WIDTH), F32),
        jax.ShapeDtypeStruct((m, WIDTH), F32),
        jax.ShapeDtypeStruct((m, D_MODEL), F32),
        jax.ShapeDtypeStruct((m, D_MODEL), F32),
        jax.ShapeDtypeStruct((m, 128), F32),
        jax.ShapeDtypeStruct((m // BLK, 1, WIDTH), F32),
    ]
    out_specs = [row(WIDTH)] * 5 + [row(256), row(WIDTH), row(WIDTH), row(D_MODEL), row(D_MODEL),
                                    row(128),
                                    pl.BlockSpec((tm // BLK, 1, WIDTH), lambda i: (i, 0, 0))]
    return pl.pallas_call(
        _inproj_kernel,
        grid=(m // tm,),
        in_specs=[row(D_MODEL), full(gn), full(wp), full(gh)],
        out_specs=out_specs,
        out_shape=out_shapes,
        compiler_params=pltpu.CompilerParams(
            dimension_semantics=("arbitrary",), vmem_limit_bytes=VMEM_LIMIT),
        name="inproj",
    )(x2, gn, wp, gh)


def _build_bias_tiles(rel_ref, bias_ref, head0):
    r = lax.broadcasted_iota(jnp.int32, (BLK, BLK), 0)
    c = lax.broadcasted_iota(jnp.int32, (BLK, BLK), 1)
    for kind in range(2):
        dist = r - c + BLK * kind
        for h in range(HEADS):
            t = jnp.full((BLK, BLK), rel_ref[0, head0 + h], F32)
            for k in range(1, REL_BUCKETS):
                t = jnp.where(dist >= _STARTS[k], rel_ref[k, head0 + h], t)
            if kind == 0:
                t = jnp.where(dist >= 0, t, NEG)
            bias_ref[3 * h + kind] = t
    for h in range(HEADS):
        bias_ref[3 * h + 2] = jnp.full((BLK, BLK), rel_ref[REL_BUCKETS - 1, head0 + h], F32)


def _flash_step(qm, kj, vj, add, carry):
    m, l, acc = carry
    s = _dot_t(qm, kj) + add
    m_new = jnp.maximum(m, jnp.max(s, axis=-1, keepdims=True))
    alpha = jnp.exp(m - m_new)
    p = jnp.exp(s - m_new)
    l = alpha * l + jnp.sum(p, axis=-1, keepdims=True)
    acc = alpha * acc + _dot(p.astype(BF16), vj)
    return m_new, l, acc


def _flash_init():
    return (jnp.full((BLK, 1), M_INIT, F32), jnp.zeros((BLK, 1), F32), jnp.zeros((BLK, LANES), F32))


def _moba_kernel(rel_ref, q_ref, k_ref, v_ref, kmean_ref, sg_ref, o_ref, bias_ref):
    b = pl.program_id(0)
    i = pl.program_id(1)
    nb = kmean_ref.shape[1]

    @pl.when((b == 0) & (i == 0))
    def _():
        _build_bias_tiles(rel_ref, bias_ref, 0)

    lane = lax.broadcasted_iota(jnp.int32, (BLK, LANES), 1)
    nidx = lax.broadcasted_iota(jnp.int32, (BLK, nb), 1)
    for p in range(HEADS // 2):
        cs = slice(LANES * p, LANES * (p + 1))
        q2 = q_ref[0, :, cs]
        km2 = kmean_ref[0, :, cs].astype(BF16)
        outs = []
        for hh in range(2):
            h = 2 * p + hh
            qm = jnp.where((lane >= HEAD_DIM) == bool(hh), q2, jnp.zeros_like(q2))
            g = jnp.where(nidx < i, _dot_t(qm, km2), -jnp.inf)
            bits = jnp.left_shift(1, jnp.full((BLK, 1), i, jnp.int32))
            for _ in range(MOBA_TOPK):
                mx = jnp.max(g, axis=-1, keepdims=True)
                am = jnp.min(jnp.where(g == mx, nidx, nb), axis=-1, keepdims=True)
                bits = bits | jnp.where(mx > -jnp.inf, jnp.left_shift(1, am), 0)
                g = jnp.where(nidx == am, -jnp.inf, g)

            def body(j, carry, qm=qm, bits=bits, h=h, cs=cs):
                rows = pl.ds(pl.multiple_of(j * BLK, BLK), BLK)
                kj = k_ref[0, rows, cs]
                vj = v_ref[0, rows, cs]
                selcol = jnp.where((jnp.right_shift(bits, j) & 1) == 1, 0.0, NEG)
                add = bias_ref[3 * h + jnp.minimum(i - j, 2)] + selcol
                return _flash_step(qm, kj, vj, add, carry)

            _, l, acc = lax.fori_loop(0, i + 1, body, _flash_init())
            outs.append(acc / l)
        o2 = jnp.where(lane < HEAD_DIM, outs[0], outs[1])
        o_ref[0, :, cs] = (o2 * sg_ref[0, :, cs]).astype(BF16)


def _moba(rel_bias, qa, ka, va, kmean, sga):
    bsz, s, _ = qa.shape
    nb = s // BLK
    qspec = pl.BlockSpec((1, BLK, WIDTH), lambda b, i: (b, i, 0))
    kspec = pl.BlockSpec((1, s, WIDTH), lambda b, i: (b, 0, 0))
    return pl.pallas_call(
        _moba_kernel,
        grid=(bsz, nb),
        in_specs=[pl.BlockSpec(memory_space=pltpu.SMEM), qspec, kspec, kspec,
                  pl.BlockSpec((1, nb, WIDTH), lambda b, i: (b, 0, 0)), qspec],
        out_specs=qspec,
        out_shape=jax.ShapeDtypeStruct((bsz, s, WIDTH), BF16),
        scratch_shapes=[pltpu.VMEM((3 * HEADS, BLK, BLK), F32)],
        compiler_params=pltpu.CompilerParams(
            dimension_semantics=("arbitrary", "arbitrary"), vmem_limit_bytes=VMEM_LIMIT),
        name="moba",
    )(rel_bias, qa, ka, va, kmean, sga)


def _dsa_kernel(rel_ref, q_ref, qi_ref, wi_ref, kv_ref, sg_ref, o_ref,
                bias_ref, key_ref, madd_ref, *, topk):
    b = pl.program_id(0)
    i = pl.program_id(1)
    nblk = i + 1

    @pl.when((b == 0) & (i == 0))
    def _():
        _build_bias_tiles(rel_ref, bias_ref, HEADS)

    rpos = lax.broadcasted_iota(jnp.int32, (BLK, BLK), 0)
    cpos = lax.broadcasted_iota(jnp.int32, (BLK, BLK), 1)

    def valid_mask(j):
        return (j * BLK + cpos) <= (i * BLK + rpos)

    qi = qi_ref[0]
    wi = wi_ref[0]
    qms = [jnp.where((cpos // IDX_DIM) == h, qi, jnp.zeros_like(qi)) for h in range(IDX_HEADS)]
    whs = [wi[:, h:h + 1] for h in range(IDX_HEADS)]

    def score_body(j, _):
        rows = pl.ds(pl.multiple_of(j * BLK, BLK), BLK)
        ki4 = kv_ref[0, rows, 256:512]
        score = jnp.zeros((BLK, BLK), F32)
        for h in range(IDX_HEADS):
            score = score + jnp.maximum(_dot_t(qms[h], ki4), 0.0) * whs[h]
        bits = lax.bitcast_convert_type(score, jnp.int32)
        key = bits ^ (jnp.right_shift(bits, 31) & 0x7FFFFFFF)
        key_ref[j] = jnp.where(valid_mask(j), key, INT_MIN)
        return 0

    lax.fori_loop(0, nblk, score_body, 0)

    def count(pred_fn):
        def cb(j, acc):
            k = key_ref[j]
            hit = jnp.where(pred_fn(k), 1, 0)
            return acc + hit[:, :LANES] + hit[:, LANES:]
        acc = lax.fori_loop(0, nblk, cb, jnp.zeros((BLK, LANES), jnp.int32))
        return jnp.sum(acc, axis=-1, keepdims=True)

    zero = jnp.zeros((BLK, 1), jnp.int32)
    t0 = jnp.where(count(lambda k: k >= zero) >= topk, zero, INT_MIN)

    def bit_body(t, thr):
        cand = thr + jnp.left_shift(1, 30 - t)
        return jnp.where(count(lambda k: k >= cand) >= topk, cand, thr)

    thr = lax.fori_loop(0, 31, bit_body, t0)
    need = (topk - count(lambda k: k > thr)).astype(F32)

    upper = jnp.where(rpos <= cpos, 1.0, 0.0).astype(BF16)

    def mask_body(j, carry):
        k = key_ref[j]
        eq = k == thr
        pref = _dot(jnp.where(eq, 1.0, 0.0).astype(BF16), upper)
        sel = (k > thr) | (eq & ((carry + pref) <= need))
        madd_ref[j] = jnp.where(sel & valid_mask(j), 0.0, NEG)
        return carry + pref[:, BLK - 1:BLK]

    lax.fori_loop(0, nblk, mask_body, jnp.zeros((BLK, 1), F32))

    lane = lax.broadcasted_iota(jnp.int32, (BLK, LANES), 1)
    for p in range(HEADS // 2):
        cs = slice(LANES * p, LANES * (p + 1))
        q2 = q_ref[0, :, cs]
        outs = []
        for hh in range(2):
            h = 2 * p + hh
            qm = jnp.where((lane >= HEAD_DIM) == bool(hh), q2, jnp.zeros_like(q2))

            def body(j, carry, qm=qm, h=h):
                rows = pl.ds(pl.multiple_of(j * BLK, BLK), BLK)
                kj = kv_ref[0, rows, 0:128]
                vj = kv_ref[0, rows, 128:256]
                add = bias_ref[3 * h + jnp.minimum(i - j, 2)] + madd_ref[j]
                return _flash_step(qm, kj, vj, add, carry)

            _, l, acc = lax.fori_loop(0, nblk, body, _flash_init())
            outs.append(acc / l)
        o2 = jnp.where(lane < HEAD_DIM, outs[0], outs[1])
        o_ref[0, :, cs] = (o2 * sg_ref[0, :, cs]).astype(BF16)


def _dsa(rel_bias, qb, qi, wi, kv, sgb):
    bsz, s, _ = qb.shape
    nb = s // BLK
    topk = min(DSA_TOPK_MAX, s // 4)
    qspec = pl.BlockSpec((1, BLK, WIDTH), lambda b, i: (b, i, 0))
    return pl.pallas_call(
        functools.partial(_dsa_kernel, topk=topk),
        grid=(bsz, nb),
        in_specs=[pl.BlockSpec(memory_space=pltpu.SMEM), qspec,
                  pl.BlockSpec((1, BLK, 256), lambda b, i: (b, i, 0)),
                  pl.BlockSpec((1, BLK, 128), lambda b, i: (b, i, 0)),
                  pl.BlockSpec((1, s, WIDTH), lambda b, i: (b, 0, 0)), qspec],
        out_specs=qspec,
        out_shape=jax.ShapeDtypeStruct((bsz, s, WIDTH), BF16),
        scratch_shapes=[pltpu.VMEM((3 * HEADS, BLK, BLK), F32),
                        pltpu.VMEM((nb, BLK, BLK), jnp.int32),
                        pltpu.VMEM((nb, BLK, BLK), F32)],
        compiler_params=pltpu.CompilerParams(
            dimension_semantics=("arbitrary", "arbitrary"), vmem_limit_bytes=VMEM_LIMIT),
        name="dsa",
    )(rel_bias, qb, qi, wi, kv, sgb)


def _outproj_kernel(x_ref, ya_ref, yb_ref, sma_ref, smb_ref, wa_ref, wb_ref, wo_ref, o_ref):
    merged = (sma_ref[...] * _dot(ya_ref[...], wa_ref[...])
              + smb_ref[...] * _dot(yb_ref[...], wb_ref[...]))
    o_ref[...] = x_ref[...] + _dot(merged.astype(BF16), wo_ref[...])


def _outproj(x2, ya, yb, sma, smb, wa, wb, wo, tm):
    m = x2.shape[0]
    row = lambda n: pl.BlockSpec((tm, n), lambda i: (i, 0))
    full = lambda a: pl.BlockSpec(a.shape, lambda i: (0,) * a.ndim)
    return pl.pallas_call(
        _outproj_kernel,
        grid=(m // tm,),
        in_specs=[row(D_MODEL), row(WIDTH), row(WIDTH), row(D_MODEL), row(D_MODEL),
                  full(wa), full(wb), full(wo)],
        out_specs=row(D_MODEL),
        out_shape=jax.ShapeDtypeStruct((m, D_MODEL), F32),
        compiler_params=pltpu.CompilerParams(
            dimension_semantics=("arbitrary",), vmem_limit_bytes=VMEM_LIMIT),
        name="outproj",
    )(x2, ya, yb, sma, smb, wa, wb, wo)


def _pack_w_in(w):
    o = 0
    parts = {}
    for name, n in (("qa", WIDTH), ("ka", WIDTH), ("va", WIDTH), ("ga", WIDTH), ("qb", WIDTH),
                    ("kb", HEAD_DIM), ("vb", HEAD_DIM), ("gb", WIDTH), ("qi", IDX_HEADS * IDX_DIM),
                    ("ki", IDX_DIM), ("wi", IDX_HEADS), ("ma", D_MODEL), ("mb", D_MODEL)):
        parts[name] = w[:, o:o + n]
        o += n
    pad = jnp.zeros((w.shape[0], LANES - IDX_HEADS), w.dtype)
    cols = [parts["qa"], parts["ka"], parts["qb"], parts["kb"], parts["kb"],
            parts["vb"], parts["vb"], parts["ki"], parts["ki"], parts["ki"], parts["ki"],
            parts["va"], parts["qi"], parts["ga"], parts["gb"], parts["ma"], parts["mb"],
            parts["wi"], pad]
    wp = jnp.concatenate(cols, axis=1).astype(BF16)
    assert wp.shape[1] == NP_COLS
    return wp


def _layer(x2, bsz, s, gn, w_in, qn_a, kn_a, qn_b, kn_b, w_br_a, w_br_b, w_out, rel_bias, tm):
    wp = _pack_w_in(w_in)
    gh = jnp.concatenate([jnp.tile(qn_a, HEADS), jnp.tile(kn_a, HEADS),
                          jnp.tile(qn_b, HEADS), jnp.tile(kn_b, 2)])[None, :]
    (qa, ka, qb, kv, va, qi, sga, sgb, sma, smb, wi, kmean) = _inproj(x2, gn[None, :], wp, gh, tm)
    r3 = lambda a: a.reshape(bsz, s, a.shape[-1])
    ya = _moba(rel_bias, r3(qa), r3(ka), r3(va), kmean.reshape(bsz, s // BLK, WIDTH), r3(sga))
    yb = _dsa(rel_bias, r3(qb), r3(qi), r3(wi), r3(kv), r3(sgb))
    return _outproj(x2, ya.reshape(-1, WIDTH), yb.reshape(-1, WIDTH), sma, smb,
                    w_br_a.astype(BF16), w_br_b.astype(BF16), w_out.astype(BF16), tm)


def kernel(x, norm_g, w_in, q_norm_a, k_norm_a, q_norm_b, k_norm_b,
           w_branch_a, w_branch_b, w_out, rel_bias):
    bsz, s, d = x.shape
    assert d == D_MODEL and s % BLK == 0 and s // BLK <= 31
    tm = BLK
    h = x.reshape(bsz * s, d)
    for l in range(norm_g.shape[0]):
        h = _layer(h, bsz, s, norm_g[l], w_in[l], q_norm_a[l], k_norm_a[l], q_norm_b[l],
                   k_norm_b[l], w_branch_a[l], w_branch_b[l], w_out[l], rel_bias, tm)
    return h.reshape(bsz, s, d)
```

```python
import functools
import math

import numpy as np
import jax
import jax.numpy as jnp
from jax import lax
from jax.experimental import pallas as pl
from jax.experimental.pallas import tpu as pltpu

D_MODEL = 1024
HEADS = 8
HEAD_DIM = 64
WIDTH = HEADS * HEAD_DIM
BLK = 256
MOBA_TOPK = 3
IDX_HEADS = 4
IDX_DIM = 64
DSA_TOPK_MAX = 256
REL_BUCKETS = 32
REL_MAX_DIST = 128
EPS = 1e-6
NEG = -1e30
M_INIT = -1e20
INT_MIN = -2 ** 31
LANES = 128
VMEM_LIMIT = 56 * 1024 * 1024

F32 = jnp.float32
BF16 = jnp.bfloat16

CN_KA, CN_KB2, CN_KI4, CN_GA, CN_GB, CN_MA, CN_MB = 0, 512, 640, 896, 1408, 1920, 2944
N_NAT = 3968
RT_QA, RT_QB, RT_VA, RT_QI, RT_VB, RT_WI = 0, 512, 1024, 1536, 1792, 1856
N_TR = 1872
WI_ROWS = 8


def _bucket_starts():
    n = np.arange(0, 2 * REL_MAX_DIST + 2)
    max_exact = REL_BUCKETS // 2
    nf = np.maximum(n, 1).astype(np.float64)
    v = np.log(nf / max_exact) / math.log(REL_MAX_DIST / max_exact) * (REL_BUCKETS - max_exact)
    large = np.minimum(max_exact + np.trunc(v).astype(np.int64), REL_BUCKETS - 1)
    bucket = np.where(n < max_exact, n, large)
    frac = np.abs(v - np.round(v))[(n > max_exact) & (n < REL_MAX_DIST)]
    assert frac.min() > 1e-3
    return [int(np.argmax(bucket >= k)) for k in range(REL_BUCKETS)]


_STARTS = _bucket_starts()


def _dot(a, b):
    return jnp.dot(a, b, preferred_element_type=F32)


def _inproj_kernel(x_ref, gn_ref, wn_ref, wt_ref, ghn_ref, ght_ref,
                   ka_ref, kn_ref, sga_ref, sgb_ref, sma_ref, smb_ref, kmean_ref,
                   qat_ref, qbt_ref, vat_ref, qit_ref, vbt_ref, wit_ref):
    x = x_ref[...]
    r = lax.rsqrt(jnp.mean(x * x, axis=-1, keepdims=True) + EPS)
    hf = (x * r) * gn_ref[...]
    h = hf.astype(BF16)
    ht = hf.T.astype(BF16)

    gr = lax.broadcasted_iota(jnp.int32, (BLK, BLK), 0) // HEAD_DIM
    gc = lax.broadcasted_iota(jnp.int32, (BLK, BLK), 1) // HEAD_DIM
    group = jnp.where(gr == gc, 1.0, 0.0).astype(BF16)

    def proj(c0, n):
        return _dot(h, wn_ref[:, c0:c0 + n])

    def proj_t(r0, n):
        return _dot(wt_ref[r0:r0 + n, :], ht)

    def headnorm(y, c0):
        n = y.shape[-1]
        sq = y * y
        hi = sq.astype(BF16)
        lo = (sq - hi.astype(F32)).astype(BF16)
        g = group[:n, :n]
        ss = _dot(hi, g) + _dot(lo, g)
        return (y * lax.rsqrt(ss * (1.0 / HEAD_DIM) + EPS)) * ghn_ref[:, c0:c0 + n]

    def headnorm_t(yt, r0):
        ss = jnp.sum(yt * yt, axis=0, keepdims=True)
        return (yt * lax.rsqrt(ss * (1.0 / HEAD_DIM) + EPS)) * ght_ref[r0:r0 + HEAD_DIM, :]

    for c in range(0, WIDTH, BLK):
        kn = headnorm(proj(CN_KA + c, BLK), CN_KA + c)
        ka_ref[:, c:c + BLK] = kn.astype(BF16)
        for t in range(kn.shape[0] // BLK):
            kmean_ref[t, :, c:c + BLK] = jnp.mean(kn[t * BLK:(t + 1) * BLK], axis=0, keepdims=True)
        sga_ref[:, c:c + BLK] = jax.nn.silu(proj(CN_GA + c, BLK))
        sgb_ref[:, c:c + BLK] = jax.nn.silu(proj(CN_GB + c, BLK))
    kn_ref[:, 0:128] = headnorm(proj(CN_KB2, 128), CN_KB2).astype(BF16)
    kn_ref[:, 128:384] = proj(CN_KI4, 256).astype(BF16)
    for c in range(0, D_MODEL, BLK):
        sma_ref[:, c:c + BLK] = jax.nn.sigmoid(proj(CN_MA + c, BLK))
        smb_ref[:, c:c + BLK] = jax.nn.sigmoid(proj(CN_MB + c, BLK))

    scale = HEAD_DIM ** -0.5
    for hd in range(HEADS):
        rs = slice(hd * HEAD_DIM, (hd + 1) * HEAD_DIM)
        qat_ref[0, rs, :] = (headnorm_t(proj_t(RT_QA + hd * HEAD_DIM, HEAD_DIM), RT_QA + hd * HEAD_DIM)
                             * scale).astype(BF16)
        qbt_ref[0, rs, :] = (headnorm_t(proj_t(RT_QB + hd * HEAD_DIM, HEAD_DIM), RT_QB + hd * HEAD_DIM)
                             * scale).astype(BF16)
    for c in range(0, WIDTH, BLK):
        vat_ref[0, c:c + BLK, :] = proj_t(RT_VA + c, BLK).astype(BF16)
    qit_ref[0] = (proj_t(RT_QI, 256) * (IDX_DIM ** -0.5)).astype(BF16)
    vbt_ref[0] = proj_t(RT_VB, HEAD_DIM).astype(BF16)
    wit_ref[0] = proj_t(RT_WI, 16)[:WI_ROWS] * (IDX_HEADS ** -0.5)


def _inproj(x2, gn, wn, wt, ghn, ght):
    m = x2.shape[0]
    nt = m // BLK
    row = lambda n: pl.BlockSpec((BLK, n), lambda i: (i, 0))
    full = lambda a: pl.BlockSpec(a.shape, lambda i: (0,) * a.ndim)
    tile = lambda n: pl.BlockSpec((1, n, BLK), lambda i: (i, 0, 0))
    out_shapes = [
        jax.ShapeDtypeStruct((m, WIDTH), BF16),
        jax.ShapeDtypeStruct((m, 384), BF16),
        jax.ShapeDtypeStruct((m, WIDTH), F32),
        jax.ShapeDtypeStruct((m, WIDTH), F32),
        jax.ShapeDtypeStruct((m, D_MODEL), F32),
        jax.ShapeDtypeStruct((m, D_MODEL), F32),
        jax.ShapeDtypeStruct((nt, 1, WIDTH), F32),
        jax.ShapeDtypeStruct((nt, WIDTH, BLK), BF16),
        jax.ShapeDtypeStruct((nt, WIDTH, BLK), BF16),
        jax.ShapeDtypeStruct((nt, WIDTH, BLK), BF16),
        jax.ShapeDtypeStruct((nt, 256, BLK), BF16),
        jax.ShapeDtypeStruct((nt, HEAD_DIM, BLK), BF16),
        jax.ShapeDtypeStruct((nt, WI_ROWS, BLK), F32),
    ]
    out_specs = [row(WIDTH), row(384), row(WIDTH), row(WIDTH), row(D_MODEL), row(D_MODEL),
                 pl.BlockSpec((1, 1, WIDTH), lambda i: (i, 0, 0)),
                 tile(WIDTH), tile(WIDTH), tile(WIDTH), tile(256), tile(HEAD_DIM), tile(WI_ROWS)]
    return pl.pallas_call(
        _inproj_kernel,
        grid=(nt,),
        in_specs=[row(D_MODEL), full(gn), full(wn), full(wt), full(ghn), full(ght)],
        out_specs=out_specs,
        out_shape=out_shapes,
        compiler_params=pltpu.CompilerParams(
            dimension_semantics=("arbitrary",), vmem_limit_bytes=VMEM_LIMIT),
        name="inproj",
    )(x2, gn, wn, wt, ghn, ght)


def _build_bias_tiles(rel_ref, bias_ref, head0):
    kpos = lax.broadcasted_iota(jnp.int32, (BLK, BLK), 0)
    qpos = lax.broadcasted_iota(jnp.int32, (BLK, BLK), 1)
    for kind in range(2):
        dist = qpos - kpos + BLK * kind
        for h in range(HEADS):
            t = jnp.full((BLK, BLK), rel_ref[0, head0 + h], F32)
            for k in range(1, REL_BUCKETS):
                t = jnp.where(dist >= _STARTS[k], rel_ref[k, head0 + h], t)
            if kind == 0:
                t = jnp.where(dist >= 0, t, NEG)
            bias_ref[3 * h + kind] = t
    for h in range(HEADS):
        bias_ref[3 * h + 2] = jnp.full((BLK, BLK), rel_ref[REL_BUCKETS - 1, head0 + h], F32)


def _store_masked_queries(qt_ref, qm_ref):
    rowh = lax.broadcasted_iota(jnp.int32, (LANES, BLK), 0)
    for p in range(HEADS // 2):
        q2 = qt_ref[0, LANES * p:LANES * (p + 1), :]
        for hh in range(2):
            qm_ref[2 * p + hh] = jnp.where((rowh >= HEAD_DIM) == bool(hh), q2, jnp.zeros_like(q2))


def _flash_block(score_fn, pv_fn, s_ref, acc_ref, ms, ls, sels=None):
    m_news = []
    for h in range(HEADS):
        s = score_fn(h)
        s_ref[h] = s
        m_news.append(jnp.maximum(ms[h], jnp.max(s, axis=0, keepdims=True)))
    ms_out, ls_out = [], []
    for h in range(HEADS):
        alpha = jnp.exp(ms[h] - m_news[h])
        p = jnp.exp(s_ref[h] - m_news[h])
        l_new = alpha * ls[h] + jnp.sum(p, axis=0, keepdims=True)
        acc_new = alpha * acc_ref[h] + pv_fn(h, p.astype(BF16))
        if sels is None:
            acc_ref[h] = acc_new
            ms_out.append(m_news[h])
            ls_out.append(l_new)
        else:
            acc_ref[h] = jnp.where(sels[h], acc_new, acc_ref[h])
            ms_out.append(jnp.where(sels[h], m_news[h], ms[h]))
            ls_out.append(jnp.where(sels[h], l_new, ls[h]))
    return tuple(ms_out), tuple(ls_out)


def _write_output(acc_ref, ls, sg_ref, o_ref):
    for p in range(HEADS // 2):
        ot = jnp.concatenate([acc_ref[2 * p] / ls[2 * p], acc_ref[2 * p + 1] / ls[2 * p + 1]], axis=0)
        cs = slice(LANES * p, LANES * (p + 1))
        o_ref[0, :, cs] = (ot.T * sg_ref[0, :, cs]).astype(BF16)


def _moba_kernel(rel_ref, qt_ref, k_ref, vt_ref, kmean_ref, sg_ref, o_ref,
                 bias_ref, qm_ref, acc_ref, s_ref):
    b = pl.program_id(0)
    i = pl.program_id(1)
    nb = kmean_ref.shape[1]

    @pl.when((b == 0) & (i == 0))
    def _():
        _build_bias_tiles(rel_ref, bias_ref, 0)

    _store_masked_queries(qt_ref, qm_ref)
    acc_ref[...] = jnp.zeros_like(acc_ref)

    nidx = lax.broadcasted_iota(jnp.int32, (nb, BLK), 0)
    sel_bits = []
    for h in range(HEADS):
        cs = slice(LANES * (h // 2), LANES * (h // 2 + 1))
        gate = _dot(kmean_ref[0, :, cs].astype(BF16), qm_ref[h])
        g = jnp.where(nidx < i, gate, -jnp.inf)
        bits = jnp.left_shift(1, jnp.full((1, BLK), i, jnp.int32))
        for _ in range(MOBA_TOPK):
            mx = jnp.max(g, axis=0, keepdims=True)
            am = jnp.min(jnp.where(g == mx, nidx, nb), axis=0, keepdims=True)
            bits = bits | jnp.where(mx > -jnp.inf, jnp.left_shift(1, am), 0)
            g = jnp.where(nidx == am, -jnp.inf, g)
        sel_bits.append(bits)

    def body(j, carry):
        ms, ls = carry
        rows = pl.ds(pl.multiple_of(j * BLK, BLK), BLK)
        kind = jnp.minimum(i - j, 2)

        def score(h):
            cs = slice(LANES * (h // 2), LANES * (h // 2 + 1))
            return _dot(k_ref[0, rows, cs], qm_ref[h]) + bias_ref[3 * h + kind]

        def pv(h, p):
            return _dot(vt_ref[0, j, h * HEAD_DIM:(h + 1) * HEAD_DIM, :], p)

        sels = [(jnp.right_shift(sel_bits[h], j) & 1) == 1 for h in range(HEADS)]
        return _flash_block(score, pv, s_ref, acc_ref, ms, ls, sels)

    init = (tuple(jnp.full((1, BLK), M_INIT, F32) for _ in range(HEADS)),
            tuple(jnp.zeros((1, BLK), F32) for _ in range(HEADS)))
    _, ls = lax.fori_loop(0, i + 1, body, init)
    _write_output(acc_ref, ls, sg_ref, o_ref)


def _moba(rel_bias, qat, ka, vat, kmean, sga):
    bsz, s, _ = ka.shape
    nb = s // BLK
    qspec = pl.BlockSpec((1, BLK, WIDTH), lambda b, i: (b, i, 0))
    return pl.pallas_call(
        _moba_kernel,
        grid=(bsz, nb),
        in_specs=[pl.BlockSpec(memory_space=pltpu.SMEM),
                  pl.BlockSpec((1, WIDTH, BLK), lambda b, i: (b * nb + i, 0, 0)),
                  pl.BlockSpec((1, s, WIDTH), lambda b, i: (b, 0, 0)),
                  pl.BlockSpec((1, nb, WIDTH, BLK), lambda b, i: (b, 0, 0, 0)),
                  pl.BlockSpec((1, nb, WIDTH), lambda b, i: (b, 0, 0)), qspec],
        out_specs=qspec,
        out_shape=jax.ShapeDtypeStruct((bsz, s, WIDTH), BF16),
        scratch_shapes=[pltpu.VMEM((3 * HEADS, BLK, BLK), F32),
                        pltpu.VMEM((HEADS, LANES, BLK), BF16),
                        pltpu.VMEM((HEADS, HEAD_DIM, BLK), F32),
                        pltpu.VMEM((HEADS, BLK, BLK), F32)],
        compiler_params=pltpu.CompilerParams(
            dimension_semantics=("arbitrary", "arbitrary"), vmem_limit_bytes=VMEM_LIMIT),
        name="moba",
    )(rel_bias, qat, ka, vat.reshape(bsz, nb, WIDTH, BLK), kmean, sga)


def _dsa_kernel(rel_ref, qt_ref, qit_ref, wit_ref, kn_ref, vt_ref, sg_ref, o_ref,
                bias_ref, qm_ref, acc_ref, s_ref, key_ref, madd_ref, *, topk):
    b = pl.program_id(0)
    i = pl.program_id(1)
    nblk = i + 1

    @pl.when((b == 0) & (i == 0))
    def _():
        _build_bias_tiles(rel_ref, bias_ref, HEADS)

    _store_masked_queries(qt_ref, qm_ref)
    acc_ref[...] = jnp.zeros_like(acc_ref)

    kpos = lax.broadcasted_iota(jnp.int32, (BLK, BLK), 0)
    qpos = lax.broadcasted_iota(jnp.int32, (BLK, BLK), 1)

    def valid_mask(j):
        return (j * BLK + kpos) <= (i * BLK + qpos)

    qit = qit_ref[0]
    wit = wit_ref[0]
    qims = [jnp.where((kpos // IDX_DIM) == h, qit, jnp.zeros_like(qit)) for h in range(IDX_HEADS)]
    whs = [wit[h:h + 1, :] for h in range(IDX_HEADS)]

    def score_body(j, _):
        rows = pl.ds(pl.multiple_of(j * BLK, BLK), BLK)
        ki4 = kn_ref[0, rows, 128:384]
        score = jnp.zeros((BLK, BLK), F32)
        for h in range(IDX_HEADS):
            score = score + jnp.maximum(_dot(ki4, qims[h]), 0.0) * whs[h]
        bits = lax.bitcast_convert_type(score, jnp.int32)
        key = bits ^ (jnp.right_shift(bits, 31) & 0x7FFFFFFF)
        key_ref[j] = jnp.where(valid_mask(j), key, INT_MIN)
        return 0

    lax.fori_loop(0, nblk, score_body, 0)

    def count(pred_fn):
        def cb(j, acc):
            hit = jnp.where(pred_fn(key_ref[j]), 1, 0)
            return acc + jnp.sum(hit.reshape(BLK // 8, 8, BLK), axis=0)
        acc = lax.fori_loop(0, nblk, cb, jnp.zeros((8, BLK), jnp.int32))
        return jnp.sum(acc, axis=0, keepdims=True)

    zero = jnp.zeros((1, BLK), jnp.int32)
    t0 = jnp.where(count(lambda k: k >= zero) >= topk, zero, INT_MIN)

    def bit_body(t, thr):
        cand = thr + jnp.left_shift(1, 30 - t)
        return jnp.where(count(lambda k: k >= cand) >= topk, cand, thr)

    thr = lax.fori_loop(0, 31, bit_body, t0)
    need = (topk - count(lambda k: k > thr)).astype(F32)

    lower = jnp.where(qpos <= kpos, 1.0, 0.0).astype(BF16)

    def mask_body(j, carry):
        k = key_ref[j]
        eq = k == thr
        pref = _dot(lower, jnp.where(eq, 1.0, 0.0).astype(BF16))
        sel = (k > thr) | (eq & ((carry + pref) <= need))
        madd_ref[j] = jnp.where(sel & valid_mask(j), 0.0, NEG)
        return carry + pref[BLK - 1:BLK, :]

    lax.fori_loop(0, nblk, mask_body, jnp.zeros((1, BLK), F32))

    def body(j, carry):
        ms, ls = carry
        rows = pl.ds(pl.multiple_of(j * BLK, BLK), BLK)
        kind = jnp.minimum(i - j, 2)
        kj = kn_ref[0, rows, 0:128]
        vj = vt_ref[0, j]

        def score(h):
            return _dot(kj, qm_ref[h]) + bias_ref[3 * h + kind] + madd_ref[j]

        return _flash_block(score, lambda h, p: _dot(vj, p), s_ref, acc_ref, ms, ls)

    init = (tuple(jnp.full((1, BLK), M_INIT, F32) for _ in range(HEADS)),
            tuple(jnp.zeros((1, BLK), F32) for _ in range(HEADS)))
    _, ls = lax.fori_loop(0, nblk, body, init)
    _write_output(acc_ref, ls, sg_ref, o_ref)


def _dsa(rel_bias, qbt, qit, wit, kn, vbt, sgb):
    bsz, s, _ = kn.shape
    nb = s // BLK
    topk = min(DSA_TOPK_MAX, s // 4)
    qspec = pl.BlockSpec((1, BLK, WIDTH), lambda b, i: (b, i, 0))
    tile = lambda n: pl.BlockSpec((1, n, BLK), lambda b, i: (b * nb + i, 0, 0))
    return pl.pallas_call(
        functools.partial(_dsa_kernel, topk=topk),
        grid=(bsz, nb),
        in_specs=[pl.BlockSpec(memory_space=pltpu.SMEM), tile(WIDTH), tile(256), tile(WI_ROWS),
                  pl.BlockSpec((1, s, 384), lambda b, i: (b, 0, 0)),
                  pl.BlockSpec((1, nb, HEAD_DIM, BLK), lambda b, i: (b, 0, 0, 0)), qspec],
        out_specs=qspec,
        out_shape=jax.ShapeDtypeStruct((bsz, s, WIDTH), BF16),
        scratch_shapes=[pltpu.VMEM((3 * HEADS, BLK, BLK), F32),
                        pltpu.VMEM((HEADS, LANES, BLK), BF16),
                        pltpu.VMEM((HEADS, HEAD_DIM, BLK), F32),
                        pltpu.VMEM((HEADS, BLK, BLK), F32),
                        pltpu.VMEM((nb, BLK, BLK), jnp.int32),
                        pltpu.VMEM((nb, BLK, BLK), F32)],
        compiler_params=pltpu.CompilerParams(
            dimension_semantics=("arbitrary", "arbitrary"), vmem_limit_bytes=VMEM_LIMIT),
        name="dsa",
    )(rel_bias, qbt, qit, wit, kn, vbt.reshape(bsz, nb, HEAD_DIM, BLK), sgb)


def _outproj_kernel(x_ref, ya_ref, yb_ref, sma_ref, smb_ref, wa_ref, wb_ref, wo_ref, o_ref):
    merged = (sma_ref[...] * _dot(ya_ref[...], wa_ref[...])
              + smb_ref[...] * _dot(yb_ref[...], wb_ref[...]))
    o_ref[...] = x_ref[...] + _dot(merged.astype(BF16), wo_ref[...])


def _outproj(x2, ya, yb, sma, smb, wa, wb, wo):
    m = x2.shape[0]
    row = lambda n: pl.BlockSpec((BLK, n), lambda i: (i, 0))
    full = lambda a: pl.BlockSpec(a.shape, lambda i: (0,) * a.ndim)
    return pl.pallas_call(
        _outproj_kernel,
        grid=(m // BLK,),
        in_specs=[row(D_MODEL), row(WIDTH), row(WIDTH), row(D_MODEL), row(D_MODEL),
                  full(wa), full(wb), full(wo)],
        out_specs=row(D_MODEL),
        out_shape=jax.ShapeDtypeStruct((m, D_MODEL), F32),
        compiler_params=pltpu.CompilerParams(
            dimension_semantics=("arbitrary",), vmem_limit_bytes=VMEM_LIMIT),
        name="outproj",
    )(x2, ya, yb, sma, smb, wa, wb, wo)


def _pack_w_in(w):
    o = 0
    parts = {}
    for name, n in (("qa", WIDTH), ("ka", WIDTH), ("va", WIDTH), ("ga", WIDTH), ("qb", WIDTH),
                    ("kb", HEAD_DIM), ("vb", HEAD_DIM), ("gb", WIDTH), ("qi", IDX_HEADS * IDX_DIM),
                    ("ki", IDX_DIM), ("wi", IDX_HEADS), ("ma", D_MODEL), ("mb", D_MODEL)):
        parts[name] = w[:, o:o + n]
        o += n
    wn = jnp.concatenate([parts["ka"], parts["kb"], parts["kb"],
                          parts["ki"], parts["ki"], parts["ki"], parts["ki"],
                          parts["ga"], parts["gb"], parts["ma"], parts["mb"]], axis=1).astype(BF16)
    pad = jnp.zeros((w.shape[0], N_TR - RT_WI - IDX_HEADS), w.dtype)
    wt = jnp.concatenate([parts["qa"], parts["qb"], parts["va"], parts["qi"], parts["vb"],
                          parts["wi"], pad], axis=1).T.astype(BF16)
    assert wn.shape[1] == N_NAT and wt.shape[0] == N_TR
    return wn, wt


def _layer(x2, bsz, s, gn, w_in, qn_a, kn_a, qn_b, kn_b, w_br_a, w_br_b, w_out, rel_bias):
    wn, wt = _pack_w_in(w_in)
    ghn = jnp.concatenate([jnp.tile(kn_a, HEADS), jnp.tile(kn_b, 2)])[None, :]
    ght = jnp.concatenate([jnp.tile(qn_a, HEADS), jnp.tile(qn_b, HEADS)])[:, None]
    (ka, kn, sga, sgb, sma, smb, kmean, qat, qbt, vat, qit, vbt, wit) = _inproj(
        x2, gn[None, :], wn, wt, ghn, ght)
    r3 = lambda a: a.reshape(bsz, s, a.shape[-1])
    ya = _moba(rel_bias, qat, r3(ka), vat, kmean.reshape(bsz, s // BLK, WIDTH), r3(sga))
    yb = _dsa(rel_bias, qbt, qit, wit, r3(kn), vbt, r3(sgb))
    return _outproj(x2, ya.reshape(-1, WIDTH), yb.reshape(-1, WIDTH), sma, smb,
                    w_br_a.astype(BF16), w_br_b.astype(BF16), w_out.astype(BF16))


def kernel(x, norm_g, w_in, q_norm_a, k_norm_a, q_norm_b, k_norm_b,
           w_branch_a, w_branch_b, w_out, rel_bias):
    bsz, s, d = x.shape
    assert d == D_MODEL and s % BLK == 0 and s // BLK <= 31
    h = x.reshape(bsz * s, d)
    for l in range(norm_g.shape[0]):
        h = _layer(h, bsz, s, norm_g[l], w_in[l], q_norm_a[l], k_norm_a[l], q_norm_b[l],
                   k_norm_b[l], w_branch_a[l], w_branch_b[l], w_out[l], rel_bias)
    return h.reshape(bsz, s, d)
```

```python
import functools
import math

import numpy as np
import jax
import jax.numpy as jnp
from jax import lax
from jax.experimental import pallas as pl
from jax.experimental.pallas import tpu as pltpu

D_MODEL = 1024
HEADS = 8
HEAD_DIM = 64
WIDTH = HEADS * HEAD_DIM
BLK = 256
MOBA_TOPK = 3
IDX_HEADS = 4
IDX_DIM = 64
DSA_TOPK_MAX = 256
REL_BUCKETS = 32
REL_MAX_DIST = 128
EPS = 1e-6
NEG = -1e30
M_INIT = -1e20
INT_MIN = -2 ** 31
INT16_MIN = -2 ** 15
PACK16 = 16
LANES = 128
VMEM_LIMIT = 56 * 1024 * 1024

F32 = jnp.float32
BF16 = jnp.bfloat16

CN_KA, CN_KB2, CN_KI4, CN_GA, CN_GB, CN_MA, CN_MB = 0, 512, 640, 896, 1408, 1920, 2944
N_NAT = 3968
RT_QA, RT_QB, RT_VA, RT_QI, RT_VB, RT_WI = 0, 512, 1024, 1536, 1792, 1856
N_TR = 1872
WI_ROWS = 8
ONES_ROWS = 16
VROWS = HEAD_DIM + ONES_ROWS


def _bucket_starts():
    n = np.arange(0, 2 * REL_MAX_DIST + 2)
    max_exact = REL_BUCKETS // 2
    nf = np.maximum(n, 1).astype(np.float64)
    v = np.log(nf / max_exact) / math.log(REL_MAX_DIST / max_exact) * (REL_BUCKETS - max_exact)
    large = np.minimum(max_exact + np.trunc(v).astype(np.int64), REL_BUCKETS - 1)
    bucket = np.where(n < max_exact, n, large)
    frac = np.abs(v - np.round(v))[(n > max_exact) & (n < REL_MAX_DIST)]
    assert frac.min() > 1e-3
    return [int(np.argmax(bucket >= k)) for k in range(REL_BUCKETS)]


_STARTS = _bucket_starts()


def _dot(a, b):
    return jnp.dot(a, b, preferred_element_type=F32)


def _inproj_kernel(x_ref, gn_ref, wn_ref, wt_ref, ghn_ref, ght_ref,
                   ka_ref, kn_ref, sga_ref, sgb_ref, sma_ref, smb_ref, kmean_ref,
                   qat_ref, qbt_ref, vat_ref, qit_ref, vbt_ref, wit_ref):
    x = x_ref[...]
    r = lax.rsqrt(jnp.mean(x * x, axis=-1, keepdims=True) + EPS)
    hf = (x * r) * gn_ref[...]
    h = hf.astype(BF16)
    ht = hf.T.astype(BF16)

    gr = lax.broadcasted_iota(jnp.int32, (BLK, BLK), 0) // HEAD_DIM
    gc = lax.broadcasted_iota(jnp.int32, (BLK, BLK), 1) // HEAD_DIM
    group = jnp.where(gr == gc, 1.0, 0.0).astype(BF16)

    def proj(c0, n):
        return _dot(h, wn_ref[:, c0:c0 + n])

    def proj_t(r0, n):
        return _dot(wt_ref[r0:r0 + n, :], ht)

    def headnorm(y, c0):
        n = y.shape[-1]
        sq = y * y
        hi = sq.astype(BF16)
        lo = (sq - hi.astype(F32)).astype(BF16)
        g = group[:n, :n]
        ss = _dot(hi, g) + _dot(lo, g)
        return (y * lax.rsqrt(ss * (1.0 / HEAD_DIM) + EPS)) * ghn_ref[:, c0:c0 + n]

    def headnorm_t(yt, r0):
        ss = jnp.sum(yt * yt, axis=0, keepdims=True)
        return (yt * lax.rsqrt(ss * (1.0 / HEAD_DIM) + EPS)) * ght_ref[r0:r0 + HEAD_DIM, :]

    for c in range(0, WIDTH, BLK):
        kn = headnorm(proj(CN_KA + c, BLK), CN_KA + c)
        ka_ref[:, c:c + BLK] = kn.astype(BF16)
        for t in range(kn.shape[0] // BLK):
            kmean_ref[t, :, c:c + BLK] = jnp.mean(kn[t * BLK:(t + 1) * BLK], axis=0, keepdims=True)
        sga_ref[:, c:c + BLK] = jax.nn.silu(proj(CN_GA + c, BLK))
        sgb_ref[:, c:c + BLK] = jax.nn.silu(proj(CN_GB + c, BLK))
    kn_ref[:, 0:128] = headnorm(proj(CN_KB2, 128), CN_KB2).astype(BF16)
    kn_ref[:, 128:384] = proj(CN_KI4, 256).astype(BF16)
    for c in range(0, D_MODEL, BLK):
        sma_ref[:, c:c + BLK] = jax.nn.sigmoid(proj(CN_MA + c, BLK))
        smb_ref[:, c:c + BLK] = jax.nn.sigmoid(proj(CN_MB + c, BLK))

    scale = HEAD_DIM ** -0.5
    for hd in range(HEADS):
        rs = slice(hd * HEAD_DIM, (hd + 1) * HEAD_DIM)
        qat_ref[0, rs, :] = (headnorm_t(proj_t(RT_QA + hd * HEAD_DIM, HEAD_DIM), RT_QA + hd * HEAD_DIM)
                             * scale).astype(BF16)
        qbt_ref[0, rs, :] = (headnorm_t(proj_t(RT_QB + hd * HEAD_DIM, HEAD_DIM), RT_QB + hd * HEAD_DIM)
                             * scale).astype(BF16)
    ones = jnp.ones((ONES_ROWS, BLK), BF16)
    for hd in range(HEADS):
        vat_ref[0, hd * VROWS:hd * VROWS + HEAD_DIM, :] = proj_t(RT_VA + hd * HEAD_DIM, HEAD_DIM).astype(BF16)
        vat_ref[0, hd * VROWS + HEAD_DIM:(hd + 1) * VROWS, :] = ones
    qit_ref[0] = (proj_t(RT_QI, 256) * (IDX_DIM ** -0.5)).astype(BF16)
    vbt_ref[0, 0:HEAD_DIM, :] = proj_t(RT_VB, HEAD_DIM).astype(BF16)
    vbt_ref[0, HEAD_DIM:VROWS, :] = ones
    wit_ref[0] = proj_t(RT_WI, 16)[:WI_ROWS] * (IDX_HEADS ** -0.5)


def _inproj(x2, gn, wn, wt, ghn, ght):
    m = x2.shape[0]
    nt = m // BLK
    row = lambda n: pl.BlockSpec((BLK, n), lambda i: (i, 0))
    full = lambda a: pl.BlockSpec(a.shape, lambda i: (0,) * a.ndim)
    tile = lambda n: pl.BlockSpec((1, n, BLK), lambda i: (i, 0, 0))
    out_shapes = [
        jax.ShapeDtypeStruct((m, WIDTH), BF16),
        jax.ShapeDtypeStruct((m, 384), BF16),
        jax.ShapeDtypeStruct((m, WIDTH), F32),
        jax.ShapeDtypeStruct((m, WIDTH), F32),
        jax.ShapeDtypeStruct((m, D_MODEL), F32),
        jax.ShapeDtypeStruct((m, D_MODEL), F32),
        jax.ShapeDtypeStruct((nt, 1, WIDTH), F32),
        jax.ShapeDtypeStruct((nt, WIDTH, BLK), BF16),
        jax.ShapeDtypeStruct((nt, WIDTH, BLK), BF16),
        jax.ShapeDtypeStruct((nt, HEADS * VROWS, BLK), BF16),
        jax.ShapeDtypeStruct((nt, 256, BLK), BF16),
        jax.ShapeDtypeStruct((nt, VROWS, BLK), BF16),
        jax.ShapeDtypeStruct((nt, WI_ROWS, BLK), F32),
    ]
    out_specs = [row(WIDTH), row(384), row(WIDTH), row(WIDTH), row(D_MODEL), row(D_MODEL),
                 pl.BlockSpec((1, 1, WIDTH), lambda i: (i, 0, 0)),
                 tile(WIDTH), tile(WIDTH), tile(HEADS * VROWS), tile(256), tile(VROWS), tile(WI_ROWS)]
    return pl.pallas_call(
        _inproj_kernel,
        grid=(nt,),
        in_specs=[row(D_MODEL), full(gn), full(wn), full(wt), full(ghn), full(ght)],
        out_specs=out_specs,
        out_shape=out_shapes,
        compiler_params=pltpu.CompilerParams(
            dimension_semantics=("arbitrary",), vmem_limit_bytes=VMEM_LIMIT),
        name="inproj",
    )(x2, gn, wn, wt, ghn, ght)


def _build_bias_tiles(rel_ref, bias_ref, head0):
    kpos = lax.broadcasted_iota(jnp.int32, (BLK, BLK), 0)
    qpos = lax.broadcasted_iota(jnp.int32, (BLK, BLK), 1)
    for kind in range(2):
        dist = qpos - kpos + BLK * kind
        for h in range(HEADS):
            t = jnp.full((BLK, BLK), rel_ref[0, head0 + h], F32)
            for k in range(1, REL_BUCKETS):
                t = jnp.where(dist >= _STARTS[k], rel_ref[k, head0 + h], t)
            if kind == 0:
                t = jnp.where(dist >= 0, t, NEG)
            bias_ref[3 * h + kind] = t
    for h in range(HEADS):
        bias_ref[3 * h + 2] = jnp.full((BLK, BLK), rel_ref[REL_BUCKETS - 1, head0 + h], F32)


def _store_masked_queries(qt_ref, qm_ref):
    rowh = lax.broadcasted_iota(jnp.int32, (LANES, BLK), 0)
    for p in range(HEADS // 2):
        q2 = qt_ref[0, LANES * p:LANES * (p + 1), :]
        for hh in range(2):
            qm_ref[2 * p + hh] = jnp.where((rowh >= HEAD_DIM) == bool(hh), q2, jnp.zeros_like(q2))


def _flash_over_blocks(i, score_fn, pv_fn, s_ref, acc_ref, sel_fn=None):
    def stage(j, ms):
        out = []
        for h in range(HEADS):
            s = score_fn(j, h)
            s_ref[h] = s
            m_new = jnp.maximum(ms[h], jnp.max(s, axis=0, keepdims=True))
            out.append(m_new if sel_fn is None else jnp.where(sel_fn(j, h), m_new, ms[h]))
        return tuple(out)

    def finish(j, ms_old, ms_new):
        for h in range(HEADS):
            alpha = jnp.exp(ms_old[h] - ms_new[h])
            p = jnp.exp(s_ref[h] - ms_new[h])
            acc_new = alpha * acc_ref[h] + pv_fn(j, h, p.astype(BF16))
            acc_ref[h] = acc_new if sel_fn is None else jnp.where(sel_fn(j, h), acc_new, acc_ref[h])

    m0 = tuple(jnp.full((1, BLK), M_INIT, F32) for _ in range(HEADS))
    m1 = stage(0, m0)

    def body(j, carry):
        m_prev, m_cur = carry
        finish(j, m_prev, m_cur)
        return m_cur, stage(j + 1, m_cur)

    m_prev, m_cur = lax.fori_loop(0, i, body, (m0, m1))
    finish(i, m_prev, m_cur)


def _write_output(acc_ref, sg_ref, o_ref):
    def head_out(h):
        return acc_ref[h, 0:HEAD_DIM, :] / acc_ref[h, HEAD_DIM:HEAD_DIM + 1, :]

    for p in range(HEADS // 2):
        ot = jnp.concatenate([head_out(2 * p), head_out(2 * p + 1)], axis=0)
        cs = slice(LANES * p, LANES * (p + 1))
        o_ref[0, :, cs] = (ot.T * sg_ref[0, :, cs]).astype(BF16)


def _moba_kernel(rel_ref, qt_ref, k_ref, vt_ref, kmean_ref, sg_ref, o_ref,
                 bias_ref, qm_ref, acc_ref, s_ref):
    b = pl.program_id(0)
    i = pl.program_id(1)
    nb = kmean_ref.shape[1]

    @pl.when((b == 0) & (i == 0))
    def _():
        _build_bias_tiles(rel_ref, bias_ref, 0)

    _store_masked_queries(qt_ref, qm_ref)
    acc_ref[...] = jnp.zeros_like(acc_ref)

    nidx = lax.broadcasted_iota(jnp.int32, (nb, BLK), 0)
    sel_bits = []
    for h in range(HEADS):
        cs = slice(LANES * (h // 2), LANES * (h // 2 + 1))
        gate = _dot(kmean_ref[0, :, cs].astype(BF16), qm_ref[h])
        g = jnp.where(nidx < i, gate, -jnp.inf)
        bits = jnp.left_shift(1, jnp.full((1, BLK), i, jnp.int32))
        for _ in range(MOBA_TOPK):
            mx = jnp.max(g, axis=0, keepdims=True)
            am = jnp.min(jnp.where(g == mx, nidx, nb), axis=0, keepdims=True)
            bits = bits | jnp.where(mx > -jnp.inf, jnp.left_shift(1, am), 0)
            g = jnp.where(nidx == am, -jnp.inf, g)
        sel_bits.append(bits)

    def score(j, h):
        rows = pl.ds(pl.multiple_of(j * BLK, BLK), BLK)
        cs = slice(LANES * (h // 2), LANES * (h // 2 + 1))
        return _dot(k_ref[0, rows, cs], qm_ref[h]) + bias_ref[3 * h + jnp.minimum(i - j, 2)]

    def pv(j, h, p):
        return _dot(vt_ref[0, j, h * VROWS:(h + 1) * VROWS, :], p)

    def sel(j, h):
        return (jnp.right_shift(sel_bits[h], j) & 1) == 1

    _flash_over_blocks(i, score, pv, s_ref, acc_ref, sel)
    _write_output(acc_ref, sg_ref, o_ref)


def _moba(rel_bias, qat, ka, vat, kmean, sga):
    bsz, s, _ = ka.shape
    nb = s // BLK
    qspec = pl.BlockSpec((1, BLK, WIDTH), lambda b, i: (b, i, 0))
    return pl.pallas_call(
        _moba_kernel,
        grid=(bsz, nb),
        in_specs=[pl.BlockSpec(memory_space=pltpu.SMEM),
                  pl.BlockSpec((1, WIDTH, BLK), lambda b, i: (b * nb + i, 0, 0)),
                  pl.BlockSpec((1, s, WIDTH), lambda b, i: (b, 0, 0)),
                  pl.BlockSpec((1, nb, HEADS * VROWS, BLK), lambda b, i: (b, 0, 0, 0)),
                  pl.BlockSpec((1, nb, WIDTH), lambda b, i: (b, 0, 0)), qspec],
        out_specs=qspec,
        out_shape=jax.ShapeDtypeStruct((bsz, s, WIDTH), BF16),
        scratch_shapes=[pltpu.VMEM((3 * HEADS, BLK, BLK), F32),
                        pltpu.VMEM((HEADS, LANES, BLK), BF16),
                        pltpu.VMEM((HEADS, VROWS, BLK), F32),
                        pltpu.VMEM((HEADS, BLK, BLK), F32)],
        compiler_params=pltpu.CompilerParams(
            dimension_semantics=("arbitrary", "arbitrary"), vmem_limit_bytes=VMEM_LIMIT),
        name="moba",
    )(rel_bias, qat, ka, vat.reshape(bsz, nb, HEADS * VROWS, BLK), kmean, sga)


def _dsa_kernel(rel_ref, qt_ref, qit_ref, wit_ref, kn_ref, vt_ref, sg_ref, o_ref,
                bias_ref, qm_ref, acc_ref, s_ref, key_ref, khi_ref, klo_ref, madd_ref, *, topk):
    b = pl.program_id(0)
    i = pl.program_id(1)
    nblk = i + 1

    @pl.when((b == 0) & (i == 0))
    def _():
        _build_bias_tiles(rel_ref, bias_ref, HEADS)

    _store_masked_queries(qt_ref, qm_ref)
    acc_ref[...] = jnp.zeros_like(acc_ref)

    kpos = lax.broadcasted_iota(jnp.int32, (BLK, BLK), 0)
    qpos = lax.broadcasted_iota(jnp.int32, (BLK, BLK), 1)

    def valid_mask(j):
        return (j * BLK + kpos) <= (i * BLK + qpos)

    qit = qit_ref[0]
    wit = wit_ref[0]
    qims = [jnp.where((kpos // IDX_DIM) == h, qit, jnp.zeros_like(qit)) for h in range(IDX_HEADS)]
    whs = [wit[h:h + 1, :] for h in range(IDX_HEADS)]

    def score_body(j, _):
        rows = pl.ds(pl.multiple_of(j * BLK, BLK), BLK)
        ki4 = kn_ref[0, rows, 128:384]
        score = jnp.zeros((BLK, BLK), F32)
        for h in range(IDX_HEADS):
            score = score + jnp.maximum(_dot(ki4, qims[h]), 0.0) * whs[h]
        bits = lax.bitcast_convert_type(score, jnp.int32)
        key = bits ^ (jnp.right_shift(bits, 31) & 0x7FFFFFFF)
        key = jnp.where(valid_mask(j), key, INT_MIN)
        key_ref[j] = key
        khi_ref[j] = jnp.right_shift(key, 16).astype(jnp.int16)
        return 0

    lax.fori_loop(0, nblk, score_body, 0)

    def count16(ref, cand, strict=False):
        c16 = jnp.broadcast_to(cand, (PACK16, BLK)).astype(jnp.int16)

        def cb(j, acc):
            k3 = ref[j].reshape(BLK // PACK16, PACK16, BLK)
            for g in range(BLK // PACK16):
                hit = (k3[g] > c16) if strict else (k3[g] >= c16)
                acc = acc + jnp.where(hit, jnp.int16(1), jnp.int16(0))
            return acc

        acc = lax.fori_loop(0, nblk, cb, jnp.zeros((PACK16, BLK), jnp.int16))
        return jnp.sum(acc.astype(jnp.int32), axis=0, keepdims=True)

    def bisect16(ref, want):
        zero = jnp.zeros((1, BLK), jnp.int32)
        t0 = jnp.where(count16(ref, zero) >= want, zero, INT16_MIN)

        def bit_body(t, thr):
            cand = thr + jnp.left_shift(1, 14 - t)
            return jnp.where(count16(ref, cand) >= want, cand, thr)

        return lax.fori_loop(0, 15, bit_body, t0)

    thr_hi = bisect16(khi_ref, topk)
    above_hi = count16(khi_ref, thr_hi, strict=True)

    def low_body(j, _):
        key = key_ref[j]
        low = (key & 0xFFFF) - 0x8000
        member = jnp.right_shift(key, 16) == thr_hi
        klo_ref[j] = jnp.where(member, low, INT16_MIN).astype(jnp.int16)
        return 0

    lax.fori_loop(0, nblk, low_body, 0)
    thr_lo = bisect16(klo_ref, topk - above_hi)
    thr = thr_hi * 65536 + (thr_lo + 0x8000)
    need = (topk - above_hi - count16(klo_ref, thr_lo, strict=True)).astype(F32)

    lower = jnp.where(qpos <= kpos, 1.0, 0.0).astype(BF16)

    def mask_body(j, carry):
        k = key_ref[j]
        eq = k == thr
        pref = _dot(lower, jnp.where(eq, 1.0, 0.0).astype(BF16))
        sel = (k > thr) | (eq & ((carry + pref) <= need))
        madd_ref[j] = jnp.where(sel & valid_mask(j), 0.0, NEG)
        return carry + pref[BLK - 1:BLK, :]

    lax.fori_loop(0, nblk, mask_body, jnp.zeros((1, BLK), F32))

    def score(j, h):
        rows = pl.ds(pl.multiple_of(j * BLK, BLK), BLK)
        return (_dot(kn_ref[0, rows, 0:128], qm_ref[h])
                + bias_ref[3 * h + jnp.minimum(i - j, 2)] + madd_ref[j])

    _flash_over_blocks(i, score, lambda j, h, p: _dot(vt_ref[0, j], p), s_ref, acc_ref)
    _write_output(acc_ref, sg_ref, o_ref)


def _dsa(rel_bias, qbt, qit, wit, kn, vbt, sgb):
    bsz, s, _ = kn.shape
    nb = s // BLK
    topk = min(DSA_TOPK_MAX, s // 4)
    qspec = pl.BlockSpec((1, BLK, WIDTH), lambda b, i: (b, i, 0))
    tile = lambda n: pl.BlockSpec((1, n, BLK), lambda b, i: (b * nb + i, 0, 0))
    return pl.pallas_call(
        functools.partial(_dsa_kernel, topk=topk),
        grid=(bsz, nb),
        in_specs=[pl.BlockSpec(memory_space=pltpu.SMEM), tile(WIDTH), tile(256), tile(WI_ROWS),
                  pl.BlockSpec((1, s, 384), lambda b, i: (b, 0, 0)),
                  pl.BlockSpec((1, nb, VROWS, BLK), lambda b, i: (b, 0, 0, 0)), qspec],
        out_specs=qspec,
        out_shape=jax.ShapeDtypeStruct((bsz, s, WIDTH), BF16),
        scratch_shapes=[pltpu.VMEM((3 * HEADS, BLK, BLK), F32),
                        pltpu.VMEM((HEADS, LANES, BLK), BF16),
                        pltpu.VMEM((HEADS, VROWS, BLK), F32),
                        pltpu.VMEM((HEADS, BLK, BLK), F32),
                        pltpu.VMEM((nb, BLK, BLK), jnp.int32),
                        pltpu.VMEM((nb, BLK, BLK), jnp.int16),
                        pltpu.VMEM((nb, BLK, BLK), jnp.int16),
                        pltpu.VMEM((nb, BLK, BLK), F32)],
        compiler_params=pltpu.CompilerParams(
            dimension_semantics=("arbitrary", "arbitrary"), vmem_limit_bytes=VMEM_LIMIT),
        name="dsa",
    )(rel_bias, qbt, qit, wit, kn, vbt.reshape(bsz, nb, VROWS, BLK), sgb)


def _outproj_kernel(x_ref, ya_ref, yb_ref, sma_ref, smb_ref, wa_ref, wb_ref, wo_ref, o_ref):
    merged = (sma_ref[...] * _dot(ya_ref[...], wa_ref[...])
              + smb_ref[...] * _dot(yb_ref[...], wb_ref[...]))
    o_ref[...] = x_ref[...] + _dot(merged.astype(BF16), wo_ref[...])


def _outproj(x2, ya, yb, sma, smb, wa, wb, wo):
    m = x2.shape[0]
    row = lambda n: pl.BlockSpec((BLK, n), lambda i: (i, 0))
    full = lambda a: pl.BlockSpec(a.shape, lambda i: (0,) * a.ndim)
    return pl.pallas_call(
        _outproj_kernel,
        grid=(m // BLK,),
        in_specs=[row(D_MODEL), row(WIDTH), row(WIDTH), row(D_MODEL), row(D_MODEL),
                  full(wa), full(wb), full(wo)],
        out_specs=row(D_MODEL),
        out_shape=jax.ShapeDtypeStruct((m, D_MODEL), F32),
        compiler_params=pltpu.CompilerParams(
            dimension_semantics=("arbitrary",), vmem_limit_bytes=VMEM_LIMIT),
        name="outproj",
    )(x2, ya, yb, sma, smb, wa, wb, wo)


def _pack_w_in(w):
    o = 0
    parts = {}
    for name, n in (("qa", WIDTH), ("ka", WIDTH), ("va", WIDTH), ("ga", WIDTH), ("qb", WIDTH),
                    ("kb", HEAD_DIM), ("vb", HEAD_DIM), ("gb", WIDTH), ("qi", IDX_HEADS * IDX_DIM),
                    ("ki", IDX_DIM), ("wi", IDX_HEADS), ("ma", D_MODEL), ("mb", D_MODEL)):
        parts[name] = w[:, o:o + n]
        o += n
    wn = jnp.concatenate([parts["ka"], parts["kb"], parts["kb"],
                          parts["ki"], parts["ki"], parts["ki"], parts["ki"],
                          parts["ga"], parts["gb"], parts["ma"], parts["mb"]], axis=1).astype(BF16)
    pad = jnp.zeros((w.shape[0], N_TR - RT_WI - IDX_HEADS), w.dtype)
    wt = jnp.concatenate([parts["qa"], parts["qb"], parts["va"], parts["qi"], parts["vb"],
                          parts["wi"], pad], axis=1).T.astype(BF16)
    assert wn.shape[1] == N_NAT and wt.shape[0] == N_TR
    return wn, wt


def _layer(x2, bsz, s, gn, w_in, qn_a, kn_a, qn_b, kn_b, w_br_a, w_br_b, w_out, rel_bias):
    wn, wt = _pack_w_in(w_in)
    ghn = jnp.concatenate([jnp.tile(kn_a, HEADS), jnp.tile(kn_b, 2)])[None, :]
    ght = jnp.concatenate([jnp.tile(qn_a, HEADS), jnp.tile(qn_b, HEADS)])[:, None]
    (ka, kn, sga, sgb, sma, smb, kmean, qat, qbt, vat, qit, vbt, wit) = _inproj(
        x2, gn[None, :], wn, wt, ghn, ght)
    r3 = lambda a: a.reshape(bsz, s, a.shape[-1])
    ya = _moba(rel_bias, qat, r3(ka), vat, kmean.reshape(bsz, s // BLK, WIDTH), r3(sga))
    yb = _dsa(rel_bias, qbt, qit, wit, r3(kn), vbt, r3(sgb))
    return _outproj(x2, ya.reshape(-1, WIDTH), yb.reshape(-1, WIDTH), sma, smb,
                    w_br_a.astype(BF16), w_br_b.astype(BF16), w_out.astype(BF16))


def kernel(x, norm_g, w_in, q_norm_a, k_norm_a, q_norm_b, k_norm_b,
           w_branch_a, w_branch_b, w_out, rel_bias):
    bsz, s, d = x.shape
    assert d == D_MODEL and s % BLK == 0 and s // BLK <= 31
    h = x.reshape(bsz * s, d)
    for l in range(norm_g.shape[0]):
        h = _layer(h, bsz, s, norm_g[l], w_in[l], q_norm_a[l], k_norm_a[l], q_norm_b[l],
                   k_norm_b[l], w_branch_a[l], w_branch_b[l], w_out[l], rel_bias)
    return h.reshape(bsz, s, d)
```

```python
import functools
import math

import numpy as np
import jax
import jax.numpy as jnp
from jax import lax
from jax.experimental import pallas as pl
from jax.experimental.pallas import tpu as pltpu

D_MODEL = 1024
HEADS = 8
HEAD_DIM = 64
WIDTH = HEADS * HEAD_DIM
BLK = 256
MOBA_TOPK = 3
IDX_HEADS = 4
IDX_DIM = 64
DSA_TOPK_MAX = 256
REL_BUCKETS = 32
REL_MAX_DIST = 128
EPS = 1e-6
NEG = -1e30
M_INIT = -1e20
INT_MIN = -2 ** 31
INT16_MIN = -2 ** 15
PACK16 = 16
LANES = 128
VMEM_LIMIT = 56 * 1024 * 1024

F32 = jnp.float32
BF16 = jnp.bfloat16

CN_KA, CN_KB2, CN_KI4, CN_GA, CN_GB, CN_MA, CN_MB = 0, 512, 640, 896, 1408, 1920, 2944
N_NAT = 3968
RT_QA, RT_QB, RT_VA, RT_QI, RT_VB, RT_WI = 0, 512, 1024, 1536, 1792, 1856
N_TR = 1872
WI_ROWS = 8
ROWS_IN = 256
ONES_ROWS = 16
VROWS = HEAD_DIM + ONES_ROWS


def _bucket_starts():
    n = np.arange(0, 2 * REL_MAX_DIST + 2)
    max_exact = REL_BUCKETS // 2
    nf = np.maximum(n, 1).astype(np.float64)
    v = np.log(nf / max_exact) / math.log(REL_MAX_DIST / max_exact) * (REL_BUCKETS - max_exact)
    large = np.minimum(max_exact + np.trunc(v).astype(np.int64), REL_BUCKETS - 1)
    bucket = np.where(n < max_exact, n, large)
    frac = np.abs(v - np.round(v))[(n > max_exact) & (n < REL_MAX_DIST)]
    assert frac.min() > 1e-3
    return [int(np.argmax(bucket >= k)) for k in range(REL_BUCKETS)]


_STARTS = _bucket_starts()


def _dot(a, b):
    return jnp.dot(a, b, preferred_element_type=F32)


def _inproj_kernel(x_ref, gn_ref, wn_ref, wt_ref, ghn_ref, ght_ref,
                   ka_ref, kn_ref, sga_ref, sgb_ref, sma_ref, smb_ref, kmean_ref,
                   qat_ref, qbt_ref, vat_ref, qit_ref, vbt_ref, wit_ref):
    x = x_ref[...]
    r = lax.rsqrt(jnp.mean(x * x, axis=-1, keepdims=True) + EPS)
    hf = (x * r) * gn_ref[...]
    h = hf.astype(BF16)
    ht = hf.T.astype(BF16)

    gr = lax.broadcasted_iota(jnp.int32, (BLK, BLK), 0) // HEAD_DIM
    gc = lax.broadcasted_iota(jnp.int32, (BLK, BLK), 1) // HEAD_DIM
    group = jnp.where(gr == gc, 1.0, 0.0).astype(BF16)

    def proj(c0, n):
        return _dot(h, wn_ref[:, c0:c0 + n])

    def proj_t(r0, n):
        return _dot(wt_ref[r0:r0 + n, :], ht)

    def headnorm(y, c0):
        n = y.shape[-1]
        sq = y * y
        hi = sq.astype(BF16)
        lo = (sq - hi.astype(F32)).astype(BF16)
        g = group[:n, :n]
        ss = _dot(hi, g) + _dot(lo, g)
        return (y * lax.rsqrt(ss * (1.0 / HEAD_DIM) + EPS)) * ghn_ref[:, c0:c0 + n]

    def headnorm_t(yt, r0):
        ss = jnp.sum(yt * yt, axis=0, keepdims=True)
        return (yt * lax.rsqrt(ss * (1.0 / HEAD_DIM) + EPS)) * ght_ref[r0:r0 + HEAD_DIM, :]

    for c in range(0, WIDTH, BLK):
        kn = headnorm(proj(CN_KA + c, BLK), CN_KA + c)
        ka_ref[:, c:c + BLK] = kn.astype(BF16)
        for t in range(kn.shape[0] // BLK):
            kmean_ref[t, :, c:c + BLK] = jnp.mean(kn[t * BLK:(t + 1) * BLK], axis=0, keepdims=True)
        sga_ref[:, c:c + BLK] = jax.nn.silu(proj(CN_GA + c, BLK))
        sgb_ref[:, c:c + BLK] = jax.nn.silu(proj(CN_GB + c, BLK))
    kn_ref[:, 0:128] = headnorm(proj(CN_KB2, 128), CN_KB2).astype(BF16)
    kn_ref[:, 128:384] = proj(CN_KI4, 256).astype(BF16)
    for c in range(0, D_MODEL, BLK):
        sma_ref[:, c:c + BLK] = jax.nn.sigmoid(proj(CN_MA + c, BLK))
        smb_ref[:, c:c + BLK] = jax.nn.sigmoid(proj(CN_MB + c, BLK))

    def put_t(ref, r0, val):
        for t in range(ROWS_IN // BLK):
            ref[t, r0:r0 + val.shape[0], :] = val[:, t * BLK:(t + 1) * BLK]

    scale = HEAD_DIM ** -0.5
    ones = jnp.ones((ONES_ROWS, ROWS_IN), BF16)
    for hd in range(HEADS):
        r0 = hd * HEAD_DIM
        put_t(qat_ref, r0, (headnorm_t(proj_t(RT_QA + r0, HEAD_DIM), RT_QA + r0) * scale).astype(BF16))
        put_t(qbt_ref, r0, (headnorm_t(proj_t(RT_QB + r0, HEAD_DIM), RT_QB + r0) * scale).astype(BF16))
        put_t(vat_ref, hd * VROWS, proj_t(RT_VA + r0, HEAD_DIM).astype(BF16))
        put_t(vat_ref, hd * VROWS + HEAD_DIM, ones)
    put_t(qit_ref, 0, (proj_t(RT_QI, 256) * (IDX_DIM ** -0.5)).astype(BF16))
    put_t(vbt_ref, 0, proj_t(RT_VB, HEAD_DIM).astype(BF16))
    put_t(vbt_ref, HEAD_DIM, ones)
    put_t(wit_ref, 0, proj_t(RT_WI, 16)[:WI_ROWS] * (IDX_HEADS ** -0.5))


def _inproj(x2, gn, wn, wt, ghn, ght):
    m = x2.shape[0]
    nt = m // BLK
    tpb = ROWS_IN // BLK
    row = lambda n: pl.BlockSpec((ROWS_IN, n), lambda i: (i, 0))
    full = lambda a: pl.BlockSpec(a.shape, lambda i: (0,) * a.ndim)
    tile = lambda n: pl.BlockSpec((tpb, n, BLK), lambda i: (i, 0, 0))
    out_shapes = [
        jax.ShapeDtypeStruct((m, WIDTH), BF16),
        jax.ShapeDtypeStruct((m, 384), BF16),
        jax.ShapeDtypeStruct((m, WIDTH), F32),
        jax.ShapeDtypeStruct((m, WIDTH), F32),
        jax.ShapeDtypeStruct((m, D_MODEL), F32),
        jax.ShapeDtypeStruct((m, D_MODEL), F32),
        jax.ShapeDtypeStruct((nt, 1, WIDTH), F32),
        jax.ShapeDtypeStruct((nt, WIDTH, BLK), BF16),
        jax.ShapeDtypeStruct((nt, WIDTH, BLK), BF16),
        jax.ShapeDtypeStruct((nt, HEADS * VROWS, BLK), BF16),
        jax.ShapeDtypeStruct((nt, 256, BLK), BF16),
        jax.ShapeDtypeStruct((nt, VROWS, BLK), BF16),
        jax.ShapeDtypeStruct((nt, WI_ROWS, BLK), F32),
    ]
    out_specs = [row(WIDTH), row(384), row(WIDTH), row(WIDTH), row(D_MODEL), row(D_MODEL),
                 pl.BlockSpec((tpb, 1, WIDTH), lambda i: (i, 0, 0)),
                 tile(WIDTH), tile(WIDTH), tile(HEADS * VROWS), tile(256), tile(VROWS), tile(WI_ROWS)]
    return pl.pallas_call(
        _inproj_kernel,
        grid=(nt // tpb,),
        in_specs=[row(D_MODEL), full(gn), full(wn), full(wt), full(ghn), full(ght)],
        out_specs=out_specs,
        out_shape=out_shapes,
        compiler_params=pltpu.CompilerParams(
            dimension_semantics=("arbitrary",), vmem_limit_bytes=VMEM_LIMIT),
        name="inproj",
    )(x2, gn, wn, wt, ghn, ght)


def _build_bias_tiles(rel_ref, bias_ref, head0):
    kpos = lax.broadcasted_iota(jnp.int32, (BLK, BLK), 0)
    qpos = lax.broadcasted_iota(jnp.int32, (BLK, BLK), 1)
    for kind in range(2):
        dist = qpos - kpos + BLK * kind
        for h in range(HEADS):
            t = jnp.full((BLK, BLK), rel_ref[0, head0 + h], F32)
            for k in range(1, REL_BUCKETS):
                t = jnp.where(dist >= _STARTS[k], rel_ref[k, head0 + h], t)
            if kind == 0:
                t = jnp.where(dist >= 0, t, NEG)
            bias_ref[3 * h + kind] = t
    for h in range(HEADS):
        bias_ref[3 * h + 2] = jnp.full((BLK, BLK), rel_ref[REL_BUCKETS - 1, head0 + h], F32)


def _store_masked_queries(qt_ref, qm_ref):
    rowh = lax.broadcasted_iota(jnp.int32, (LANES, BLK), 0)
    for p in range(HEADS // 2):
        q2 = qt_ref[0, LANES * p:LANES * (p + 1), :]
        for hh in range(2):
            qm_ref[2 * p + hh] = jnp.where((rowh >= HEAD_DIM) == bool(hh), q2, jnp.zeros_like(q2))


def _flash_over_blocks(i, score_fn, pv_fn, s_ref, acc_ref, sel_fn=None):
    def stage(j, ms):
        out = []
        for h in range(HEADS):
            s = score_fn(j, h)
            s_ref[h] = s
            m_new = jnp.maximum(ms[h], jnp.max(s, axis=0, keepdims=True))
            out.append(m_new if sel_fn is None else jnp.where(sel_fn(j, h), m_new, ms[h]))
        return tuple(out)

    def finish(j, ms_old, ms_new):
        for h in range(HEADS):
            alpha = jnp.exp(ms_old[h] - ms_new[h])
            p = jnp.exp(s_ref[h] - ms_new[h])
            acc_new = alpha * acc_ref[h] + pv_fn(j, h, p.astype(BF16))
            acc_ref[h] = acc_new if sel_fn is None else jnp.where(sel_fn(j, h), acc_new, acc_ref[h])

    m0 = tuple(jnp.full((1, BLK), M_INIT, F32) for _ in range(HEADS))
    m1 = stage(0, m0)

    def body(j, carry):
        m_prev, m_cur = carry
        finish(j, m_prev, m_cur)
        return m_cur, stage(j + 1, m_cur)

    m_prev, m_cur = lax.fori_loop(0, i, body, (m0, m1))
    finish(i, m_prev, m_cur)


def _write_output(acc_ref, sg_ref, o_ref):
    def head_out(h):
        return acc_ref[h, 0:HEAD_DIM, :] / acc_ref[h, HEAD_DIM:HEAD_DIM + 1, :]

    for p in range(HEADS // 2):
        ot = jnp.concatenate([head_out(2 * p), head_out(2 * p + 1)], axis=0)
        cs = slice(LANES * p, LANES * (p + 1))
        o_ref[0, :, cs] = (ot.T * sg_ref[0, :, cs]).astype(BF16)


def _moba_kernel(rel_ref, qt_ref, k_ref, vt_ref, kmean_ref, sg_ref, o_ref,
                 bias_ref, qm_ref, acc_ref, s_ref):
    b = pl.program_id(0)
    i = pl.program_id(1)
    nb = kmean_ref.shape[1]

    @pl.when((b == 0) & (i == 0))
    def _():
        _build_bias_tiles(rel_ref, bias_ref, 0)

    _store_masked_queries(qt_ref, qm_ref)
    acc_ref[...] = jnp.zeros_like(acc_ref)

    nidx = lax.broadcasted_iota(jnp.int32, (nb, BLK), 0)
    sel_bits = []
    for h in range(HEADS):
        cs = slice(LANES * (h // 2), LANES * (h // 2 + 1))
        gate = _dot(kmean_ref[0, :, cs].astype(BF16), qm_ref[h])
        g = jnp.where(nidx < i, gate, -jnp.inf)
        bits = jnp.left_shift(1, jnp.full((1, BLK), i, jnp.int32))
        for _ in range(MOBA_TOPK):
            mx = jnp.max(g, axis=0, keepdims=True)
            am = jnp.min(jnp.where(g == mx, nidx, nb), axis=0, keepdims=True)
            bits = bits | jnp.where(mx > -jnp.inf, jnp.left_shift(1, am), 0)
            g = jnp.where(nidx == am, -jnp.inf, g)
        sel_bits.append(bits)

    def score(j, h):
        rows = pl.ds(pl.multiple_of(j * BLK, BLK), BLK)
        cs = slice(LANES * (h // 2), LANES * (h // 2 + 1))
        return _dot(k_ref[0, rows, cs], qm_ref[h]) + bias_ref[3 * h + jnp.minimum(i - j, 2)]

    def pv(j, h, p):
        return _dot(vt_ref[0, j, h * VROWS:(h + 1) * VROWS, :], p)

    def sel(j, h):
        return (jnp.right_shift(sel_bits[h], j) & 1) == 1

    _flash_over_blocks(i, score, pv, s_ref, acc_ref, sel)
    _write_output(acc_ref, sg_ref, o_ref)


def _moba(rel_bias, qat, ka, vat, kmean, sga):
    bsz, s, _ = ka.shape
    nb = s // BLK
    qspec = pl.BlockSpec((1, BLK, WIDTH), lambda b, i: (b, i, 0))
    return pl.pallas_call(
        _moba_kernel,
        grid=(bsz, nb),
        in_specs=[pl.BlockSpec(memory_space=pltpu.SMEM),
                  pl.BlockSpec((1, WIDTH, BLK), lambda b, i: (b * nb + i, 0, 0)),
                  pl.BlockSpec((1, s, WIDTH), lambda b, i: (b, 0, 0)),
                  pl.BlockSpec((1, nb, HEADS * VROWS, BLK), lambda b, i: (b, 0, 0, 0)),
                  pl.BlockSpec((1, nb, WIDTH), lambda b, i: (b, 0, 0)), qspec],
        out_specs=qspec,
        out_shape=jax.ShapeDtypeStruct((bsz, s, WIDTH), BF16),
        scratch_shapes=[pltpu.VMEM((3 * HEADS, BLK, BLK), F32),
                        pltpu.VMEM((HEADS, LANES, BLK), BF16),
                        pltpu.VMEM((HEADS, VROWS, BLK), F32),
                        pltpu.VMEM((HEADS, BLK, BLK), F32)],
        compiler_params=pltpu.CompilerParams(
            dimension_semantics=("arbitrary", "arbitrary"), vmem_limit_bytes=VMEM_LIMIT),
        name="moba",
    )(rel_bias, qat, ka, vat.reshape(bsz, nb, HEADS * VROWS, BLK), kmean, sga)


def _dsa_kernel(rel_ref, qt_ref, qit_ref, wit_ref, kn_ref, vt_ref, sg_ref, o_ref,
                bias_ref, qm_ref, acc_ref, s_ref, key_ref, khi_ref, klo_ref, madd_ref, *, topk):
    b = pl.program_id(0)
    i = pl.program_id(1)
    nblk = i + 1

    @pl.when((b == 0) & (i == 0))
    def _():
        _build_bias_tiles(rel_ref, bias_ref, HEADS)

    _store_masked_queries(qt_ref, qm_ref)
    acc_ref[...] = jnp.zeros_like(acc_ref)

    kpos = lax.broadcasted_iota(jnp.int32, (BLK, BLK), 0)
    qpos = lax.broadcasted_iota(jnp.int32, (BLK, BLK), 1)

    def valid_mask(j):
        return (j * BLK + kpos) <= (i * BLK + qpos)

    qit = qit_ref[0]
    wit = wit_ref[0]
    qims = [jnp.where((kpos // IDX_DIM) == h, qit, jnp.zeros_like(qit)) for h in range(IDX_HEADS)]
    whs = [wit[h:h + 1, :] for h in range(IDX_HEADS)]

    def score_body(j, _):
        rows = pl.ds(pl.multiple_of(j * BLK, BLK), BLK)
        ki4 = kn_ref[0, rows, 128:384]
        score = jnp.zeros((BLK, BLK), F32)
        for h in range(IDX_HEADS):
            score = score + jnp.maximum(_dot(ki4, qims[h]), 0.0) * whs[h]
        bits = lax.bitcast_convert_type(score, jnp.int32)
        key = bits ^ (jnp.right_shift(bits, 31) & 0x7FFFFFFF)
        key = jnp.where(valid_mask(j), key, INT_MIN)
        key_ref[j] = key
        khi_ref[j] = jnp.right_shift(key, 16).astype(jnp.int16)
        return 0

    pad = jnp.minimum(nblk, key_ref.shape[0] - 1)
    khi_ref[pad] = jnp.full((BLK, BLK), INT16_MIN, jnp.int16)
    klo_ref[pad] = jnp.full((BLK, BLK), INT16_MIN, jnp.int16)
    lax.fori_loop(0, nblk, score_body, 0)

    def count16(ref, cand, strict=False):
        c16 = jnp.broadcast_to(cand, (PACK16, BLK)).astype(jnp.int16)

        def cb(t, acc):
            for j in (2 * t, 2 * t + 1):
                k3 = ref[j].reshape(BLK // PACK16, PACK16, BLK)
                for g in range(BLK // PACK16):
                    hit = (k3[g] > c16) if strict else (k3[g] >= c16)
                    acc = acc + jnp.where(hit, jnp.int16(1), jnp.int16(0))
            return acc

        acc = lax.fori_loop(0, (nblk + 1) // 2, cb, jnp.zeros((PACK16, BLK), jnp.int16))
        return jnp.sum(acc.astype(jnp.int32), axis=0, keepdims=True)

    def bisect16(ref, want):
        zero = jnp.zeros((1, BLK), jnp.int32)
        t0 = jnp.where(count16(ref, zero) >= want, zero, INT16_MIN)

        def bit_body(t, thr):
            cand = thr + jnp.left_shift(1, 14 - t)
            return jnp.where(count16(ref, cand) >= want, cand, thr)

        return lax.fori_loop(0, 15, bit_body, t0)

    thr_hi = bisect16(khi_ref, topk)
    above_hi = count16(khi_ref, thr_hi, strict=True)

    def low_body(j, _):
        key = key_ref[j]
        low = (key & 0xFFFF) - 0x8000
        member = jnp.right_shift(key, 16) == thr_hi
        klo_ref[j] = jnp.where(member, low, INT16_MIN).astype(jnp.int16)
        return 0

    lax.fori_loop(0, nblk, low_body, 0)
    thr_lo = bisect16(klo_ref, topk - above_hi)
    thr = thr_hi * 65536 + (thr_lo + 0x8000)
    above = above_hi + count16(klo_ref, thr_lo, strict=True)
    at_thr = above_hi + count16(klo_ref, thr_lo) - above
    need = topk - above

    tied = (at_thr > need) & (thr > INT_MIN)
    any_tied = jnp.max(jnp.where(tied, 1, 0)) > 0

    @pl.when(jnp.logical_not(any_tied))
    def _():
        thr_valid = jnp.maximum(thr, INT_MIN + 1)

        def mask_body(j, _):
            madd_ref[j] = jnp.where(key_ref[j] >= thr_valid, 0.0, NEG)
            return 0

        lax.fori_loop(0, nblk, mask_body, 0)

    @pl.when(any_tied)
    def _():
        lower = jnp.where(qpos <= kpos, 1.0, 0.0).astype(BF16)
        needf = need.astype(F32)

        def mask_body(j, carry):
            k = key_ref[j]
            eq = k == thr
            pref = _dot(lower, jnp.where(eq, 1.0, 0.0).astype(BF16))
            sel = (k > thr) | (eq & ((carry + pref) <= needf))
            madd_ref[j] = jnp.where(sel & valid_mask(j), 0.0, NEG)
            return carry + pref[BLK - 1:BLK, :]

        lax.fori_loop(0, nblk, mask_body, jnp.zeros((1, BLK), F32))

    def score(j, h):
        rows = pl.ds(pl.multiple_of(j * BLK, BLK), BLK)
        return (_dot(kn_ref[0, rows, 0:128], qm_ref[h])
                + bias_ref[3 * h + jnp.minimum(i - j, 2)] + madd_ref[j])

    _flash_over_blocks(i, score, lambda j, h, p: _dot(vt_ref[0, j], p), s_ref, acc_ref)
    _write_output(acc_ref, sg_ref, o_ref)


def _dsa(rel_bias, qbt, qit, wit, kn, vbt, sgb):
    bsz, s, _ = kn.shape
    nb = s // BLK
    topk = min(DSA_TOPK_MAX, s // 4)
    qspec = pl.BlockSpec((1, BLK, WIDTH), lambda b, i: (b, i, 0))
    tile = lambda n: pl.BlockSpec((1, n, BLK), lambda b, i: (b * nb + i, 0, 0))
    return pl.pallas_call(
        functools.partial(_dsa_kernel, topk=topk),
        grid=(bsz, nb),
        in_specs=[pl.BlockSpec(memory_space=pltpu.SMEM), tile(WIDTH), tile(256), tile(WI_ROWS),
                  pl.BlockSpec((1, s, 384), lambda b, i: (b, 0, 0)),
                  pl.BlockSpec((1, nb, VROWS, BLK), lambda b, i: (b, 0, 0, 0)), qspec],
        out_specs=qspec,
        out_shape=jax.ShapeDtypeStruct((bsz, s, WIDTH), BF16),
        scratch_shapes=[pltpu.VMEM((3 * HEADS, BLK, BLK), F32),
                        pltpu.VMEM((HEADS, LANES, BLK), BF16),
                        pltpu.VMEM((HEADS, VROWS, BLK), F32),
                        pltpu.VMEM((HEADS, BLK, BLK), F32),
                        pltpu.VMEM((nb, BLK, BLK), jnp.int32),
                        pltpu.VMEM((nb, BLK, BLK), jnp.int16),
                        pltpu.VMEM((nb, BLK, BLK), jnp.int16),
                        pltpu.VMEM((nb, BLK, BLK), F32)],
        compiler_params=pltpu.CompilerParams(
            dimension_semantics=("arbitrary", "arbitrary"), vmem_limit_bytes=VMEM_LIMIT),
        name="dsa",
    )(rel_bias, qbt, qit, wit, kn, vbt.reshape(bsz, nb, VROWS, BLK), sgb)


def _outproj_kernel(x_ref, ya_ref, yb_ref, sma_ref, smb_ref, wa_ref, wb_ref, wo_ref, o_ref):
    merged = (sma_ref[...] * _dot(ya_ref[...], wa_ref[...])
              + smb_ref[...] * _dot(yb_ref[...], wb_ref[...]))
    o_ref[...] = x_ref[...] + _dot(merged.astype(BF16), wo_ref[...])


def _outproj(x2, ya, yb, sma, smb, wa, wb, wo):
    m = x2.shape[0]
    row = lambda n: pl.BlockSpec((BLK, n), lambda i: (i, 0))
    full = lambda a: pl.BlockSpec(a.shape, lambda i: (0,) * a.ndim)
    return pl.pallas_call(
        _outproj_kernel,
        grid=(m // BLK,),
        in_specs=[row(D_MODEL), row(WIDTH), row(WIDTH), row(D_MODEL), row(D_MODEL),
                  full(wa), full(wb), full(wo)],
        out_specs=row(D_MODEL),
        out_shape=jax.ShapeDtypeStruct((m, D_MODEL), F32),
        compiler_params=pltpu.CompilerParams(
            dimension_semantics=("arbitrary",), vmem_limit_bytes=VMEM_LIMIT),
        name="outproj",
    )(x2, ya, yb, sma, smb, wa, wb, wo)


def _pack_w_in(w):
    o = 0
    parts = {}
    for name, n in (("qa", WIDTH), ("ka", WIDTH), ("va", WIDTH), ("ga", WIDTH), ("qb", WIDTH),
                    ("kb", HEAD_DIM), ("vb", HEAD_DIM), ("gb", WIDTH), ("qi", IDX_HEADS * IDX_DIM),
                    ("ki", IDX_DIM), ("wi", IDX_HEADS), ("ma", D_MODEL), ("mb", D_MODEL)):
        parts[name] = w[:, o:o + n]
        o += n
    wn = jnp.concatenate([parts["ka"], parts["kb"], parts["kb"],
                          parts["ki"], parts["ki"], parts["ki"], parts["ki"],
                          parts["ga"], parts["gb"], parts["ma"], parts["mb"]], axis=1).astype(BF16)
    pad = jnp.zeros((w.shape[0], N_TR - RT_WI - IDX_HEADS), w.dtype)
    wt = jnp.concatenate([parts["qa"], parts["qb"], parts["va"], parts["qi"], parts["vb"],
                          parts["wi"], pad], axis=1).T.astype(BF16)
    assert wn.shape[1] == N_NAT and wt.shape[0] == N_TR
    return wn, wt


def _layer(x2, bsz, s, gn, w_in, qn_a, kn_a, qn_b, kn_b, w_br_a, w_br_b, w_out, rel_bias):
    wn, wt = _pack_w_in(w_in)
    ghn = jnp.concatenate([jnp.tile(kn_a, HEADS), jnp.tile(kn_b, 2)])[None, :]
    ght = jnp.concatenate([jnp.tile(qn_a, HEADS), jnp.tile(qn_b, HEADS)])[:, None]
    (ka, kn, sga, sgb, sma, smb, kmean, qat, qbt, vat, qit, vbt, wit) = _inproj(
        x2, gn[None, :], wn, wt, ghn, ght)
    r3 = lambda a: a.reshape(bsz, s, a.shape[-1])
    ya = _moba(rel_bias, qat, r3(ka), vat, kmean.reshape(bsz, s // BLK, WIDTH), r3(sga))
    yb = _dsa(rel_bias, qbt, qit, wit, r3(kn), vbt, r3(sgb))
    return _outproj(x2, ya.reshape(-1, WIDTH), yb.reshape(-1, WIDTH), sma, smb,
                    w_br_a.astype(BF16), w_br_b.astype(BF16), w_out.astype(BF16))


def kernel(x, norm_g, w_in, q_norm_a, k_norm_a, q_norm_b, k_norm_b,
           w_branch_a, w_branch_b, w_out, rel_bias):
    bsz, s, d = x.shape
    assert d == D_MODEL and s % BLK == 0 and s // BLK <= 31
    h = x.reshape(bsz * s, d)
    for l in range(norm_g.shape[0]):
        h = _layer(h, bsz, s, norm_g[l], w_in[l], q_norm_a[l], k_norm_a[l], q_norm_b[l],
                   k_norm_b[l], w_branch_a[l], w_branch_b[l], w_out[l], rel_bias)
    return h.reshape(bsz, s, d)
```

```python
import functools
import math

import numpy as np
import jax
import jax.numpy as jnp
from jax import lax
from jax.experimental import pallas as pl
from jax.experimental.pallas import tpu as pltpu

D_MODEL = 1024
HEADS = 8
HEAD_DIM = 64
WIDTH = HEADS * HEAD_DIM
BLK = 256
MOBA_TOPK = 3
IDX_HEADS = 4
IDX_DIM = 64
DSA_TOPK_MAX = 256
REL_BUCKETS = 32
REL_MAX_DIST = 128
EPS = 1e-6
NEG = -1e30
M_INIT = -1e20
INT_MIN = -2 ** 31
INT16_MIN = -2 ** 15
PACK16 = 16
LANES = 128
VMEM_LIMIT = 56 * 1024 * 1024

F32 = jnp.float32
BF16 = jnp.bfloat16

CN_KA, CN_KB2, CN_KI4, CN_GA, CN_GB, CN_MA, CN_MB = 0, 512, 640, 896, 1408, 1920, 2944
N_NAT = 3968
RT_QA, RT_QB, RT_VA, RT_QI, RT_VB, RT_WI = 0, 512, 1024, 1536, 1792, 1856
N_TR = 1872
WI_ROWS = 8
ROWS_IN = 256
ONES_ROWS = 16
VROWS = HEAD_DIM + ONES_ROWS


def _bucket_starts():
    n = np.arange(0, 2 * REL_MAX_DIST + 2)
    max_exact = REL_BUCKETS // 2
    nf = np.maximum(n, 1).astype(np.float64)
    v = np.log(nf / max_exact) / math.log(REL_MAX_DIST / max_exact) * (REL_BUCKETS - max_exact)
    large = np.minimum(max_exact + np.trunc(v).astype(np.int64), REL_BUCKETS - 1)
    bucket = np.where(n < max_exact, n, large)
    frac = np.abs(v - np.round(v))[(n > max_exact) & (n < REL_MAX_DIST)]
    assert frac.min() > 1e-3
    return [int(np.argmax(bucket >= k)) for k in range(REL_BUCKETS)]


_STARTS = _bucket_starts()


def _dot(a, b):
    return jnp.dot(a, b, preferred_element_type=F32)


def _inproj_kernel(x_ref, gn_ref, wn_ref, wt_ref, ghn_ref, ght_ref,
                   ka_ref, kn_ref, sga_ref, sgb_ref, sma_ref, smb_ref, kmean_ref,
                   qat_ref, qbt_ref, vat_ref, qit_ref, vbt_ref, wit_ref):
    x = x_ref[...]
    r = lax.rsqrt(jnp.mean(x * x, axis=-1, keepdims=True) + EPS)
    hf = (x * r) * gn_ref[...]
    h = hf.astype(BF16)
    ht = hf.T.astype(BF16)

    gr = lax.broadcasted_iota(jnp.int32, (BLK, BLK), 0) // HEAD_DIM
    gc = lax.broadcasted_iota(jnp.int32, (BLK, BLK), 1) // HEAD_DIM
    group = jnp.where(gr == gc, 1.0, 0.0).astype(BF16)

    def proj(c0, n):
        return _dot(h, wn_ref[:, c0:c0 + n])

    def proj_t(r0, n):
        return _dot(wt_ref[r0:r0 + n, :], ht)

    def headnorm(y, c0):
        n = y.shape[-1]
        sq = y * y
        hi = sq.astype(BF16)
        lo = (sq - hi.astype(F32)).astype(BF16)
        g = group[:n, :n]
        ss = _dot(hi, g) + _dot(lo, g)
        return (y * lax.rsqrt(ss * (1.0 / HEAD_DIM) + EPS)) * ghn_ref[:, c0:c0 + n]

    def headnorm_t(yt, r0):
        ss = jnp.sum(yt * yt, axis=0, keepdims=True)
        return (yt * lax.rsqrt(ss * (1.0 / HEAD_DIM) + EPS)) * ght_ref[r0:r0 + HEAD_DIM, :]

    for c in range(0, WIDTH, BLK):
        kn = headnorm(proj(CN_KA + c, BLK), CN_KA + c)
        ka_ref[:, c:c + BLK] = kn.astype(BF16)
        for t in range(kn.shape[0] // BLK):
            kmean_ref[t, :, c:c + BLK] = jnp.mean(kn[t * BLK:(t + 1) * BLK], axis=0, keepdims=True)
        sga_ref[:, c:c + BLK] = jax.nn.silu(proj(CN_GA + c, BLK))
        sgb_ref[:, c:c + BLK] = jax.nn.silu(proj(CN_GB + c, BLK))
    kn_ref[:, 0:128] = headnorm(proj(CN_KB2, 128), CN_KB2).astype(BF16)
    kn_ref[:, 128:384] = proj(CN_KI4, 256).astype(BF16)
    for c in range(0, D_MODEL, BLK):
        sma_ref[:, c:c + BLK] = jax.nn.sigmoid(proj(CN_MA + c, BLK))
        smb_ref[:, c:c + BLK] = jax.nn.sigmoid(proj(CN_MB + c, BLK))

    def put_t(ref, r0, val):
        for t in range(ROWS_IN // BLK):
            ref[t, r0:r0 + val.shape[0], :] = val[:, t * BLK:(t + 1) * BLK]

    scale = HEAD_DIM ** -0.5
    ones = jnp.ones((ONES_ROWS, ROWS_IN), BF16)
    for hd in range(HEADS):
        r0 = hd * HEAD_DIM
        put_t(qat_ref, r0, (headnorm_t(proj_t(RT_QA + r0, HEAD_DIM), RT_QA + r0) * scale).astype(BF16))
        put_t(qbt_ref, r0, (headnorm_t(proj_t(RT_QB + r0, HEAD_DIM), RT_QB + r0) * scale).astype(BF16))
        put_t(vat_ref, hd * VROWS, proj_t(RT_VA + r0, HEAD_DIM).astype(BF16))
        put_t(vat_ref, hd * VROWS + HEAD_DIM, ones)
    put_t(qit_ref, 0, (proj_t(RT_QI, 256) * (IDX_DIM ** -0.5)).astype(BF16))
    put_t(vbt_ref, 0, proj_t(RT_VB, HEAD_DIM).astype(BF16))
    put_t(vbt_ref, HEAD_DIM, ones)
    put_t(wit_ref, 0, proj_t(RT_WI, 16)[:WI_ROWS] * (IDX_HEADS ** -0.5))


def _inproj(x2, gn, wn, wt, ghn, ght):
    m = x2.shape[0]
    nt = m // BLK
    tpb = ROWS_IN // BLK
    row = lambda n: pl.BlockSpec((ROWS_IN, n), lambda i: (i, 0))
    full = lambda a: pl.BlockSpec(a.shape, lambda i: (0,) * a.ndim)
    tile = lambda n: pl.BlockSpec((tpb, n, BLK), lambda i: (i, 0, 0))
    out_shapes = [
        jax.ShapeDtypeStruct((m, WIDTH), BF16),
        jax.ShapeDtypeStruct((m, 384), BF16),
        jax.ShapeDtypeStruct((m, WIDTH), F32),
        jax.ShapeDtypeStruct((m, WIDTH), F32),
        jax.ShapeDtypeStruct((m, D_MODEL), F32),
        jax.ShapeDtypeStruct((m, D_MODEL), F32),
        jax.ShapeDtypeStruct((nt, 1, WIDTH), F32),
        jax.ShapeDtypeStruct((nt, WIDTH, BLK), BF16),
        jax.ShapeDtypeStruct((nt, WIDTH, BLK), BF16),
        jax.ShapeDtypeStruct((nt, HEADS * VROWS, BLK), BF16),
        jax.ShapeDtypeStruct((nt, 256, BLK), BF16),
        jax.ShapeDtypeStruct((nt, VROWS, BLK), BF16),
        jax.ShapeDtypeStruct((nt, WI_ROWS, BLK), F32),
    ]
    out_specs = [row(WIDTH), row(384), row(WIDTH), row(WIDTH), row(D_MODEL), row(D_MODEL),
                 pl.BlockSpec((tpb, 1, WIDTH), lambda i: (i, 0, 0)),
                 tile(WIDTH), tile(WIDTH), tile(HEADS * VROWS), tile(256), tile(VROWS), tile(WI_ROWS)]
    return pl.pallas_call(
        _inproj_kernel,
        grid=(nt // tpb,),
        in_specs=[row(D_MODEL), full(gn), full(wn), full(wt), full(ghn), full(ght)],
        out_specs=out_specs,
        out_shape=out_shapes,
        compiler_params=pltpu.CompilerParams(
            dimension_semantics=("arbitrary",), vmem_limit_bytes=VMEM_LIMIT),
        name="inproj",
    )(x2, gn, wn, wt, ghn, ght)


def _build_bias_tiles(rel_ref, bias_ref, head0):
    kpos = lax.broadcasted_iota(jnp.int32, (BLK, BLK), 0)
    qpos = lax.broadcasted_iota(jnp.int32, (BLK, BLK), 1)
    for kind in range(2):
        dist = qpos - kpos + BLK * kind
        for h in range(HEADS):
            t = jnp.full((BLK, BLK), rel_ref[0, head0 + h], F32)
            for k in range(1, REL_BUCKETS):
                t = jnp.where(dist >= _STARTS[k], rel_ref[k, head0 + h], t)
            if kind == 0:
                t = jnp.where(dist >= 0, t, NEG)
            bias_ref[3 * h + kind] = t
    for h in range(HEADS):
        bias_ref[3 * h + 2] = jnp.full((BLK, BLK), rel_ref[REL_BUCKETS - 1, head0 + h], F32)


def _store_masked_queries(qt_ref, qm_ref):
    rowh = lax.broadcasted_iota(jnp.int32, (LANES, BLK), 0)
    for p in range(HEADS // 2):
        q2 = qt_ref[0, LANES * p:LANES * (p + 1), :]
        for hh in range(2):
            qm_ref[2 * p + hh] = jnp.where((rowh >= HEAD_DIM) == bool(hh), q2, jnp.zeros_like(q2))


def _flash_over_blocks(npair, score_fn, pv_fn, s_ref, acc_ref, sel_fn=None):
    def stage(t, ms):
        out = []
        for h in range(HEADS):
            m_new = ms[h]
            for u in range(2):
                j = 2 * t + u
                s = score_fn(j, h)
                s_ref[h, u * BLK:(u + 1) * BLK, :] = s
                cand = jnp.max(s, axis=0, keepdims=True)
                if sel_fn is not None:
                    cand = jnp.where(sel_fn(j, h), cand, M_INIT)
                m_new = jnp.maximum(m_new, cand)
            out.append(m_new)
        return tuple(out)

    def finish(t, ms_old, ms_new):
        for h in range(HEADS):
            acc = jnp.exp(ms_old[h] - ms_new[h]) * acc_ref[h]
            for u in range(2):
                j = 2 * t + u
                p = jnp.exp(s_ref[h, u * BLK:(u + 1) * BLK, :] - ms_new[h])
                pv = pv_fn(j, h, p.astype(BF16))
                acc = acc + (pv if sel_fn is None else jnp.where(sel_fn(j, h), pv, 0.0))
            acc_ref[h] = acc

    m0 = tuple(jnp.full((1, BLK), M_INIT, F32) for _ in range(HEADS))
    m1 = stage(0, m0)

    def body(t, carry):
        m_prev, m_cur = carry
        finish(t, m_prev, m_cur)
        return m_cur, stage(t + 1, m_cur)

    m_prev, m_cur = lax.fori_loop(0, npair - 1, body, (m0, m1))
    finish(npair - 1, m_prev, m_cur)


def _write_output(acc_ref, sg_ref, o_ref):
    def head_out(h):
        return acc_ref[h, 0:HEAD_DIM, :] / acc_ref[h, HEAD_DIM:HEAD_DIM + 1, :]

    for p in range(HEADS // 2):
        ot = jnp.concatenate([head_out(2 * p), head_out(2 * p + 1)], axis=0)
        cs = slice(LANES * p, LANES * (p + 1))
        o_ref[0, :, cs] = (ot.T * sg_ref[0, :, cs]).astype(BF16)


def _moba_kernel(rel_ref, qt_ref, k_ref, vt_ref, kmean_ref, sg_ref, o_ref,
                 bias_ref, qm_ref, acc_ref, s_ref):
    b = pl.program_id(0)
    i = pl.program_id(1)
    nb = kmean_ref.shape[1]

    @pl.when((b == 0) & (i == 0))
    def _():
        _build_bias_tiles(rel_ref, bias_ref, 0)

    _store_masked_queries(qt_ref, qm_ref)
    acc_ref[...] = jnp.zeros_like(acc_ref)

    nidx = lax.broadcasted_iota(jnp.int32, (nb, BLK), 0)
    sel_bits = []
    for h in range(HEADS):
        cs = slice(LANES * (h // 2), LANES * (h // 2 + 1))
        gate = _dot(kmean_ref[0, :, cs].astype(BF16), qm_ref[h])
        g = jnp.where(nidx < i, gate, -jnp.inf)
        bits = jnp.left_shift(1, jnp.full((1, BLK), i, jnp.int32))
        for _ in range(MOBA_TOPK):
            mx = jnp.max(g, axis=0, keepdims=True)
            am = jnp.min(jnp.where(g == mx, nidx, nb), axis=0, keepdims=True)
            bits = bits | jnp.where(mx > -jnp.inf, jnp.left_shift(1, am), 0)
            g = jnp.where(nidx == am, -jnp.inf, g)
        sel_bits.append(bits)

    def score(j, h):
        rows = pl.ds(pl.multiple_of(j * BLK, BLK), BLK)
        cs = slice(LANES * (h // 2), LANES * (h // 2 + 1))
        return _dot(k_ref[0, rows, cs], qm_ref[h]) + bias_ref[3 * h + jnp.clip(i - j, 0, 2)]

    def pv(j, h, p):
        return _dot(vt_ref[0, j, h * VROWS:(h + 1) * VROWS, :], p)

    def sel(j, h):
        return (jnp.right_shift(sel_bits[h], j) & 1) == 1

    _flash_over_blocks((i + 2) // 2, score, pv, s_ref, acc_ref, sel)
    _write_output(acc_ref, sg_ref, o_ref)


def _moba(rel_bias, qat, ka, vat, kmean, sga):
    bsz, s, _ = ka.shape
    nb = s // BLK
    qspec = pl.BlockSpec((1, BLK, WIDTH), lambda b, i: (b, i, 0))
    return pl.pallas_call(
        _moba_kernel,
        grid=(bsz, nb),
        in_specs=[pl.BlockSpec(memory_space=pltpu.SMEM),
                  pl.BlockSpec((1, WIDTH, BLK), lambda b, i: (b * nb + i, 0, 0)),
                  pl.BlockSpec((1, s, WIDTH), lambda b, i: (b, 0, 0)),
                  pl.BlockSpec((1, nb, HEADS * VROWS, BLK), lambda b, i: (b, 0, 0, 0)),
                  pl.BlockSpec((1, nb, WIDTH), lambda b, i: (b, 0, 0)), qspec],
        out_specs=qspec,
        out_shape=jax.ShapeDtypeStruct((bsz, s, WIDTH), BF16),
        scratch_shapes=[pltpu.VMEM((3 * HEADS, BLK, BLK), F32),
                        pltpu.VMEM((HEADS, LANES, BLK), BF16),
                        pltpu.VMEM((HEADS, VROWS, BLK), F32),
                        pltpu.VMEM((HEADS, 2 * BLK, BLK), F32)],
        compiler_params=pltpu.CompilerParams(
            dimension_semantics=("arbitrary", "arbitrary"), vmem_limit_bytes=VMEM_LIMIT),
        name="moba",
    )(rel_bias, qat, ka, vat.reshape(bsz, nb, HEADS * VROWS, BLK), kmean, sga)


def _dsa_kernel(rel_ref, qt_ref, qit_ref, wit_ref, kn_ref, vt_ref, sg_ref, o_ref,
                bias_ref, qm_ref, acc_ref, s_ref, key_ref, khi_ref, klo_ref, madd_ref, *, topk):
    b = pl.program_id(0)
    i = pl.program_id(1)
    nblk = i + 1

    @pl.when((b == 0) & (i == 0))
    def _():
        _build_bias_tiles(rel_ref, bias_ref, HEADS)

    _store_masked_queries(qt_ref, qm_ref)
    acc_ref[...] = jnp.zeros_like(acc_ref)

    kpos = lax.broadcasted_iota(jnp.int32, (BLK, BLK), 0)
    qpos = lax.broadcasted_iota(jnp.int32, (BLK, BLK), 1)

    qit = qit_ref[0]
    wit = wit_ref[0]
    qims = [jnp.where((kpos // IDX_DIM) == h, qit, jnp.zeros_like(qit)) for h in range(IDX_HEADS)]
    whs = [wit[h:h + 1, :] for h in range(IDX_HEADS)]

    def score_block(j, causal):
        rows = pl.ds(pl.multiple_of(j * BLK, BLK), BLK)
        ki4 = kn_ref[0, rows, 128:384]
        score = jnp.zeros((BLK, BLK), F32)
        for h in range(IDX_HEADS):
            score = score + jnp.maximum(_dot(ki4, qims[h]), 0.0) * whs[h]
        bits = lax.bitcast_convert_type(score, jnp.int32)
        key = bits ^ (jnp.right_shift(bits, 31) & 0x7FFFFFFF)
        if causal:
            key = jnp.where(kpos <= qpos, key, INT_MIN)
        key_ref[j] = key
        khi_ref[j] = jnp.right_shift(key, 16).astype(jnp.int16)

    pad = jnp.minimum(nblk, key_ref.shape[0] - 1)
    key_ref[pad] = jnp.full((BLK, BLK), INT_MIN, jnp.int32)
    khi_ref[pad] = jnp.full((BLK, BLK), INT16_MIN, jnp.int16)
    klo_ref[pad] = jnp.full((BLK, BLK), INT16_MIN, jnp.int16)

    def score_body(j, _):
        score_block(j, False)
        return 0

    lax.fori_loop(0, i, score_body, 0)
    score_block(i, True)
    npair = (nblk + 1) // 2

    def count16(ref, cand, strict=False):
        c16 = jnp.broadcast_to(cand, (PACK16, BLK)).astype(jnp.int16)

        def cb(t, acc):
            for j in (2 * t, 2 * t + 1):
                k3 = ref[j].reshape(BLK // PACK16, PACK16, BLK)
                for g in range(BLK // PACK16):
                    hit = (k3[g] > c16) if strict else (k3[g] >= c16)
                    acc = acc + jnp.where(hit, jnp.int16(1), jnp.int16(0))
            return acc

        acc = lax.fori_loop(0, npair, cb, jnp.zeros((PACK16, BLK), jnp.int16))
        return jnp.sum(acc.astype(jnp.int32), axis=0, keepdims=True)

    def bisect16(ref, want):
        zero = jnp.zeros((1, BLK), jnp.int32)
        t0 = jnp.where(count16(ref, zero) >= want, zero, INT16_MIN)

        def bit_body(t, thr):
            cand = thr + jnp.left_shift(1, 14 - t)
            return jnp.where(count16(ref, cand) >= want, cand, thr)

        return lax.fori_loop(0, 15, bit_body, t0)

    thr_hi = bisect16(khi_ref, topk)
    above_hi = count16(khi_ref, thr_hi, strict=True)

    def low_body(j, _):
        key = key_ref[j]
        low = (key & 0xFFFF) - 0x8000
        member = jnp.right_shift(key, 16) == thr_hi
        klo_ref[j] = jnp.where(member, low, INT16_MIN).astype(jnp.int16)
        return 0

    lax.fori_loop(0, nblk, low_body, 0)
    thr_lo = bisect16(klo_ref, topk - above_hi)
    thr = thr_hi * 65536 + (thr_lo + 0x8000)
    above = above_hi + count16(klo_ref, thr_lo, strict=True)

    need = jnp.where(thr > INT_MIN, topk - above, 0).astype(F32)
    lower = jnp.where(qpos <= kpos, 1.0, 0.0).astype(BF16)

    def mask_body(t, carry):
        for j in (2 * t, 2 * t + 1):
            k = key_ref[j]
            eq = k == thr
            pref = _dot(lower, jnp.where(eq, 1.0, 0.0).astype(BF16))
            sel = (k > thr) | (eq & ((carry + pref) <= need))
            madd_ref[j] = jnp.where(sel, 0.0, NEG)
            carry = carry + pref[BLK - 1:BLK, :]
        return carry

    lax.fori_loop(0, npair, mask_body, jnp.zeros((1, BLK), F32))

    def score(j, h):
        rows = pl.ds(pl.multiple_of(j * BLK, BLK), BLK)
        return (_dot(kn_ref[0, rows, 0:128], qm_ref[h])
                + bias_ref[3 * h + jnp.clip(i - j, 0, 2)] + madd_ref[j])

    _flash_over_blocks(npair, score, lambda j, h, p: _dot(vt_ref[0, j], p), s_ref, acc_ref)
    _write_output(acc_ref, sg_ref, o_ref)


def _dsa(rel_bias, qbt, qit, wit, kn, vbt, sgb):
    bsz, s, _ = kn.shape
    nb = s // BLK
    topk = min(DSA_TOPK_MAX, s // 4)
    qspec = pl.BlockSpec((1, BLK, WIDTH), lambda b, i: (b, i, 0))
    tile = lambda n: pl.BlockSpec((1, n, BLK), lambda b, i: (b * nb + i, 0, 0))
    return pl.pallas_call(
        functools.partial(_dsa_kernel, topk=topk),
        grid=(bsz, nb),
        in_specs=[pl.BlockSpec(memory_space=pltpu.SMEM), tile(WIDTH), tile(256), tile(WI_ROWS),
                  pl.BlockSpec((1, s, 384), lambda b, i: (b, 0, 0)),
                  pl.BlockSpec((1, nb, VROWS, BLK), lambda b, i: (b, 0, 0, 0)), qspec],
        out_specs=qspec,
        out_shape=jax.ShapeDtypeStruct((bsz, s, WIDTH), BF16),
        scratch_shapes=[pltpu.VMEM((3 * HEADS, BLK, BLK), F32),
                        pltpu.VMEM((HEADS, LANES, BLK), BF16),
                        pltpu.VMEM((HEADS, VROWS, BLK), F32),
                        pltpu.VMEM((HEADS, 2 * BLK, BLK), F32),
                        pltpu.VMEM((nb, BLK, BLK), jnp.int32),
                        pltpu.VMEM((nb, BLK, BLK), jnp.int16),
                        pltpu.VMEM((nb, BLK, BLK), jnp.int16),
                        pltpu.VMEM((nb, BLK, BLK), F32)],
        compiler_params=pltpu.CompilerParams(
            dimension_semantics=("arbitrary", "arbitrary"), vmem_limit_bytes=VMEM_LIMIT),
        name="dsa",
    )(rel_bias, qbt, qit, wit, kn, vbt.reshape(bsz, nb, VROWS, BLK), sgb)


def _outproj_kernel(x_ref, ya_ref, yb_ref, sma_ref, smb_ref, wa_ref, wb_ref, wo_ref, o_ref):
    merged = (sma_ref[...] * _dot(ya_ref[...], wa_ref[...])
              + smb_ref[...] * _dot(yb_ref[...], wb_ref[...]))
    o_ref[...] = x_ref[...] + _dot(merged.astype(BF16), wo_ref[...])


def _outproj(x2, ya, yb, sma, smb, wa, wb, wo):
    m = x2.shape[0]
    row = lambda n: pl.BlockSpec((BLK, n), lambda i: (i, 0))
    full = lambda a: pl.BlockSpec(a.shape, lambda i: (0,) * a.ndim)
    return pl.pallas_call(
        _outproj_kernel,
        grid=(m // BLK,),
        in_specs=[row(D_MODEL), row(WIDTH), row(WIDTH), row(D_MODEL), row(D_MODEL),
                  full(wa), full(wb), full(wo)],
        out_specs=row(D_MODEL),
        out_shape=jax.ShapeDtypeStruct((m, D_MODEL), F32),
        compiler_params=pltpu.CompilerParams(
            dimension_semantics=("arbitrary",), vmem_limit_bytes=VMEM_LIMIT),
        name="outproj",
    )(x2, ya, yb, sma, smb, wa, wb, wo)


def _pack_w_in(w):
    o = 0
    parts = {}
    for name, n in (("qa", WIDTH), ("ka", WIDTH), ("va", WIDTH), ("ga", WIDTH), ("qb", WIDTH),
                    ("kb", HEAD_DIM), ("vb", HEAD_DIM), ("gb", WIDTH), ("qi", IDX_HEADS * IDX_DIM),
                    ("ki", IDX_DIM), ("wi", IDX_HEADS), ("ma", D_MODEL), ("mb", D_MODEL)):
        parts[name] = w[:, o:o + n]
        o += n
    wn = jnp.concatenate([parts["ka"], parts["kb"], parts["kb"],
                          parts["ki"], parts["ki"], parts["ki"], parts["ki"],
                          parts["ga"], parts["gb"], parts["ma"], parts["mb"]], axis=1).astype(BF16)
    pad = jnp.zeros((w.shape[0], N_TR - RT_WI - IDX_HEADS), w.dtype)
    wt = jnp.concatenate([parts["qa"], parts["qb"], parts["va"], parts["qi"], parts["vb"],
                          parts["wi"], pad], axis=1).T.astype(BF16)
    assert wn.shape[1] == N_NAT and wt.shape[0] == N_TR
    return wn, wt


def _layer(x2, bsz, s, gn, w_in, qn_a, kn_a, qn_b, kn_b, w_br_a, w_br_b, w_out, rel_bias):
    wn, wt = _pack_w_in(w_in)
    ghn = jnp.concatenate([jnp.tile(kn_a, HEADS), jnp.tile(kn_b, 2)])[None, :]
    ght = jnp.concatenate([jnp.tile(qn_a, HEADS), jnp.tile(qn_b, HEADS)])[:, None]
    (ka, kn, sga, sgb, sma, smb, kmean, qat, qbt, vat, qit, vbt, wit) = _inproj(
        x2, gn[None, :], wn, wt, ghn, ght)
    r3 = lambda a: a.reshape(bsz, s, a.shape[-1])
    ya = _moba(rel_bias, qat, r3(ka), vat, kmean.reshape(bsz, s // BLK, WIDTH), r3(sga))
    yb = _dsa(rel_bias, qbt, qit, wit, r3(kn), vbt, r3(sgb))
    return _outproj(x2, ya.reshape(-1, WIDTH), yb.reshape(-1, WIDTH), sma, smb,
                    w_br_a.astype(BF16), w_br_b.astype(BF16), w_out.astype(BF16))


def kernel(x, norm_g, w_in, q_norm_a, k_norm_a, q_norm_b, k_norm_b,
           w_branch_a, w_branch_b, w_out, rel_bias):
    bsz, s, d = x.shape
    assert d == D_MODEL and s % BLK == 0 and s // BLK <= 31
    h = x.reshape(bsz * s, d)
    for l in range(norm_g.shape[0]):
        h = _layer(h, bsz, s, norm_g[l], w_in[l], q_norm_a[l], k_norm_a[l], q_norm_b[l],
                   k_norm_b[l], w_branch_a[l], w_branch_b[l], w_out[l], rel_bias)
    return h.reshape(bsz, s, d)
```

```python
import functools
import math

import numpy as np
import jax
import jax.numpy as jnp
from jax import lax
from jax.experimental import pallas as pl
from jax.experimental.pallas import tpu as pltpu

D_MODEL = 1024
HEADS = 8
HEAD_DIM = 64
WIDTH = HEADS * HEAD_DIM
BLK = 256
MOBA_TOPK = 3
IDX_HEADS = 4
IDX_DIM = 64
DSA_TOPK_MAX = 256
REL_BUCKETS = 32
REL_MAX_DIST = 128
EPS = 1e-6
NEG = -1e30
M_INIT = -1e20
INT_MIN = -2 ** 31
INT16_MIN = -2 ** 15
PACK16 = 16
LANES = 128
VMEM_LIMIT = 56 * 1024 * 1024

F32 = jnp.float32
BF16 = jnp.bfloat16

CN_KA, CN_KB2, CN_KI4, CN_GA, CN_GB, CN_MA, CN_MB = 0, 512, 640, 896, 1408, 1920, 2944
N_NAT = 3968
RT_QA, RT_QB, RT_VA, RT_QI, RT_VB, RT_WI = 0, 512, 1024, 1536, 1792, 1856
N_TR = 1872
WI_ROWS = 8
ROWS_IN = 512
ROWS_OUT = 512
ONES_ROWS = 16
VROWS = HEAD_DIM + ONES_ROWS


def _bucket_starts():
    n = np.arange(0, 2 * REL_MAX_DIST + 2)
    max_exact = REL_BUCKETS // 2
    nf = np.maximum(n, 1).astype(np.float64)
    v = np.log(nf / max_exact) / math.log(REL_MAX_DIST / max_exact) * (REL_BUCKETS - max_exact)
    large = np.minimum(max_exact + np.trunc(v).astype(np.int64), REL_BUCKETS - 1)
    bucket = np.where(n < max_exact, n, large)
    frac = np.abs(v - np.round(v))[(n > max_exact) & (n < REL_MAX_DIST)]
    assert frac.min() > 1e-3
    return [int(np.argmax(bucket >= k)) for k in range(REL_BUCKETS)]


_STARTS = _bucket_starts()


def _dot(a, b):
    return jnp.dot(a, b, preferred_element_type=F32)


def _inproj_kernel(x_ref, gn_ref, wn_ref, wt_ref, ghn_ref, ght_ref,
                   ka_ref, kn_ref, sga_ref, sgb_ref, sma_ref, smb_ref, kmean_ref,
                   qat_ref, qbt_ref, vat_ref, qit_ref, vbt_ref, wit_ref):
    x = x_ref[...]
    r = lax.rsqrt(jnp.mean(x * x, axis=-1, keepdims=True) + EPS)
    hf = (x * r) * gn_ref[...]
    h = hf.astype(BF16)
    ht = hf.T.astype(BF16)

    gr = lax.broadcasted_iota(jnp.int32, (BLK, BLK), 0) // HEAD_DIM
    gc = lax.broadcasted_iota(jnp.int32, (BLK, BLK), 1) // HEAD_DIM
    group = jnp.where(gr == gc, 1.0, 0.0).astype(BF16)

    def proj(c0, n):
        return _dot(h, wn_ref[:, c0:c0 + n])

    def proj_t(r0, n):
        return _dot(wt_ref[r0:r0 + n, :], ht)

    def headnorm(y, c0):
        n = y.shape[-1]
        sq = y * y
        hi = sq.astype(BF16)
        lo = (sq - hi.astype(F32)).astype(BF16)
        g = group[:n, :n]
        ss = _dot(hi, g) + _dot(lo, g)
        return (y * lax.rsqrt(ss * (1.0 / HEAD_DIM) + EPS)) * ghn_ref[:, c0:c0 + n]

    def headnorm_t(yt, r0):
        ss = jnp.sum(yt * yt, axis=0, keepdims=True)
        return (yt * lax.rsqrt(ss * (1.0 / HEAD_DIM) + EPS)) * ght_ref[r0:r0 + HEAD_DIM, :]

    for c in range(0, WIDTH, BLK):
        kn = headnorm(proj(CN_KA + c, BLK), CN_KA + c)
        ka_ref[:, c:c + BLK] = kn.astype(BF16)
        for t in range(kn.shape[0] // BLK):
            kmean_ref[t, :, c:c + BLK] = jnp.mean(kn[t * BLK:(t + 1) * BLK], axis=0, keepdims=True)
        sga_ref[:, c:c + BLK] = jax.nn.silu(proj(CN_GA + c, BLK))
        sgb_ref[:, c:c + BLK] = jax.nn.silu(proj(CN_GB + c, BLK))
    kn_ref[:, 0:128] = headnorm(proj(CN_KB2, 128), CN_KB2).astype(BF16)
    kn_ref[:, 128:384] = proj(CN_KI4, 256).astype(BF16)
    for c in range(0, D_MODEL, BLK):
        sma_ref[:, c:c + BLK] = jax.nn.sigmoid(proj(CN_MA + c, BLK))
        smb_ref[:, c:c + BLK] = jax.nn.sigmoid(proj(CN_MB + c, BLK))

    def put_t(ref, r0, val):
        for t in range(ROWS_IN // BLK):
            ref[t, r0:r0 + val.shape[0], :] = val[:, t * BLK:(t + 1) * BLK]

    scale = HEAD_DIM ** -0.5
    ones = jnp.ones((ONES_ROWS, ROWS_IN), BF16)
    qat = proj_t(RT_QA, WIDTH)
    qbt = proj_t(RT_QB, WIDTH)
    vat = proj_t(RT_VA, WIDTH)
    for hd in range(HEADS):
        r0 = hd * HEAD_DIM
        put_t(qat_ref, r0, (headnorm_t(qat[r0:r0 + HEAD_DIM], RT_QA + r0) * scale).astype(BF16))
        put_t(qbt_ref, r0, (headnorm_t(qbt[r0:r0 + HEAD_DIM], RT_QB + r0) * scale).astype(BF16))
        put_t(vat_ref, hd * VROWS, vat[r0:r0 + HEAD_DIM].astype(BF16))
        put_t(vat_ref, hd * VROWS + HEAD_DIM, ones)
    put_t(qit_ref, 0, (proj_t(RT_QI, 256) * (IDX_DIM ** -0.5)).astype(BF16))
    tail = proj_t(RT_VB, N_TR - RT_VB)
    put_t(vbt_ref, 0, tail[0:HEAD_DIM].astype(BF16))
    put_t(vbt_ref, HEAD_DIM, ones)
    put_t(wit_ref, 0, tail[RT_WI - RT_VB:RT_WI - RT_VB + WI_ROWS] * (IDX_HEADS ** -0.5))


def _inproj(x2, gn, wn, wt, ghn, ght):
    m = x2.shape[0]
    nt = m // BLK
    tpb = ROWS_IN // BLK
    row = lambda n: pl.BlockSpec((ROWS_IN, n), lambda i: (i, 0))
    full = lambda a: pl.BlockSpec(a.shape, lambda i: (0,) * a.ndim)
    tile = lambda n: pl.BlockSpec((tpb, n, BLK), lambda i: (i, 0, 0))
    out_shapes = [
        jax.ShapeDtypeStruct((m, WIDTH), BF16),
        jax.ShapeDtypeStruct((m, 384), BF16),
        jax.ShapeDtypeStruct((m, WIDTH), F32),
        jax.ShapeDtypeStruct((m, WIDTH), F32),
        jax.ShapeDtypeStruct((m, D_MODEL), F32),
        jax.ShapeDtypeStruct((m, D_MODEL), F32),
        jax.ShapeDtypeStruct((nt, 1, WIDTH), F32),
        jax.ShapeDtypeStruct((nt, WIDTH, BLK), BF16),
        jax.ShapeDtypeStruct((nt, WIDTH, BLK), BF16),
        jax.ShapeDtypeStruct((nt, HEADS * VROWS, BLK), BF16),
        jax.ShapeDtypeStruct((nt, 256, BLK), BF16),
        jax.ShapeDtypeStruct((nt, VROWS, BLK), BF16),
        jax.ShapeDtypeStruct((nt, WI_ROWS, BLK), F32),
    ]
    out_specs = [row(WIDTH), row(384), row(WIDTH), row(WIDTH), row(D_MODEL), row(D_MODEL),
                 pl.BlockSpec((tpb, 1, WIDTH), lambda i: (i, 0, 0)),
                 tile(WIDTH), tile(WIDTH), tile(HEADS * VROWS), tile(256), tile(VROWS), tile(WI_ROWS)]
    return pl.pallas_call(
        _inproj_kernel,
        grid=(nt // tpb,),
        in_specs=[row(D_MODEL), full(gn), full(wn), full(wt), full(ghn), full(ght)],
        out_specs=out_specs,
        out_shape=out_shapes,
        compiler_params=pltpu.CompilerParams(
            dimension_semantics=("arbitrary",), vmem_limit_bytes=VMEM_LIMIT),
        name="inproj",
    )(x2, gn, wn, wt, ghn, ght)


def _build_bias_tiles(rel_ref, bias_ref, head0):
    kpos = lax.broadcasted_iota(jnp.int32, (BLK, BLK), 0)
    qpos = lax.broadcasted_iota(jnp.int32, (BLK, BLK), 1)
    for kind in range(2):
        dist = qpos - kpos + BLK * kind
        for h in range(HEADS):
            t = jnp.full((BLK, BLK), rel_ref[0, head0 + h], F32)
            for k in range(1, REL_BUCKETS):
                t = jnp.where(dist >= _STARTS[k], rel_ref[k, head0 + h], t)
            if kind == 0:
                t = jnp.where(dist >= 0, t, NEG)
            bias_ref[3 * h + kind] = t
    for h in range(HEADS):
        bias_ref[3 * h + 2] = jnp.full((BLK, BLK), rel_ref[REL_BUCKETS - 1, head0 + h], F32)


def _store_masked_queries(qt_ref, qm_ref):
    rowh = lax.broadcasted_iota(jnp.int32, (LANES, BLK), 0)
    for p in range(HEADS // 2):
        q2 = qt_ref[0, LANES * p:LANES * (p + 1), :]
        for hh in range(2):
            qm_ref[2 * p + hh] = jnp.where((rowh >= HEAD_DIM) == bool(hh), q2, jnp.zeros_like(q2))


def _flash_over_blocks(npair, score_fn, pv_fn, s_ref, acc_ref, sel_fn=None):
    def stage(t, ms):
        out = []
        for h in range(HEADS):
            m_new = ms[h]
            for u in range(2):
                j = 2 * t + u
                s = score_fn(j, h)
                s_ref[h, u * BLK:(u + 1) * BLK, :] = s
                cand = jnp.max(s, axis=0, keepdims=True)
                if sel_fn is not None:
                    cand = jnp.where(sel_fn(j, h), cand, M_INIT)
                m_new = jnp.maximum(m_new, cand)
            out.append(m_new)
        return tuple(out)

    def finish(t, ms_old, ms_new):
        for h in range(HEADS):
            acc = jnp.exp(ms_old[h] - ms_new[h]) * acc_ref[h]
            for u in range(2):
                j = 2 * t + u
                p = jnp.exp(s_ref[h, u * BLK:(u + 1) * BLK, :] - ms_new[h])
                pv = pv_fn(j, h, p.astype(BF16))
                acc = acc + (pv if sel_fn is None else jnp.where(sel_fn(j, h), pv, 0.0))
            acc_ref[h] = acc

    m0 = tuple(jnp.full((1, BLK), M_INIT, F32) for _ in range(HEADS))
    m1 = stage(0, m0)

    def body(t, carry):
        m_prev, m_cur = carry
        finish(t, m_prev, m_cur)
        return m_cur, stage(t + 1, m_cur)

    m_prev, m_cur = lax.fori_loop(0, npair - 1, body, (m0, m1))
    finish(npair - 1, m_prev, m_cur)


def _write_output(acc_ref, sg_ref, o_ref):
    def head_out(h):
        return acc_ref[h, 0:HEAD_DIM, :] / acc_ref[h, HEAD_DIM:HEAD_DIM + 1, :]

    for p in range(HEADS // 2):
        ot = jnp.concatenate([head_out(2 * p), head_out(2 * p + 1)], axis=0)
        cs = slice(LANES * p, LANES * (p + 1))
        o_ref[0, :, cs] = (ot.T * sg_ref[0, :, cs]).astype(BF16)


def _moba_kernel(rel_ref, qt_ref, k_ref, vt_ref, kmean_ref, sg_ref, o_ref,
                 bias_ref, qm_ref, acc_ref, s_ref):
    b = pl.program_id(0)
    i = pl.program_id(1)
    nb = kmean_ref.shape[1]

    @pl.when((b == 0) & (i == 0))
    def _():
        _build_bias_tiles(rel_ref, bias_ref, 0)

    _store_masked_queries(qt_ref, qm_ref)
    acc_ref[...] = jnp.zeros_like(acc_ref)

    nidx = lax.broadcasted_iota(jnp.int32, (nb, BLK), 0)
    sel_bits = []
    for h in range(HEADS):
        cs = slice(LANES * (h // 2), LANES * (h // 2 + 1))
        gate = _dot(kmean_ref[0, :, cs].astype(BF16), qm_ref[h])
        g = jnp.where(nidx < i, gate, -jnp.inf)
        bits = jnp.left_shift(1, jnp.full((1, BLK), i, jnp.int32))
        for _ in range(MOBA_TOPK):
            mx = jnp.max(g, axis=0, keepdims=True)
            am = jnp.min(jnp.where(g == mx, nidx, nb), axis=0, keepdims=True)
            bits = bits | jnp.where(mx > -jnp.inf, jnp.left_shift(1, am), 0)
            g = jnp.where(nidx == am, -jnp.inf, g)
        sel_bits.append(bits)

    def score(j, h):
        rows = pl.ds(pl.multiple_of(j * BLK, BLK), BLK)
        cs = slice(LANES * (h // 2), LANES * (h // 2 + 1))
        return _dot(k_ref[0, rows, cs], qm_ref[h]) + bias_ref[3 * h + jnp.clip(i - j, 0, 2)]

    def pv(j, h, p):
        return _dot(vt_ref[0, j, h * VROWS:(h + 1) * VROWS, :], p)

    def sel(j, h):
        return (jnp.right_shift(sel_bits[h], j) & 1) == 1

    _flash_over_blocks((i + 2) // 2, score, pv, s_ref, acc_ref, sel)
    _write_output(acc_ref, sg_ref, o_ref)


def _moba(rel_bias, qat, ka, vat, kmean, sga):
    bsz, s, _ = ka.shape
    nb = s // BLK
    qspec = pl.BlockSpec((1, BLK, WIDTH), lambda b, i: (b, i, 0))
    return pl.pallas_call(
        _moba_kernel,
        grid=(bsz, nb),
        in_specs=[pl.BlockSpec(memory_space=pltpu.SMEM),
                  pl.BlockSpec((1, WIDTH, BLK), lambda b, i: (b * nb + i, 0, 0)),
                  pl.BlockSpec((1, s, WIDTH), lambda b, i: (b, 0, 0)),
                  pl.BlockSpec((1, nb, HEADS * VROWS, BLK), lambda b, i: (b, 0, 0, 0)),
                  pl.BlockSpec((1, nb, WIDTH), lambda b, i: (b, 0, 0)), qspec],
        out_specs=qspec,
        out_shape=jax.ShapeDtypeStruct((bsz, s, WIDTH), BF16),
        scratch_shapes=[pltpu.VMEM((3 * HEADS, BLK, BLK), F32),
                        pltpu.VMEM((HEADS, LANES, BLK), BF16),
                        pltpu.VMEM((HEADS, VROWS, BLK), F32),
                        pltpu.VMEM((HEADS, 2 * BLK, BLK), F32)],
        compiler_params=pltpu.CompilerParams(
            dimension_semantics=("arbitrary", "arbitrary"), vmem_limit_bytes=VMEM_LIMIT),
        name="moba",
    )(rel_bias, qat, ka, vat.reshape(bsz, nb, HEADS * VROWS, BLK), kmean, sga)


def _dsa_kernel(rel_ref, qt_ref, qit_ref, wit_ref, kn_ref, vt_ref, sg_ref, o_ref,
                bias_ref, qm_ref, acc_ref, s_ref, key_ref, khi_ref, klo_ref, madd_ref, *, topk):
    b = pl.program_id(0)
    i = pl.program_id(1)
    nblk = i + 1

    @pl.when((b == 0) & (i == 0))
    def _():
        _build_bias_tiles(rel_ref, bias_ref, HEADS)

    _store_masked_queries(qt_ref, qm_ref)
    acc_ref[...] = jnp.zeros_like(acc_ref)

    kpos = lax.broadcasted_iota(jnp.int32, (BLK, BLK), 0)
    qpos = lax.broadcasted_iota(jnp.int32, (BLK, BLK), 1)

    qit = qit_ref[0]
    wit = wit_ref[0]
    qims = [jnp.where((kpos // IDX_DIM) == h, qit, jnp.zeros_like(qit)) for h in range(IDX_HEADS)]
    whs = [wit[h:h + 1, :] for h in range(IDX_HEADS)]

    def score_block(j, causal):
        rows = pl.ds(pl.multiple_of(j * BLK, BLK), BLK)
        ki4 = kn_ref[0, rows, 128:384]
        score = jnp.zeros((BLK, BLK), F32)
        for h in range(IDX_HEADS):
            score = score + jnp.maximum(_dot(ki4, qims[h]), 0.0) * whs[h]
        bits = lax.bitcast_convert_type(score, jnp.int32)
        key = bits ^ (jnp.right_shift(bits, 31) & 0x7FFFFFFF)
        if causal:
            key = jnp.where(kpos <= qpos, key, INT_MIN)
        key_ref[j] = key
        khi_ref[j] = jnp.right_shift(key, 16).astype(jnp.int16)

    pad = jnp.minimum(nblk, key_ref.shape[0] - 1)
    key_ref[pad] = jnp.full((BLK, BLK), INT_MIN, jnp.int32)
    khi_ref[pad] = jnp.full((BLK, BLK), INT16_MIN, jnp.int16)
    klo_ref[pad] = jnp.full((BLK, BLK), INT16_MIN, jnp.int16)

    def score_pair(t, _):
        score_block(2 * t, False)
        score_block(2 * t + 1, False)
        return 0

    def score_single(_, c):
        score_block(i - 1, False)
        return c

    lax.fori_loop(0, i // 2, score_pair, 0)
    lax.fori_loop(0, i % 2, score_single, 0)
    score_block(i, True)
    npair = (nblk + 1) // 2

    def count16(ref, cand, strict=False):
        c16 = jnp.broadcast_to(cand, (PACK16, BLK)).astype(jnp.int16)

        def cb(t, acc):
            for j in (2 * t, 2 * t + 1):
                k3 = ref[j].reshape(BLK // PACK16, PACK16, BLK)
                for g in range(BLK // PACK16):
                    hit = (k3[g] > c16) if strict else (k3[g] >= c16)
                    acc = acc + jnp.where(hit, jnp.int16(1), jnp.int16(0))
            return acc

        acc = lax.fori_loop(0, npair, cb, jnp.zeros((PACK16, BLK), jnp.int16))
        return jnp.sum(acc.astype(jnp.int32), axis=0, keepdims=True)

    def bisect16(ref, want):
        zero = jnp.zeros((1, BLK), jnp.int32)
        t0 = jnp.where(count16(ref, zero) >= want, zero, INT16_MIN)

        def bit_body(t, thr):
            cand = thr + jnp.left_shift(1, 14 - t)
            return jnp.where(count16(ref, cand) >= want, cand, thr)

        return lax.fori_loop(0, 15, bit_body, t0)

    thr_hi = bisect16(khi_ref, topk)
    above_hi = count16(khi_ref, thr_hi, strict=True)

    def low_body(j, _):
        key = key_ref[j]
        low = (key & 0xFFFF) - 0x8000
        member = jnp.right_shift(key, 16) == thr_hi
        klo_ref[j] = jnp.where(member, low, INT16_MIN).astype(jnp.int16)
        return 0

    lax.fori_loop(0, nblk, low_body, 0)
    thr_lo = bisect16(klo_ref, topk - above_hi)
    thr = thr_hi * 65536 + (thr_lo + 0x8000)
    above = above_hi + count16(klo_ref, thr_lo, strict=True)

    need = jnp.where(thr > INT_MIN, topk - above, 0).astype(F32)
    lower = jnp.where(qpos <= kpos, 1.0, 0.0).astype(BF16)

    def mask_body(t, carry):
        for j in (2 * t, 2 * t + 1):
            k = key_ref[j]
            eq = k == thr
            pref = _dot(lower, jnp.where(eq, 1.0, 0.0).astype(BF16))
            sel = (k > thr) | (eq & ((carry + pref) <= need))
            madd_ref[j] = jnp.where(sel, 0.0, NEG)
            carry = carry + pref[BLK - 1:BLK, :]
        return carry

    lax.fori_loop(0, npair, mask_body, jnp.zeros((1, BLK), F32))

    def score(j, h):
        rows = pl.ds(pl.multiple_of(j * BLK, BLK), BLK)
        return (_dot(kn_ref[0, rows, 0:128], qm_ref[h])
                + bias_ref[3 * h + jnp.clip(i - j, 0, 2)] + madd_ref[j])

    _flash_over_blocks(npair, score, lambda j, h, p: _dot(vt_ref[0, j], p), s_ref, acc_ref)
    _write_output(acc_ref, sg_ref, o_ref)


def _dsa(rel_bias, qbt, qit, wit, kn, vbt, sgb):
    bsz, s, _ = kn.shape
    nb = s // BLK
    topk = min(DSA_TOPK_MAX, s // 4)
    qspec = pl.BlockSpec((1, BLK, WIDTH), lambda b, i: (b, i, 0))
    tile = lambda n: pl.BlockSpec((1, n, BLK), lambda b, i: (b * nb + i, 0, 0))
    return pl.pallas_call(
        functools.partial(_dsa_kernel, topk=topk),
        grid=(bsz, nb),
        in_specs=[pl.BlockSpec(memory_space=pltpu.SMEM), tile(WIDTH), tile(256), tile(WI_ROWS),
                  pl.BlockSpec((1, s, 384), lambda b, i: (b, 0, 0)),
                  pl.BlockSpec((1, nb, VROWS, BLK), lambda b, i: (b, 0, 0, 0)), qspec],
        out_specs=qspec,
        out_shape=jax.ShapeDtypeStruct((bsz, s, WIDTH), BF16),
        scratch_shapes=[pltpu.VMEM((3 * HEADS, BLK, BLK), F32),
                        pltpu.VMEM((HEADS, LANES, BLK), BF16),
                        pltpu.VMEM((HEADS, VROWS, BLK), F32),
                        pltpu.VMEM((HEADS, 2 * BLK, BLK), F32),
                        pltpu.VMEM((nb, BLK, BLK), jnp.int32),
                        pltpu.VMEM((nb, BLK, BLK), jnp.int16),
                        pltpu.VMEM((nb, BLK, BLK), jnp.int16),
                        pltpu.VMEM((nb, BLK, BLK), F32)],
        compiler_params=pltpu.CompilerParams(
            dimension_semantics=("arbitrary", "arbitrary"), vmem_limit_bytes=VMEM_LIMIT),
        name="dsa",
    )(rel_bias, qbt, qit, wit, kn, vbt.reshape(bsz, nb, VROWS, BLK), sgb)


def _outproj_kernel(x_ref, ya_ref, yb_ref, sma_ref, smb_ref, wa_ref, wb_ref, wo_ref, o_ref):
    merged = (sma_ref[...] * _dot(ya_ref[...], wa_ref[...])
              + smb_ref[...] * _dot(yb_ref[...], wb_ref[...]))
    o_ref[...] = x_ref[...] + _dot(merged.astype(BF16), wo_ref[...])


def _outproj(x2, ya, yb, sma, smb, wa, wb, wo):
    m = x2.shape[0]
    row = lambda n: pl.BlockSpec((ROWS_OUT, n), lambda i: (i, 0))
    full = lambda a: pl.BlockSpec(a.shape, lambda i: (0,) * a.ndim)
    return pl.pallas_call(
        _outproj_kernel,
        grid=(m // ROWS_OUT,),
        in_specs=[row(D_MODEL), row(WIDTH), row(WIDTH), row(D_MODEL), row(D_MODEL),
                  full(wa), full(wb), full(wo)],
        out_specs=row(D_MODEL),
        out_shape=jax.ShapeDtypeStruct((m, D_MODEL), F32),
        compiler_params=pltpu.CompilerParams(
            dimension_semantics=("arbitrary",), vmem_limit_bytes=VMEM_LIMIT),
        name="outproj",
    )(x2, ya, yb, sma, smb, wa, wb, wo)


def _pack_w_in(w):
    o = 0
    parts = {}
    for name, n in (("qa", WIDTH), ("ka", WIDTH), ("va", WIDTH), ("ga", WIDTH), ("qb", WIDTH),
                    ("kb", HEAD_DIM), ("vb", HEAD_DIM), ("gb", WIDTH), ("qi", IDX_HEADS * IDX_DIM),
                    ("ki", IDX_DIM), ("wi", IDX_HEADS), ("ma", D_MODEL), ("mb", D_MODEL)):
        parts[name] = w[:, o:o + n]
        o += n
    wn = jnp.concatenate([parts["ka"], parts["kb"], parts["kb"],
                          parts["ki"], parts["ki"], parts["ki"], parts["ki"],
                          parts["ga"], parts["gb"], parts["ma"], parts["mb"]], axis=1).astype(BF16)
    pad = jnp.zeros((w.shape[0], N_TR - RT_WI - IDX_HEADS), w.dtype)
    wt = jnp.concatenate([parts["qa"], parts["qb"], parts["va"], parts["qi"], parts["vb"],
                          parts["wi"], pad], axis=1).T.astype(BF16)
    assert wn.shape[1] == N_NAT and wt.shape[0] == N_TR
    return wn, wt


def _layer(x2, bsz, s, gn, w_in, qn_a, kn_a, qn_b, kn_b, w_br_a, w_br_b, w_out, rel_bias):
    wn, wt = _pack_w_in(w_in)
    ghn = jnp.concatenate([jnp.tile(kn_a, HEADS), jnp.tile(kn_b, 2)])[None, :]
    ght = jnp.concatenate([jnp.tile(qn_a, HEADS), jnp.tile(qn_b, HEADS)])[:, None]
    (ka, kn, sga, sgb, sma, smb, kmean, qat, qbt, vat, qit, vbt, wit) = _inproj(
        x2, gn[None, :], wn, wt, ghn, ght)
    r3 = lambda a: a.reshape(bsz, s, a.shape[-1])
    ya = _moba(rel_bias, qat, r3(ka), vat, kmean.reshape(bsz, s // BLK, WIDTH), r3(sga))
    yb = _dsa(rel_bias, qbt, qit, wit, r3(kn), vbt, r3(sgb))
    return _outproj(x2, ya.reshape(-1, WIDTH), yb.reshape(-1, WIDTH), sma, smb,
                    w_br_a.astype(BF16), w_br_b.astype(BF16), w_out.astype(BF16))


def kernel(x, norm_g, w_in, q_norm_a, k_norm_a, q_norm_b, k_norm_b,
           w_branch_a, w_branch_b, w_out, rel_bias):
    bsz, s, d = x.shape
    assert d == D_MODEL and s % BLK == 0 and s // BLK <= 31
    h = x.reshape(bsz * s, d)
    for l in range(norm_g.shape[0]):
        h = _layer(h, bsz, s, norm_g[l], w_in[l], q_norm_a[l], k_norm_a[l], q_norm_b[l],
                   k_norm_b[l], w_branch_a[l], w_branch_b[l], w_out[l], rel_bias)
    return h.reshape(bsz, s, d)
```

```python
import functools
import math

import numpy as np
import jax
import jax.numpy as jnp
from jax import lax
from jax.experimental import pallas as pl
from jax.experimental.pallas import tpu as pltpu

D_MODEL = 1024
HEADS = 8
HEAD_DIM = 64
WIDTH = HEADS * HEAD_DIM
BLK = 256
MOBA_TOPK = 3
IDX_HEADS = 4
IDX_DIM = 64
DSA_TOPK_MAX = 256
REL_BUCKETS = 32
REL_MAX_DIST = 128
EPS = 1e-6
NEG = -1e30
M_INIT = -1e20
MAX_PLAIN_LOGIT = 40.0
INT_MIN = -2 ** 31
INT16_MIN = -2 ** 15
PACK16 = 16
LANES = 128
VMEM_LIMIT = 56 * 1024 * 1024

F32 = jnp.float32
BF16 = jnp.bfloat16

CN_KA, CN_KB2, CN_KI4, CN_GA, CN_GB, CN_MA, CN_MB = 0, 512, 640, 896, 1408, 1920, 2944
N_NAT = 3968
RT_QA, RT_QB, RT_VA, RT_QI, RT_VB, RT_WI = 0, 512, 1024, 1536, 1792, 1856
N_TR = 1872
WI_ROWS = 8
ROWS_IN = 512
ROWS_OUT = 512
ONES_ROWS = 16
VROWS = HEAD_DIM + ONES_ROWS


def _bucket_starts():
    n = np.arange(0, 2 * REL_MAX_DIST + 2)
    max_exact = REL_BUCKETS // 2
    nf = np.maximum(n, 1).astype(np.float64)
    v = np.log(nf / max_exact) / math.log(REL_MAX_DIST / max_exact) * (REL_BUCKETS - max_exact)
    large = np.minimum(max_exact + np.trunc(v).astype(np.int64), REL_BUCKETS - 1)
    bucket = np.where(n < max_exact, n, large)
    frac = np.abs(v - np.round(v))[(n > max_exact) & (n < REL_MAX_DIST)]
    assert frac.min() > 1e-3
    return [int(np.argmax(bucket >= k)) for k in range(REL_BUCKETS)]


_STARTS = _bucket_starts()


def _dot(a, b):
    return jnp.dot(a, b, preferred_element_type=F32)


def _inproj_kernel(x_ref, gn_ref, wn_ref, wt_ref, ghn_ref, ght_ref,
                   ka_ref, kn_ref, sga_ref, sgb_ref, sma_ref, smb_ref, kmean_ref,
                   qat_ref, qbt_ref, vat_ref, qit_ref, vbt_ref, wit_ref):
    x = x_ref[...]
    r = lax.rsqrt(jnp.mean(x * x, axis=-1, keepdims=True) + EPS)
    hf = (x * r) * gn_ref[...]
    h = hf.astype(BF16)
    ht = hf.T.astype(BF16)

    gr = lax.broadcasted_iota(jnp.int32, (BLK, BLK), 0) // HEAD_DIM
    gc = lax.broadcasted_iota(jnp.int32, (BLK, BLK), 1) // HEAD_DIM
    group = jnp.where(gr == gc, 1.0, 0.0).astype(BF16)

    def proj(c0, n):
        return _dot(h, wn_ref[:, c0:c0 + n])

    def proj_t(r0, n):
        return _dot(wt_ref[r0:r0 + n, :], ht)

    def headnorm(y, c0):
        n = y.shape[-1]
        sq = y * y
        hi = sq.astype(BF16)
        lo = (sq - hi.astype(F32)).astype(BF16)
        g = group[:n, :n]
        ss = _dot(hi, g) + _dot(lo, g)
        return (y * lax.rsqrt(ss * (1.0 / HEAD_DIM) + EPS)) * ghn_ref[:, c0:c0 + n]

    def headnorm_t(yt, r0):
        ss = jnp.sum(yt * yt, axis=0, keepdims=True)
        return (yt * lax.rsqrt(ss * (1.0 / HEAD_DIM) + EPS)) * ght_ref[r0:r0 + HEAD_DIM, :]

    for c in range(0, WIDTH, BLK):
        kn = headnorm(proj(CN_KA + c, BLK), CN_KA + c)
        ka_ref[:, c:c + BLK] = kn.astype(BF16)
        for t in range(kn.shape[0] // BLK):
            kmean_ref[t, :, c:c + BLK] = jnp.mean(kn[t * BLK:(t + 1) * BLK], axis=0, keepdims=True)
        sga_ref[:, c:c + BLK] = jax.nn.silu(proj(CN_GA + c, BLK))
        sgb_ref[:, c:c + BLK] = jax.nn.silu(proj(CN_GB + c, BLK))
    kn_ref[:, 0:128] = headnorm(proj(CN_KB2, 128), CN_KB2).astype(BF16)
    kn_ref[:, 128:384] = proj(CN_KI4, 256).astype(BF16)
    for c in range(0, D_MODEL, BLK):
        sma_ref[:, c:c + BLK] = jax.nn.sigmoid(proj(CN_MA + c, BLK))
        smb_ref[:, c:c + BLK] = jax.nn.sigmoid(proj(CN_MB + c, BLK))

    def put_t(ref, r0, val):
        for t in range(ROWS_IN // BLK):
            ref[t, r0:r0 + val.shape[0], :] = val[:, t * BLK:(t + 1) * BLK]

    scale = HEAD_DIM ** -0.5
    ones = jnp.ones((ONES_ROWS, ROWS_IN), BF16)
    qat = proj_t(RT_QA, WIDTH)
    qbt = proj_t(RT_QB, WIDTH)
    vat = proj_t(RT_VA, WIDTH)
    for hd in range(HEADS):
        r0 = hd * HEAD_DIM
        put_t(qat_ref, r0, (headnorm_t(qat[r0:r0 + HEAD_DIM], RT_QA + r0) * scale).astype(BF16))
        put_t(qbt_ref, r0, (headnorm_t(qbt[r0:r0 + HEAD_DIM], RT_QB + r0) * scale).astype(BF16))
        put_t(vat_ref, hd * VROWS, vat[r0:r0 + HEAD_DIM].astype(BF16))
        put_t(vat_ref, hd * VROWS + HEAD_DIM, ones)
    put_t(qit_ref, 0, (proj_t(RT_QI, 256) * (IDX_DIM ** -0.5)).astype(BF16))
    tail = proj_t(RT_VB, N_TR - RT_VB)
    put_t(vbt_ref, 0, tail[0:HEAD_DIM].astype(BF16))
    put_t(vbt_ref, HEAD_DIM, ones)
    put_t(wit_ref, 0, tail[RT_WI - RT_VB:RT_WI - RT_VB + WI_ROWS] * (IDX_HEADS ** -0.5))


def _inproj(x2, gn, wn, wt, ghn, ght):
    m = x2.shape[0]
    nt = m // BLK
    tpb = ROWS_IN // BLK
    row = lambda n: pl.BlockSpec((ROWS_IN, n), lambda i: (i, 0))
    full = lambda a: pl.BlockSpec(a.shape, lambda i: (0,) * a.ndim)
    tile = lambda n: pl.BlockSpec((tpb, n, BLK), lambda i: (i, 0, 0))
    out_shapes = [
        jax.ShapeDtypeStruct((m, WIDTH), BF16),
        jax.ShapeDtypeStruct((m, 384), BF16),
        jax.ShapeDtypeStruct((m, WIDTH), F32),
        jax.ShapeDtypeStruct((m, WIDTH), F32),
        jax.ShapeDtypeStruct((m, D_MODEL), F32),
        jax.ShapeDtypeStruct((m, D_MODEL), F32),
        jax.ShapeDtypeStruct((nt, 1, WIDTH), F32),
        jax.ShapeDtypeStruct((nt, WIDTH, BLK), BF16),
        jax.ShapeDtypeStruct((nt, WIDTH, BLK), BF16),
        jax.ShapeDtypeStruct((nt, HEADS * VROWS, BLK), BF16),
        jax.ShapeDtypeStruct((nt, 256, BLK), BF16),
        jax.ShapeDtypeStruct((nt, VROWS, BLK), BF16),
        jax.ShapeDtypeStruct((nt, WI_ROWS, BLK), F32),
    ]
    out_specs = [row(WIDTH), row(384), row(WIDTH), row(WIDTH), row(D_MODEL), row(D_MODEL),
                 pl.BlockSpec((tpb, 1, WIDTH), lambda i: (i, 0, 0)),
                 tile(WIDTH), tile(WIDTH), tile(HEADS * VROWS), tile(256), tile(VROWS), tile(WI_ROWS)]
    return pl.pallas_call(
        _inproj_kernel,
        grid=(nt // tpb,),
        in_specs=[row(D_MODEL), full(gn), full(wn), full(wt), full(ghn), full(ght)],
        out_specs=out_specs,
        out_shape=out_shapes,
        compiler_params=pltpu.CompilerParams(
            dimension_semantics=("arbitrary",), vmem_limit_bytes=VMEM_LIMIT),
        name="inproj",
    )(x2, gn, wn, wt, ghn, ght)


def _build_bias_tiles(rel_ref, bias_ref, head0):
    kpos = lax.broadcasted_iota(jnp.int32, (BLK, BLK), 0)
    qpos = lax.broadcasted_iota(jnp.int32, (BLK, BLK), 1)
    for kind in range(2):
        dist = qpos - kpos + BLK * kind
        for h in range(HEADS):
            t = jnp.full((BLK, BLK), rel_ref[0, head0 + h], F32)
            for k in range(1, REL_BUCKETS):
                t = jnp.where(dist >= _STARTS[k], rel_ref[k, head0 + h], t)
            if kind == 0:
                t = jnp.where(dist >= 0, t, NEG)
            bias_ref[3 * h + kind] = t
    for h in range(HEADS):
        bias_ref[3 * h + 2] = jnp.full((BLK, BLK), rel_ref[REL_BUCKETS - 1, head0 + h], F32)


def _store_masked_queries(qt_ref, qm_ref):
    rowh = lax.broadcasted_iota(jnp.int32, (LANES, BLK), 0)
    for p in range(HEADS // 2):
        q2 = qt_ref[0, LANES * p:LANES * (p + 1), :]
        for hh in range(2):
            qm_ref[2 * p + hh] = jnp.where((rowh >= HEAD_DIM) == bool(hh), q2, jnp.zeros_like(q2))


def _flash_exact(npair, score_fn, pv_fn, s_ref, acc_ref, sel_fn=None):
    def stage(t, ms):
        out = []
        for h in range(HEADS):
            m_new = ms[h]
            for u in range(2):
                j = 2 * t + u
                s = score_fn(j, h)
                s_ref[h, u * BLK:(u + 1) * BLK, :] = s
                cand = jnp.max(s, axis=0, keepdims=True)
                if sel_fn is not None:
                    cand = jnp.where(sel_fn(j, h), cand, M_INIT)
                m_new = jnp.maximum(m_new, cand)
            out.append(m_new)
        return tuple(out)

    def finish(t, ms_old, ms_new):
        for h in range(HEADS):
            acc = jnp.exp(ms_old[h] - ms_new[h]) * acc_ref[h]
            for u in range(2):
                j = 2 * t + u
                p = jnp.exp(s_ref[h, u * BLK:(u + 1) * BLK, :] - ms_new[h])
                pv = pv_fn(j, h, p.astype(BF16))
                acc = acc + (pv if sel_fn is None else jnp.where(sel_fn(j, h), pv, 0.0))
            acc_ref[h] = acc

    def step(t, ms):
        m_new = stage(t, ms)
        finish(t, ms, m_new)
        return m_new

    lax.fori_loop(0, npair, step, tuple(jnp.full((1, BLK), M_INIT, F32) for _ in range(HEADS)))


def _flash_bounded(npair, score_fn, pv_fn, p_ref, acc_ref, sel_fn=None):
    def step(t, c):
        for h in range(HEADS):
            for u in range(2):
                p_ref[2 * h + u] = jnp.exp(score_fn(2 * t + u, h)).astype(BF16)
        for h in range(HEADS):
            acc = acc_ref[h]
            for u in range(2):
                j = 2 * t + u
                pv = pv_fn(j, h, p_ref[2 * h + u])
                acc = acc + (pv if sel_fn is None else jnp.where(sel_fn(j, h), pv, 0.0))
            acc_ref[h] = acc
        return c

    lax.fori_loop(0, npair, step, 0)


def _flash(bounded, npair, score_fn, pv_fn, s_ref, p_ref, acc_ref, sel_fn=None):
    @pl.when(bounded)
    def _():
        _flash_bounded(npair, score_fn, pv_fn, p_ref, acc_ref, sel_fn)

    @pl.when(jnp.logical_not(bounded))
    def _():
        _flash_exact(npair, score_fn, pv_fn, s_ref, acc_ref, sel_fn)


def _logit_bound(q_gain, k_gain, bias):
    return (1.01 * HEAD_DIM ** 0.5 * jnp.max(jnp.abs(q_gain)) * jnp.max(jnp.abs(k_gain))
            + jnp.max(jnp.abs(bias)))


def _write_output(acc_ref, sg_ref, o_ref):
    def head_out(h):
        return acc_ref[h, 0:HEAD_DIM, :] / acc_ref[h, HEAD_DIM:HEAD_DIM + 1, :]

    for p in range(HEADS // 2):
        ot = jnp.concatenate([head_out(2 * p), head_out(2 * p + 1)], axis=0)
        cs = slice(LANES * p, LANES * (p + 1))
        o_ref[0, :, cs] = (ot.T * sg_ref[0, :, cs]).astype(BF16)


def _moba_kernel(rel_ref, bounded_ref, qt_ref, k_ref, vt_ref, kmean_ref, sg_ref, o_ref,
                 bias_ref, qm_ref, acc_ref, s_ref, p_ref):
    b = pl.program_id(0)
    i = pl.program_id(1)
    nb = kmean_ref.shape[1]

    @pl.when((b == 0) & (i == 0))
    def _():
        _build_bias_tiles(rel_ref, bias_ref, 0)

    _store_masked_queries(qt_ref, qm_ref)
    acc_ref[...] = jnp.zeros_like(acc_ref)

    nidx = lax.broadcasted_iota(jnp.int32, (nb, BLK), 0)
    sel_bits = []
    for h in range(HEADS):
        cs = slice(LANES * (h // 2), LANES * (h // 2 + 1))
        gate = _dot(kmean_ref[0, :, cs].astype(BF16), qm_ref[h])
        g = jnp.where(nidx < i, gate, -jnp.inf)
        bits = jnp.left_shift(1, jnp.full((1, BLK), i, jnp.int32))
        for _ in range(MOBA_TOPK):
            mx = jnp.max(g, axis=0, keepdims=True)
            am = jnp.min(jnp.where(g == mx, nidx, nb), axis=0, keepdims=True)
            bits = bits | jnp.where(mx > -jnp.inf, jnp.left_shift(1, am), 0)
            g = jnp.where(nidx == am, -jnp.inf, g)
        sel_bits.append(bits)

    def score(j, h):
        rows = pl.ds(pl.multiple_of(j * BLK, BLK), BLK)
        cs = slice(LANES * (h // 2), LANES * (h // 2 + 1))
        return _dot(k_ref[0, rows, cs], qm_ref[h]) + bias_ref[3 * h + jnp.clip(i - j, 0, 2)]

    def pv(j, h, p):
        return _dot(vt_ref[0, j, h * VROWS:(h + 1) * VROWS, :], p)

    def sel(j, h):
        return (jnp.right_shift(sel_bits[h], j) & 1) == 1

    _flash(bounded_ref[0] == 1, (i + 2) // 2, score, pv, s_ref, p_ref, acc_ref, sel)
    _write_output(acc_ref, sg_ref, o_ref)


def _moba(rel_bias, bounded, qat, ka, vat, kmean, sga):
    bsz, s, _ = ka.shape
    nb = s // BLK
    qspec = pl.BlockSpec((1, BLK, WIDTH), lambda b, i: (b, i, 0))
    return pl.pallas_call(
        _moba_kernel,
        grid=(bsz, nb),
        in_specs=[pl.BlockSpec(memory_space=pltpu.SMEM), pl.BlockSpec(memory_space=pltpu.SMEM),
                  pl.BlockSpec((1, WIDTH, BLK), lambda b, i: (b * nb + i, 0, 0)),
                  pl.BlockSpec((1, s, WIDTH), lambda b, i: (b, 0, 0)),
                  pl.BlockSpec((1, nb, HEADS * VROWS, BLK), lambda b, i: (b, 0, 0, 0)),
                  pl.BlockSpec((1, nb, WIDTH), lambda b, i: (b, 0, 0)), qspec],
        out_specs=qspec,
        out_shape=jax.ShapeDtypeStruct((bsz, s, WIDTH), BF16),
        scratch_shapes=[pltpu.VMEM((3 * HEADS, BLK, BLK), F32),
                        pltpu.VMEM((HEADS, LANES, BLK), BF16),
                        pltpu.VMEM((HEADS, VROWS, BLK), F32),
                        pltpu.VMEM((HEADS, 2 * BLK, BLK), F32),
                        pltpu.VMEM((2 * HEADS, BLK, BLK), BF16)],
        compiler_params=pltpu.CompilerParams(
            dimension_semantics=("arbitrary", "arbitrary"), vmem_limit_bytes=VMEM_LIMIT),
        name="moba",
    )(rel_bias, bounded, qat, ka, vat.reshape(bsz, nb, HEADS * VROWS, BLK), kmean, sga)


def _dsa_kernel(rel_ref, bounded_ref, qt_ref, qit_ref, wit_ref, kn_ref, vt_ref, sg_ref, o_ref,
                bias_ref, qm_ref, acc_ref, s_ref, p_ref, key_ref, khi_ref, klo_ref, madd_ref,
                *, topk):
    b = pl.program_id(0)
    i = pl.program_id(1)
    nblk = i + 1

    @pl.when((b == 0) & (i == 0))
    def _():
        _build_bias_tiles(rel_ref, bias_ref, HEADS)

    _store_masked_queries(qt_ref, qm_ref)
    acc_ref[...] = jnp.zeros_like(acc_ref)

    kpos = lax.broadcasted_iota(jnp.int32, (BLK, BLK), 0)
    qpos = lax.broadcasted_iota(jnp.int32, (BLK, BLK), 1)

    qit = qit_ref[0]
    wit = wit_ref[0]
    qims = [jnp.where((kpos // IDX_DIM) == h, qit, jnp.zeros_like(qit)) for h in range(IDX_HEADS)]
    whs = [wit[h:h + 1, :] for h in range(IDX_HEADS)]

    def score_block(j, causal):
        rows = pl.ds(pl.multiple_of(j * BLK, BLK), BLK)
        ki4 = kn_ref[0, rows, 128:384]
        score = jnp.zeros((BLK, BLK), F32)
        for h in range(IDX_HEADS):
            score = score + jnp.maximum(_dot(ki4, qims[h]), 0.0) * whs[h]
        bits = lax.bitcast_convert_type(score, jnp.int32)
        key = bits ^ (jnp.right_shift(bits, 31) & 0x7FFFFFFF)
        if causal:
            key = jnp.where(kpos <= qpos, key, INT_MIN)
        key_ref[j] = key
        khi_ref[j] = jnp.right_shift(key, 16).astype(jnp.int16)

    pad = jnp.minimum(nblk, key_ref.shape[0] - 1)
    key_ref[pad] = jnp.full((BLK, BLK), INT_MIN, jnp.int32)
    khi_ref[pad] = jnp.full((BLK, BLK), INT16_MIN, jnp.int16)
    klo_ref[pad] = jnp.full((BLK, BLK), INT16_MIN, jnp.int16)

    def score_pair(t, _):
        score_block(2 * t, False)
        score_block(2 * t + 1, False)
        return 0

    def score_single(_, c):
        score_block(i - 1, False)
        return c

    lax.fori_loop(0, i // 2, score_pair, 0)
    lax.fori_loop(0, i % 2, score_single, 0)
    score_block(i, True)
    npair = (nblk + 1) // 2

    def count16(ref, cand, strict=False):
        c16 = jnp.broadcast_to(cand, (PACK16, BLK)).astype(jnp.int16)

        def cb(t, acc):
            for j in (2 * t, 2 * t + 1):
                k3 = ref[j].reshape(BLK // PACK16, PACK16, BLK)
                for g in range(BLK // PACK16):
                    hit = (k3[g] > c16) if strict else (k3[g] >= c16)
                    acc = acc + jnp.where(hit, jnp.int16(1), jnp.int16(0))
            return acc

        acc = lax.fori_loop(0, npair, cb, jnp.zeros((PACK16, BLK), jnp.int16))
        return jnp.sum(acc.astype(jnp.int32), axis=0, keepdims=True)

    def bisect16(ref, want):
        zero = jnp.zeros((1, BLK), jnp.int32)
        t0 = jnp.where(count16(ref, zero) >= want, zero, INT16_MIN)

        def bit_body(t, thr):
            cand = thr + jnp.left_shift(1, 14 - t)
            return jnp.where(count16(ref, cand) >= want, cand, thr)

        return lax.fori_loop(0, 15, bit_body, t0)

    thr_hi = bisect16(khi_ref, topk)
    above_hi = count16(khi_ref, thr_hi, strict=True)

    def low_body(j, _):
        key = key_ref[j]
        low = (key & 0xFFFF) - 0x8000
        member = jnp.right_shift(key, 16) == thr_hi
        klo_ref[j] = jnp.where(member, low, INT16_MIN).astype(jnp.int16)
        return 0

    lax.fori_loop(0, nblk, low_body, 0)
    thr_lo = bisect16(klo_ref, topk - above_hi)
    thr = thr_hi * 65536 + (thr_lo + 0x8000)
    above = above_hi + count16(klo_ref, thr_lo, strict=True)

    need = jnp.where(thr > INT_MIN, topk - above, 0).astype(F32)
    lower = jnp.where(qpos <= kpos, 1.0, 0.0).astype(BF16)

    def mask_body(t, carry):
        for j in (2 * t, 2 * t + 1):
            k = key_ref[j]
            eq = k == thr
            pref = _dot(lower, jnp.where(eq, 1.0, 0.0).astype(BF16))
            sel = (k > thr) | (eq & ((carry + pref) <= need))
            madd_ref[j] = jnp.where(sel, 0.0, NEG)
            carry = carry + pref[BLK - 1:BLK, :]
        return carry

    lax.fori_loop(0, npair, mask_body, jnp.zeros((1, BLK), F32))

    def score(j, h):
        rows = pl.ds(pl.multiple_of(j * BLK, BLK), BLK)
        return (_dot(kn_ref[0, rows, 0:128], qm_ref[h])
                + bias_ref[3 * h + jnp.clip(i - j, 0, 2)] + madd_ref[j])

    _flash(bounded_ref[0] == 1, npair, score, lambda j, h, p: _dot(vt_ref[0, j], p),
           s_ref, p_ref, acc_ref)
    _write_output(acc_ref, sg_ref, o_ref)


def _dsa(rel_bias, bounded, qbt, qit, wit, kn, vbt, sgb):
    bsz, s, _ = kn.shape
    nb = s // BLK
    topk = min(DSA_TOPK_MAX, s // 4)
    qspec = pl.BlockSpec((1, BLK, WIDTH), lambda b, i: (b, i, 0))
    tile = lambda n: pl.BlockSpec((1, n, BLK), lambda b, i: (b * nb + i, 0, 0))
    return pl.pallas_call(
        functools.partial(_dsa_kernel, topk=topk),
        grid=(bsz, nb),
        in_specs=[pl.BlockSpec(memory_space=pltpu.SMEM), pl.BlockSpec(memory_space=pltpu.SMEM),
                  tile(WIDTH), tile(256), tile(WI_ROWS),
                  pl.BlockSpec((1, s, 384), lambda b, i: (b, 0, 0)),
                  pl.BlockSpec((1, nb, VROWS, BLK), lambda b, i: (b, 0, 0, 0)), qspec],
        out_specs=qspec,
        out_shape=jax.ShapeDtypeStruct((bsz, s, WIDTH), BF16),
        scratch_shapes=[pltpu.VMEM((3 * HEADS, BLK, BLK), F32),
                        pltpu.VMEM((HEADS, LANES, BLK), BF16),
                        pltpu.VMEM((HEADS, VROWS, BLK), F32),
                        pltpu.VMEM((HEADS, 2 * BLK, BLK), F32),
                        pltpu.VMEM((2 * HEADS, BLK, BLK), BF16),
                        pltpu.VMEM((nb, BLK, BLK), jnp.int32),
                        pltpu.VMEM((nb, BLK, BLK), jnp.int16),
                        pltpu.VMEM((nb, BLK, BLK), jnp.int16),
                        pltpu.VMEM((nb, BLK, BLK), F32)],
        compiler_params=pltpu.CompilerParams(
            dimension_semantics=("arbitrary", "arbitrary"), vmem_limit_bytes=VMEM_LIMIT),
        name="dsa",
    )(rel_bias, bounded, qbt, qit, wit, kn, vbt.reshape(bsz, nb, VROWS, BLK), sgb)


def _outproj_kernel(x_ref, ya_ref, yb_ref, sma_ref, smb_ref, wa_ref, wb_ref, wo_ref, o_ref):
    merged = (sma_ref[...] * _dot(ya_ref[...], wa_ref[...])
              + smb_ref[...] * _dot(yb_ref[...], wb_ref[...]))
    o_ref[...] = x_ref[...] + _dot(merged.astype(BF16), wo_ref[...])


def _outproj(x2, ya, yb, sma, smb, wa, wb, wo):
    m = x2.shape[0]
    row = lambda n: pl.BlockSpec((ROWS_OUT, n), lambda i: (i, 0))
    full = lambda a: pl.BlockSpec(a.shape, lambda i: (0,) * a.ndim)
    return pl.pallas_call(
        _outproj_kernel,
        grid=(m // ROWS_OUT,),
        in_specs=[row(D_MODEL), row(WIDTH), row(WIDTH), row(D_MODEL), row(D_MODEL),
                  full(wa), full(wb), full(wo)],
        out_specs=row(D_MODEL),
        out_shape=jax.ShapeDtypeStruct((m, D_MODEL), F32),
        compiler_params=pltpu.CompilerParams(
            dimension_semantics=("arbitrary",), vmem_limit_bytes=VMEM_LIMIT),
        name="outproj",
    )(x2, ya, yb, sma, smb, wa, wb, wo)


def _pack_w_in(w):
    o = 0
    parts = {}
    for name, n in (("qa", WIDTH), ("ka", WIDTH), ("va", WIDTH), ("ga", WIDTH), ("qb", WIDTH),
                    ("kb", HEAD_DIM), ("vb", HEAD_DIM), ("gb", WIDTH), ("qi", IDX_HEADS * IDX_DIM),
                    ("ki", IDX_DIM), ("wi", IDX_HEADS), ("ma", D_MODEL), ("mb", D_MODEL)):
        parts[name] = w[:, o:o + n]
        o += n
    wn = jnp.concatenate([parts["ka"], parts["kb"], parts["kb"],
                          parts["ki"], parts["ki"], parts["ki"], parts["ki"],
                          parts["ga"], parts["gb"], parts["ma"], parts["mb"]], axis=1).astype(BF16)
    pad = jnp.zeros((w.shape[0], N_TR - RT_WI - IDX_HEADS), w.dtype)
    wt = jnp.concatenate([parts["qa"], parts["qb"], parts["va"], parts["qi"], parts["vb"],
                          parts["wi"], pad], axis=1).T.astype(BF16)
    assert wn.shape[1] == N_NAT and wt.shape[0] == N_TR
    return wn, wt


def _layer(x2, bsz, s, gn, w_in, qn_a, kn_a, qn_b, kn_b, w_br_a, w_br_b, w_out, rel_bias):
    wn, wt = _pack_w_in(w_in)
    ghn = jnp.concatenate([jnp.tile(kn_a, HEADS), jnp.tile(kn_b, 2)])[None, :]
    ght = jnp.concatenate([jnp.tile(qn_a, HEADS), jnp.tile(qn_b, HEADS)])[:, None]
    (ka, kn, sga, sgb, sma, smb, kmean, qat, qbt, vat, qit, vbt, wit) = _inproj(
        x2, gn[None, :], wn, wt, ghn, ght)
    r3 = lambda a: a.reshape(bsz, s, a.shape[-1])
    flag = lambda bound: (bound <= MAX_PLAIN_LOGIT).astype(jnp.int32).reshape(1)
    ya = _moba(rel_bias, flag(_logit_bound(qn_a, kn_a, rel_bias[:, :HEADS])),
               qat, r3(ka), vat, kmean.reshape(bsz, s // BLK, WIDTH), r3(sga))
    yb = _dsa(rel_bias, flag(_logit_bound(qn_b, kn_b, rel_bias[:, HEADS:])),
              qbt, qit, wit, r3(kn), vbt, r3(sgb))
    return _outproj(x2, ya.reshape(-1, WIDTH), yb.reshape(-1, WIDTH), sma, smb,
                    w_br_a.astype(BF16), w_br_b.astype(BF16), w_out.astype(BF16))


def kernel(x, norm_g, w_in, q_norm_a, k_norm_a, q_norm_b, k_norm_b,
           w_branch_a, w_branch_b, w_out, rel_bias):
    bsz, s, d = x.shape
    assert d == D_MODEL and s % BLK == 0 and s // BLK <= 31
    h = x.reshape(bsz * s, d)
    for l in range(norm_g.shape[0]):
        h = _layer(h, bsz, s, norm_g[l], w_in[l], q_norm_a[l], k_norm_a[l], q_norm_b[l],
                   k_norm_b[l], w_branch_a[l], w_branch_b[l], w_out[l], rel_bias)
    return h.reshape(bsz, s, d)
```

```python
import functools
import math

import numpy as np
import jax
import jax.numpy as jnp
from jax import lax
from jax.experimental import pallas as pl
from jax.experimental.pallas import tpu as pltpu

D_MODEL = 1024
HEADS = 8
HEAD_DIM = 64
WIDTH = HEADS * HEAD_DIM
BLK = 256
MOBA_TOPK = 3
IDX_HEADS = 4
IDX_DIM = 64
DSA_TOPK_MAX = 256
REL_BUCKETS = 32
REL_MAX_DIST = 128
EPS = 1e-6
NEG = -1e30
M_INIT = -1e20
MAX_PLAIN_LOGIT = 40.0
INT_MIN = -2 ** 31
KEY_BITS = 32
LANES = 128
VMEM_LIMIT = 56 * 1024 * 1024

F32 = jnp.float32
BF16 = jnp.bfloat16

CN_KA, CN_KB2, CN_KI4, CN_GA, CN_GB, CN_MA, CN_MB = 0, 512, 640, 896, 1408, 1920, 2944
N_NAT = 3968
RT_QA, RT_QB, RT_VA, RT_QI, RT_VB, RT_WI = 0, 512, 1024, 1536, 1792, 1856
N_TR = 1872
WI_ROWS = 8
ROWS_IN = 512
ROWS_OUT = 512
ONES_ROWS = 16
VROWS = HEAD_DIM + ONES_ROWS


def _bucket_starts():
    n = np.arange(0, 2 * REL_MAX_DIST + 2)
    max_exact = REL_BUCKETS // 2
    nf = np.maximum(n, 1).astype(np.float64)
    v = np.log(nf / max_exact) / math.log(REL_MAX_DIST / max_exact) * (REL_BUCKETS - max_exact)
    large = np.minimum(max_exact + np.trunc(v).astype(np.int64), REL_BUCKETS - 1)
    bucket = np.where(n < max_exact, n, large)
    frac = np.abs(v - np.round(v))[(n > max_exact) & (n < REL_MAX_DIST)]
    assert frac.min() > 1e-3
    return [int(np.argmax(bucket >= k)) for k in range(REL_BUCKETS)]


_STARTS = _bucket_starts()


def _dot(a, b):
    return jnp.dot(a, b, preferred_element_type=F32)


def _inproj_kernel(x_ref, gn_ref, wn_ref, wt_ref, ghn_ref, ght_ref,
                   ka_ref, kn_ref, sga_ref, sgb_ref, sma_ref, smb_ref, kmean_ref,
                   qat_ref, qbt_ref, vat_ref, qit_ref, vbt_ref, wit_ref):
    x = x_ref[...]
    r = lax.rsqrt(jnp.mean(x * x, axis=-1, keepdims=True) + EPS)
    hf = (x * r) * gn_ref[...]
    h = hf.astype(BF16)
    ht = hf.T.astype(BF16)

    gr = lax.broadcasted_iota(jnp.int32, (BLK, BLK), 0) // HEAD_DIM
    gc = lax.broadcasted_iota(jnp.int32, (BLK, BLK), 1) // HEAD_DIM
    group = jnp.where(gr == gc, 1.0, 0.0).astype(BF16)

    def proj(c0, n):
        return _dot(h, wn_ref[:, c0:c0 + n])

    def proj_t(r0, n):
        return _dot(wt_ref[r0:r0 + n, :], ht)

    def headnorm(y, c0):
        n = y.shape[-1]
        sq = y * y
        hi = sq.astype(BF16)
        lo = (sq - hi.astype(F32)).astype(BF16)
        g = group[:n, :n]
        ss = _dot(hi, g) + _dot(lo, g)
        return (y * lax.rsqrt(ss * (1.0 / HEAD_DIM) + EPS)) * ghn_ref[:, c0:c0 + n]

    def headnorm_t(yt, r0):
        ss = jnp.sum(yt * yt, axis=0, keepdims=True)
        return (yt * lax.rsqrt(ss * (1.0 / HEAD_DIM) + EPS)) * ght_ref[r0:r0 + HEAD_DIM, :]

    for c in range(0, WIDTH, BLK):
        kn = headnorm(proj(CN_KA + c, BLK), CN_KA + c)
        ka_ref[:, c:c + BLK] = kn.astype(BF16)
        for t in range(kn.shape[0] // BLK):
            kmean_ref[t, :, c:c + BLK] = jnp.mean(kn[t * BLK:(t + 1) * BLK], axis=0, keepdims=True)
        sga_ref[:, c:c + BLK] = jax.nn.silu(proj(CN_GA + c, BLK))
        sgb_ref[:, c:c + BLK] = jax.nn.silu(proj(CN_GB + c, BLK))
    kn_ref[:, 0:128] = headnorm(proj(CN_KB2, 128), CN_KB2).astype(BF16)
    kn_ref[:, 128:384] = proj(CN_KI4, 256).astype(BF16)
    for c in range(0, D_MODEL, BLK):
        sma_ref[:, c:c + BLK] = jax.nn.sigmoid(proj(CN_MA + c, BLK))
        smb_ref[:, c:c + BLK] = jax.nn.sigmoid(proj(CN_MB + c, BLK))

    def put_t(ref, r0, val):
        for t in range(ROWS_IN // BLK):
            ref[t, r0:r0 + val.shape[0], :] = val[:, t * BLK:(t + 1) * BLK]

    scale = HEAD_DIM ** -0.5
    ones = jnp.ones((ONES_ROWS, ROWS_IN), BF16)
    qat = proj_t(RT_QA, WIDTH)
    qbt = proj_t(RT_QB, WIDTH)
    vat = proj_t(RT_VA, WIDTH)
    for hd in range(HEADS):
        r0 = hd * HEAD_DIM
        put_t(qat_ref, r0, (headnorm_t(qat[r0:r0 + HEAD_DIM], RT_QA + r0) * scale).astype(BF16))
        put_t(qbt_ref, r0, (headnorm_t(qbt[r0:r0 + HEAD_DIM], RT_QB + r0) * scale).astype(BF16))
        put_t(vat_ref, hd * VROWS, vat[r0:r0 + HEAD_DIM].astype(BF16))
        put_t(vat_ref, hd * VROWS + HEAD_DIM, ones)
    put_t(qit_ref, 0, (proj_t(RT_QI, 256) * (IDX_DIM ** -0.5)).astype(BF16))
    tail = proj_t(RT_VB, N_TR - RT_VB)
    put_t(vbt_ref, 0, tail[0:HEAD_DIM].astype(BF16))
    put_t(vbt_ref, HEAD_DIM, ones)
    put_t(wit_ref, 0, tail[RT_WI - RT_VB:RT_WI - RT_VB + WI_ROWS] * (IDX_HEADS ** -0.5))


def _inproj(x2, gn, wn, wt, ghn, ght):
    m = x2.shape[0]
    nt = m // BLK
    tpb = ROWS_IN // BLK
    row = lambda n: pl.BlockSpec((ROWS_IN, n), lambda i: (i, 0))
    full = lambda a: pl.BlockSpec(a.shape, lambda i: (0,) * a.ndim)
    tile = lambda n: pl.BlockSpec((tpb, n, BLK), lambda i: (i, 0, 0))
    out_shapes = [
        jax.ShapeDtypeStruct((m, WIDTH), BF16),
        jax.ShapeDtypeStruct((m, 384), BF16),
        jax.ShapeDtypeStruct((m, WIDTH), F32),
        jax.ShapeDtypeStruct((m, WIDTH), F32),
        jax.ShapeDtypeStruct((m, D_MODEL), F32),
        jax.ShapeDtypeStruct((m, D_MODEL), F32),
        jax.ShapeDtypeStruct((nt, 1, WIDTH), F32),
        jax.ShapeDtypeStruct((nt, WIDTH, BLK), BF16),
        jax.ShapeDtypeStruct((nt, WIDTH, BLK), BF16),
        jax.ShapeDtypeStruct((nt, HEADS * VROWS, BLK), BF16),
        jax.ShapeDtypeStruct((nt, 256, BLK), BF16),
        jax.ShapeDtypeStruct((nt, VROWS, BLK), BF16),
        jax.ShapeDtypeStruct((nt, WI_ROWS, BLK), F32),
    ]
    out_specs = [row(WIDTH), row(384), row(WIDTH), row(WIDTH), row(D_MODEL), row(D_MODEL),
                 pl.BlockSpec((tpb, 1, WIDTH), lambda i: (i, 0, 0)),
                 tile(WIDTH), tile(WIDTH), tile(HEADS * VROWS), tile(256), tile(VROWS), tile(WI_ROWS)]
    return pl.pallas_call(
        _inproj_kernel,
        grid=(nt // tpb,),
        in_specs=[row(D_MODEL), full(gn), full(wn), full(wt), full(ghn), full(ght)],
        out_specs=out_specs,
        out_shape=out_shapes,
        compiler_params=pltpu.CompilerParams(
            dimension_semantics=("arbitrary",), vmem_limit_bytes=VMEM_LIMIT),
        name="inproj",
    )(x2, gn, wn, wt, ghn, ght)


def _build_bias_tiles(rel_ref, bias_ref, head0):
    kpos = lax.broadcasted_iota(jnp.int32, (BLK, BLK), 0)
    qpos = lax.broadcasted_iota(jnp.int32, (BLK, BLK), 1)
    for kind in range(2):
        dist = qpos - kpos + BLK * kind
        for h in range(HEADS):
            t = jnp.full((BLK, BLK), rel_ref[0, head0 + h], F32)
            for k in range(1, REL_BUCKETS):
                t = jnp.where(dist >= _STARTS[k], rel_ref[k, head0 + h], t)
            if kind == 0:
                t = jnp.where(dist >= 0, t, NEG)
            bias_ref[3 * h + kind] = t
    for h in range(HEADS):
        bias_ref[3 * h + 2] = jnp.full((BLK, BLK), rel_ref[REL_BUCKETS - 1, head0 + h], F32)


def _store_masked_queries(qt_ref, qm_ref):
    rowh = lax.broadcasted_iota(jnp.int32, (LANES, BLK), 0)
    for p in range(HEADS // 2):
        q2 = qt_ref[0, LANES * p:LANES * (p + 1), :]
        for hh in range(2):
            qm_ref[2 * p + hh] = jnp.where((rowh >= HEAD_DIM) == bool(hh), q2, jnp.zeros_like(q2))


def _flash_exact(npair, score_fn, pv_fn, s_ref, acc_ref, sel_fn=None):
    def stage(t, ms):
        out = []
        for h in range(HEADS):
            m_new = ms[h]
            for u in range(2):
                j = 2 * t + u
                s = score_fn(j, h)
                s_ref[h, u * BLK:(u + 1) * BLK, :] = s
                cand = jnp.max(s, axis=0, keepdims=True)
                if sel_fn is not None:
                    cand = jnp.where(sel_fn(j, h), cand, M_INIT)
                m_new = jnp.maximum(m_new, cand)
            out.append(m_new)
        return tuple(out)

    def finish(t, ms_old, ms_new):
        for h in range(HEADS):
            acc = jnp.exp(ms_old[h] - ms_new[h]) * acc_ref[h]
            for u in range(2):
                j = 2 * t + u
                p = jnp.exp(s_ref[h, u * BLK:(u + 1) * BLK, :] - ms_new[h])
                pv = pv_fn(j, h, p.astype(BF16))
                acc = acc + (pv if sel_fn is None else jnp.where(sel_fn(j, h), pv, 0.0))
            acc_ref[h] = acc

    def step(t, ms):
        m_new = stage(t, ms)
        finish(t, ms, m_new)
        return m_new

    lax.fori_loop(0, npair, step, tuple(jnp.full((1, BLK), M_INIT, F32) for _ in range(HEADS)))


def _flash_bounded(npair, score_fn, pv_fn, p_ref, acc_ref, sel_fn=None):
    def step(t, c):
        for h in range(HEADS):
            for u in range(2):
                p_ref[2 * h + u] = jnp.exp(score_fn(2 * t + u, h)).astype(BF16)
        for h in range(HEADS):
            acc = acc_ref[h]
            for u in range(2):
                j = 2 * t + u
                pv = pv_fn(j, h, p_ref[2 * h + u])
                acc = acc + (pv if sel_fn is None else jnp.where(sel_fn(j, h), pv, 0.0))
            acc_ref[h] = acc
        return c

    lax.fori_loop(0, npair, step, 0)


def _flash(bounded, npair, score_fn, pv_fn, s_ref, p_ref, acc_ref, sel_fn=None):
    @pl.when(bounded)
    def _():
        _flash_bounded(npair, score_fn, pv_fn, p_ref, acc_ref, sel_fn)

    @pl.when(jnp.logical_not(bounded))
    def _():
        _flash_exact(npair, score_fn, pv_fn, s_ref, acc_ref, sel_fn)


def _logit_bound(q_gain, k_gain, bias):
    return (1.01 * HEAD_DIM ** 0.5 * jnp.max(jnp.abs(q_gain)) * jnp.max(jnp.abs(k_gain))
            + jnp.max(jnp.abs(bias)))


def _write_output(acc_ref, sg_ref, o_ref):
    def head_out(h):
        return acc_ref[h, 0:HEAD_DIM, :] / acc_ref[h, HEAD_DIM:HEAD_DIM + 1, :]

    for p in range(HEADS // 2):
        ot = jnp.concatenate([head_out(2 * p), head_out(2 * p + 1)], axis=0)
        cs = slice(LANES * p, LANES * (p + 1))
        o_ref[0, :, cs] = (ot.T * sg_ref[0, :, cs]).astype(BF16)


def _moba_kernel(rel_ref, bounded_ref, qt_ref, k_ref, vt_ref, kmean_ref, sg_ref, o_ref,
                 bias_ref, qm_ref, acc_ref, s_ref, p_ref):
    b = pl.program_id(0)
    i = pl.program_id(1)
    nb = kmean_ref.shape[1]

    @pl.when((b == 0) & (i == 0))
    def _():
        _build_bias_tiles(rel_ref, bias_ref, 0)

    _store_masked_queries(qt_ref, qm_ref)
    acc_ref[...] = jnp.zeros_like(acc_ref)

    nidx = lax.broadcasted_iota(jnp.int32, (nb, BLK), 0)
    sel_bits = []
    for h in range(HEADS):
        cs = slice(LANES * (h // 2), LANES * (h // 2 + 1))
        gate = _dot(kmean_ref[0, :, cs].astype(BF16), qm_ref[h])
        g = jnp.where(nidx < i, gate, -jnp.inf)
        bits = jnp.left_shift(1, jnp.full((1, BLK), i, jnp.int32))
        for _ in range(MOBA_TOPK):
            mx = jnp.max(g, axis=0, keepdims=True)
            am = jnp.min(jnp.where(g == mx, nidx, nb), axis=0, keepdims=True)
            bits = bits | jnp.where(mx > -jnp.inf, jnp.left_shift(1, am), 0)
            g = jnp.where(nidx == am, -jnp.inf, g)
        sel_bits.append(bits)

    def score(j, h):
        rows = pl.ds(pl.multiple_of(j * BLK, BLK), BLK)
        cs = slice(LANES * (h // 2), LANES * (h // 2 + 1))
        return _dot(k_ref[0, rows, cs], qm_ref[h]) + bias_ref[3 * h + jnp.clip(i - j, 0, 2)]

    def pv(j, h, p):
        return _dot(vt_ref[0, j, h * VROWS:(h + 1) * VROWS, :], p)

    def sel(j, h):
        return (jnp.right_shift(sel_bits[h], j) & 1) == 1

    _flash(bounded_ref[0] == 1, (i + 2) // 2, score, pv, s_ref, p_ref, acc_ref, sel)
    _write_output(acc_ref, sg_ref, o_ref)


def _moba(rel_bias, bounded, qat, ka, vat, kmean, sga):
    bsz, s, _ = ka.shape
    nb = s // BLK
    qspec = pl.BlockSpec((1, BLK, WIDTH), lambda b, i: (b, i, 0))
    return pl.pallas_call(
        _moba_kernel,
        grid=(bsz, nb),
        in_specs=[pl.BlockSpec(memory_space=pltpu.SMEM), pl.BlockSpec(memory_space=pltpu.SMEM),
                  pl.BlockSpec((1, WIDTH, BLK), lambda b, i: (b * nb + i, 0, 0)),
                  pl.BlockSpec((1, s, WIDTH), lambda b, i: (b, 0, 0)),
                  pl.BlockSpec((1, nb, HEADS * VROWS, BLK), lambda b, i: (b, 0, 0, 0)),
                  pl.BlockSpec((1, nb, WIDTH), lambda b, i: (b, 0, 0)), qspec],
        out_specs=qspec,
        out_shape=jax.ShapeDtypeStruct((bsz, s, WIDTH), BF16),
        scratch_shapes=[pltpu.VMEM((3 * HEADS, BLK, BLK), F32),
                        pltpu.VMEM((HEADS, LANES, BLK), BF16),
                        pltpu.VMEM((HEADS, VROWS, BLK), F32),
                        pltpu.VMEM((HEADS, 2 * BLK, BLK), F32),
                        pltpu.VMEM((2 * HEADS, BLK, BLK), BF16)],
        compiler_params=pltpu.CompilerParams(
            dimension_semantics=("arbitrary", "arbitrary"), vmem_limit_bytes=VMEM_LIMIT),
        name="moba",
    )(rel_bias, bounded, qat, ka, vat.reshape(bsz, nb, HEADS * VROWS, BLK), kmean, sga)


def _bit_planes(u):
    x = [u[8 * a:8 * a + 8, :] for a in range(KEY_BITS)]
    j, m = 16, 0x0000FFFF
    while j:
        for k in range(KEY_BITS):
            if not k & j:
                t = (x[k] ^ jnp.right_shift(x[k + j], j)) & m
                x[k] = x[k] ^ t
                x[k + j] = x[k + j] ^ jnp.left_shift(t, j)
        j >>= 1
        m ^= (m << j) & 0xFFFFFFFF
    return x


def _dsa_kernel(rel_ref, bounded_ref, qt_ref, qit_ref, wit_ref, kn_ref, vt_ref, sg_ref, o_ref,
                bias_ref, qm_ref, acc_ref, s_ref, p_ref, key_ref, plane_ref, alive_ref, madd_ref,
                *, topk):
    b = pl.program_id(0)
    i = pl.program_id(1)
    nblk = i + 1
    nb = key_ref.shape[0]

    @pl.when((b == 0) & (i == 0))
    def _():
        _build_bias_tiles(rel_ref, bias_ref, HEADS)
        plane_ref[...] = jnp.zeros_like(plane_ref)

    _store_masked_queries(qt_ref, qm_ref)
    acc_ref[...] = jnp.zeros_like(acc_ref)

    kpos = lax.broadcasted_iota(jnp.int32, (BLK, BLK), 0)
    qpos = lax.broadcasted_iota(jnp.int32, (BLK, BLK), 1)

    qit = qit_ref[0]
    wit = wit_ref[0]
    qims = [jnp.where((kpos // IDX_DIM) == h, qit, jnp.zeros_like(qit)) for h in range(IDX_HEADS)]
    whs = [wit[h:h + 1, :] for h in range(IDX_HEADS)]

    def score_block(j, causal):
        rows = pl.ds(pl.multiple_of(j * BLK, BLK), BLK)
        ki4 = kn_ref[0, rows, 128:384]
        score = jnp.zeros((BLK, BLK), F32)
        for h in range(IDX_HEADS):
            score = score + jnp.maximum(_dot(ki4, qims[h]), 0.0) * whs[h]
        bits = lax.bitcast_convert_type(score, jnp.int32)
        key = bits ^ (jnp.right_shift(bits, 31) & 0x7FFFFFFF)
        if causal:
            key = jnp.where(kpos <= qpos, key, INT_MIN)
        key_ref[j] = key
        planes = _bit_planes(key ^ INT_MIN)
        for r in range(KEY_BITS):
            plane_ref[j, r] = planes[r]

    pad = jnp.minimum(nblk, nb - 1)
    key_ref[pad] = jnp.full((BLK, BLK), INT_MIN, jnp.int32)

    def score_pair(t, _):
        score_block(2 * t, False)
        score_block(2 * t + 1, False)
        return 0

    def score_single(_, c):
        score_block(i - 1, False)
        return c

    lax.fori_loop(0, i // 2, score_pair, 0)
    lax.fori_loop(0, i % 2, score_single, 0)
    score_block(i, True)
    npair = (nblk + 1) // 2

    for j in range(nb):
        alive_ref[j] = jnp.where(j < nblk, jnp.full((8, BLK), -1, jnp.int32), 0)

    def radix_round(r, carry):
        above, thr_u = carry
        ones_here = jnp.zeros((8, BLK), jnp.int32)
        for j in range(nb):
            ones_here = ones_here + lax.population_count(alive_ref[j] & plane_ref[j, r])
        ones_here = jnp.sum(ones_here, axis=0, keepdims=True)
        take = (above + ones_here) >= topk
        flip = jnp.where(take, 0, -1)
        for j in range(nb):
            alive_ref[j] = alive_ref[j] & (plane_ref[j, r] ^ flip)
        return (jnp.where(take, above, above + ones_here),
                thr_u | jnp.where(take, jnp.left_shift(1, KEY_BITS - 1 - r), 0))

    zero = jnp.zeros((1, BLK), jnp.int32)
    above, thr_u = lax.fori_loop(0, KEY_BITS, radix_round, (zero, zero))
    thr = thr_u ^ INT_MIN

    need = jnp.where(thr > INT_MIN, topk - above, 0).astype(F32)
    lower = jnp.where(qpos <= kpos, 1.0, 0.0).astype(BF16)

    def mask_body(t, carry):
        for j in (2 * t, 2 * t + 1):
            k = key_ref[j]
            eq = k == thr
            pref = _dot(lower, jnp.where(eq, 1.0, 0.0).astype(BF16))
            sel = (k > thr) | (eq & ((carry + pref) <= need))
            madd_ref[j] = jnp.where(sel, 0.0, NEG)
            carry = carry + pref[BLK - 1:BLK, :]
        return carry

    lax.fori_loop(0, npair, mask_body, jnp.zeros((1, BLK), F32))

    def score(j, h):
        rows = pl.ds(pl.multiple_of(j * BLK, BLK), BLK)
        return (_dot(kn_ref[0, rows, 0:128], qm_ref[h])
                + bias_ref[3 * h + jnp.clip(i - j, 0, 2)] + madd_ref[j])

    _flash(bounded_ref[0] == 1, npair, score, lambda j, h, p: _dot(vt_ref[0, j], p),
           s_ref, p_ref, acc_ref)
    _write_output(acc_ref, sg_ref, o_ref)


def _dsa(rel_bias, bounded, qbt, qit, wit, kn, vbt, sgb):
    bsz, s, _ = kn.shape
    nb = s // BLK
    topk = min(DSA_TOPK_MAX, s // 4)
    qspec = pl.BlockSpec((1, BLK, WIDTH), lambda b, i: (b, i, 0))
    tile = lambda n: pl.BlockSpec((1, n, BLK), lambda b, i: (b * nb + i, 0, 0))
    return pl.pallas_call(
        functools.partial(_dsa_kernel, topk=topk),
        grid=(bsz, nb),
        in_specs=[pl.BlockSpec(memory_space=pltpu.SMEM), pl.BlockSpec(memory_space=pltpu.SMEM),
                  tile(WIDTH), tile(256), tile(WI_ROWS),
                  pl.BlockSpec((1, s, 384), lambda b, i: (b, 0, 0)),
                  pl.BlockSpec((1, nb, VROWS, BLK), lambda b, i: (b, 0, 0, 0)), qspec],
        out_specs=qspec,
        out_shape=jax.ShapeDtypeStruct((bsz, s, WIDTH), BF16),
        scratch_shapes=[pltpu.VMEM((3 * HEADS, BLK, BLK), F32),
                        pltpu.VMEM((HEADS, LANES, BLK), BF16),
                        pltpu.VMEM((HEADS, VROWS, BLK), F32),
                        pltpu.VMEM((HEADS, 2 * BLK, BLK), F32),
                        pltpu.VMEM((2 * HEADS, BLK, BLK), BF16),
                        pltpu.VMEM((nb, BLK, BLK), jnp.int32),
                        pltpu.VMEM((nb, KEY_BITS, 8, BLK), jnp.int32),
                        pltpu.VMEM((nb, 8, BLK), jnp.int32),
                        pltpu.VMEM((nb, BLK, BLK), F32)],
        compiler_params=pltpu.CompilerParams(
            dimension_semantics=("arbitrary", "arbitrary"), vmem_limit_bytes=VMEM_LIMIT),
        name="dsa",
    )(rel_bias, bounded, qbt, qit, wit, kn, vbt.reshape(bsz, nb, VROWS, BLK), sgb)


def _outproj_kernel(x_ref, ya_ref, yb_ref, sma_ref, smb_ref, wa_ref, wb_ref, wo_ref, o_ref):
    merged = (sma_ref[...] * _dot(ya_ref[...], wa_ref[...])
              + smb_ref[...] * _dot(yb_ref[...], wb_ref[...]))
    o_ref[...] = x_ref[...] + _dot(merged.astype(BF16), wo_ref[...])


def _outproj(x2, ya, yb, sma, smb, wa, wb, wo):
    m = x2.shape[0]
    row = lambda n: pl.BlockSpec((ROWS_OUT, n), lambda i: (i, 0))
    full = lambda a: pl.BlockSpec(a.shape, lambda i: (0,) * a.ndim)
    return pl.pallas_call(
        _outproj_kernel,
        grid=(m // ROWS_OUT,),
        in_specs=[row(D_MODEL), row(WIDTH), row(WIDTH), row(D_MODEL), row(D_MODEL),
                  full(wa), full(wb), full(wo)],
        out_specs=row(D_MODEL),
        out_shape=jax.ShapeDtypeStruct((m, D_MODEL), F32),
        compiler_params=pltpu.CompilerParams(
            dimension_semantics=("arbitrary",), vmem_limit_bytes=VMEM_LIMIT),
        name="outproj",
    )(x2, ya, yb, sma, smb, wa, wb, wo)


def _pack_w_in(w):
    o = 0
    parts = {}
    for name, n in (("qa", WIDTH), ("ka", WIDTH), ("va", WIDTH), ("ga", WIDTH), ("qb", WIDTH),
                    ("kb", HEAD_DIM), ("vb", HEAD_DIM), ("gb", WIDTH), ("qi", IDX_HEADS * IDX_DIM),
                    ("ki", IDX_DIM), ("wi", IDX_HEADS), ("ma", D_MODEL), ("mb", D_MODEL)):
        parts[name] = w[:, o:o + n]
        o += n
    wn = jnp.concatenate([parts["ka"], parts["kb"], parts["kb"],
                          parts["ki"], parts["ki"], parts["ki"], parts["ki"],
                          parts["ga"], parts["gb"], parts["ma"], parts["mb"]], axis=1).astype(BF16)
    pad = jnp.zeros((w.shape[0], N_TR - RT_WI - IDX_HEADS), w.dtype)
    wt = jnp.concatenate([parts["qa"], parts["qb"], parts["va"], parts["qi"], parts["vb"],
                          parts["wi"], pad], axis=1).T.astype(BF16)
    assert wn.shape[1] == N_NAT and wt.shape[0] == N_TR
    return wn, wt


def _layer(x2, bsz, s, gn, w_in, qn_a, kn_a, qn_b, kn_b, w_br_a, w_br_b, w_out, rel_bias):
    wn, wt = _pack_w_in(w_in)
    ghn = jnp.concatenate([jnp.tile(kn_a, HEADS), jnp.tile(kn_b, 2)])[None, :]
    ght = jnp.concatenate([jnp.tile(qn_a, HEADS), jnp.tile(qn_b, HEADS)])[:, None]
    (ka, kn, sga, sgb, sma, smb, kmean, qat, qbt, vat, qit, vbt, wit) = _inproj(
        x2, gn[None, :], wn, wt, ghn, ght)
    r3 = lambda a: a.reshape(bsz, s, a.shape[-1])
    flag = lambda bound: (bound <= MAX_PLAIN_LOGIT).astype(jnp.int32).reshape(1)
    ya = _moba(rel_bias, flag(_logit_bound(qn_a, kn_a, rel_bias[:, :HEADS])),
               qat, r3(ka), vat, kmean.reshape(bsz, s // BLK, WIDTH), r3(sga))
    yb = _dsa(rel_bias, flag(_logit_bound(qn_b, kn_b, rel_bias[:, HEADS:])),
              qbt, qit, wit, r3(kn), vbt, r3(sgb))
    return _outproj(x2, ya.reshape(-1, WIDTH), yb.reshape(-1, WIDTH), sma, smb,
                    w_br_a.astype(BF16), w_br_b.astype(BF16), w_out.astype(BF16))


def kernel(x, norm_g, w_in, q_norm_a, k_norm_a, q_norm_b, k_norm_b,
           w_branch_a, w_branch_b, w_out, rel_bias):
    bsz, s, d = x.shape
    assert d == D_MODEL and s % BLK == 0 and s // BLK <= 31
    h = x.reshape(bsz * s, d)
    for l in range(norm_g.shape[0]):
        h = _layer(h, bsz, s, norm_g[l], w_in[l], q_norm_a[l], k_norm_a[l], q_norm_b[l],
                   k_norm_b[l], w_branch_a[l], w_branch_b[l], w_out[l], rel_bias)
    return h.reshape(bsz, s, d)
```

```python
import functools
import math

import numpy as np
import jax
import jax.numpy as jnp
from jax import lax
from jax.experimental import pallas as pl
from jax.experimental.pallas import tpu as pltpu

D_MODEL = 1024
HEADS = 8
HEAD_DIM = 64
WIDTH = HEADS * HEAD_DIM
BLK = 256
MOBA_TOPK = 3
IDX_HEADS = 4
IDX_DIM = 64
DSA_TOPK_MAX = 256
REL_BUCKETS = 32
REL_MAX_DIST = 128
EPS = 1e-6
NEG = -1e30
M_INIT = -1e20
MAX_PLAIN_LOGIT = 40.0
INT_MIN = -2 ** 31
KEY_BITS = 32
LANES = 128
VMEM_LIMIT = 56 * 1024 * 1024

F32 = jnp.float32
BF16 = jnp.bfloat16

CN_KA, CN_KB2, CN_KI4, CN_GA, CN_GB, CN_MA, CN_MB = 0, 512, 640, 896, 1408, 1920, 2944
N_NAT = 3968
RT_QA, RT_QB, RT_VA, RT_QI, RT_VB, RT_WI = 0, 512, 1024, 1536, 1792, 1856
N_TR = 1872
WI_ROWS = 8
ROWS_IN = 512
ROWS_OUT = 512
ONES_ROWS = 16
VROWS = HEAD_DIM + ONES_ROWS


def _bucket_starts():
    n = np.arange(0, 2 * REL_MAX_DIST + 2)
    max_exact = REL_BUCKETS // 2
    nf = np.maximum(n, 1).astype(np.float64)
    v = np.log(nf / max_exact) / math.log(REL_MAX_DIST / max_exact) * (REL_BUCKETS - max_exact)
    large = np.minimum(max_exact + np.trunc(v).astype(np.int64), REL_BUCKETS - 1)
    bucket = np.where(n < max_exact, n, large)
    frac = np.abs(v - np.round(v))[(n > max_exact) & (n < REL_MAX_DIST)]
    assert frac.min() > 1e-3
    return [int(np.argmax(bucket >= k)) for k in range(REL_BUCKETS)]


_STARTS = _bucket_starts()


def _dot(a, b):
    return jnp.dot(a, b, preferred_element_type=F32)


def _inproj_kernel(x_ref, gn_ref, wn_ref, wt_ref, ghn_ref, ght_ref,
                   ka_ref, kn_ref, sga_ref, sgb_ref, sma_ref, smb_ref, kmean_ref,
                   qat_ref, qbt_ref, vat_ref, qit_ref, vbt_ref, wit_ref):
    x = x_ref[...]
    r = lax.rsqrt(jnp.mean(x * x, axis=-1, keepdims=True) + EPS)
    hf = (x * r) * gn_ref[...]
    h = hf.astype(BF16)
    ht = hf.T.astype(BF16)

    gr = lax.broadcasted_iota(jnp.int32, (BLK, BLK), 0) // HEAD_DIM
    gc = lax.broadcasted_iota(jnp.int32, (BLK, BLK), 1) // HEAD_DIM
    group = jnp.where(gr == gc, 1.0, 0.0).astype(BF16)

    def proj(c0, n):
        return _dot(h, wn_ref[:, c0:c0 + n])

    def proj_t(r0, n):
        return _dot(wt_ref[r0:r0 + n, :], ht)

    def headnorm(y, c0):
        n = y.shape[-1]
        sq = y * y
        hi = sq.astype(BF16)
        lo = (sq - hi.astype(F32)).astype(BF16)
        g = group[:n, :n]
        ss = _dot(hi, g) + _dot(lo, g)
        return (y * lax.rsqrt(ss * (1.0 / HEAD_DIM) + EPS)) * ghn_ref[:, c0:c0 + n]

    def headnorm_t(yt, r0):
        ss = jnp.sum(yt * yt, axis=0, keepdims=True)
        return (yt * lax.rsqrt(ss * (1.0 / HEAD_DIM) + EPS)) * ght_ref[r0:r0 + HEAD_DIM, :]

    for c in range(0, WIDTH, BLK):
        kn = headnorm(proj(CN_KA + c, BLK), CN_KA + c)
        ka_ref[:, c:c + BLK] = kn.astype(BF16)
        for t in range(kn.shape[0] // BLK):
            kmean_ref[t, :, c:c + BLK] = jnp.mean(kn[t * BLK:(t + 1) * BLK], axis=0, keepdims=True)
        sga_ref[:, c:c + BLK] = jax.nn.silu(proj(CN_GA + c, BLK))
        sgb_ref[:, c:c + BLK] = jax.nn.silu(proj(CN_GB + c, BLK))
    kn_ref[:, 0:128] = headnorm(proj(CN_KB2, 128), CN_KB2).astype(BF16)
    kn_ref[:, 128:384] = proj(CN_KI4, 256).astype(BF16)
    for c in range(0, D_MODEL, BLK):
        sma_ref[:, c:c + BLK] = jax.nn.sigmoid(proj(CN_MA + c, BLK))
        smb_ref[:, c:c + BLK] = jax.nn.sigmoid(proj(CN_MB + c, BLK))

    def put_t(ref, r0, val):
        for t in range(ROWS_IN // BLK):
            ref[t, r0:r0 + val.shape[0], :] = val[:, t * BLK:(t + 1) * BLK]

    scale = HEAD_DIM ** -0.5
    ones = jnp.ones((ONES_ROWS, ROWS_IN), BF16)
    qat = proj_t(RT_QA, WIDTH)
    qbt = proj_t(RT_QB, WIDTH)
    vat = proj_t(RT_VA, WIDTH)
    for hd in range(HEADS):
        r0 = hd * HEAD_DIM
        put_t(qat_ref, r0, (headnorm_t(qat[r0:r0 + HEAD_DIM], RT_QA + r0) * scale).astype(BF16))
        put_t(qbt_ref, r0, (headnorm_t(qbt[r0:r0 + HEAD_DIM], RT_QB + r0) * scale).astype(BF16))
        put_t(vat_ref, hd * VROWS, vat[r0:r0 + HEAD_DIM].astype(BF16))
        put_t(vat_ref, hd * VROWS + HEAD_DIM, ones)
    put_t(qit_ref, 0, (proj_t(RT_QI, 256) * (IDX_DIM ** -0.5)).astype(BF16))
    tail = proj_t(RT_VB, N_TR - RT_VB)
    put_t(vbt_ref, 0, tail[0:HEAD_DIM].astype(BF16))
    put_t(vbt_ref, HEAD_DIM, ones)
    put_t(wit_ref, 0, tail[RT_WI - RT_VB:RT_WI - RT_VB + WI_ROWS] * (IDX_HEADS ** -0.5))


def _inproj(x2, gn, wn, wt, ghn, ght):
    m = x2.shape[0]
    nt = m // BLK
    tpb = ROWS_IN // BLK
    row = lambda n: pl.BlockSpec((ROWS_IN, n), lambda i: (i, 0))
    full = lambda a: pl.BlockSpec(a.shape, lambda i: (0,) * a.ndim)
    tile = lambda n: pl.BlockSpec((tpb, n, BLK), lambda i: (i, 0, 0))
    out_shapes = [
        jax.ShapeDtypeStruct((m, WIDTH), BF16),
        jax.ShapeDtypeStruct((m, 384), BF16),
        jax.ShapeDtypeStruct((m, WIDTH), F32),
        jax.ShapeDtypeStruct((m, WIDTH), F32),
        jax.ShapeDtypeStruct((m, D_MODEL), F32),
        jax.ShapeDtypeStruct((m, D_MODEL), F32),
        jax.ShapeDtypeStruct((nt, 1, WIDTH), F32),
        jax.ShapeDtypeStruct((nt, WIDTH, BLK), BF16),
        jax.ShapeDtypeStruct((nt, WIDTH, BLK), BF16),
        jax.ShapeDtypeStruct((nt, HEADS * VROWS, BLK), BF16),
        jax.ShapeDtypeStruct((nt, 256, BLK), BF16),
        jax.ShapeDtypeStruct((nt, VROWS, BLK), BF16),
        jax.ShapeDtypeStruct((nt, WI_ROWS, BLK), F32),
    ]
    out_specs = [row(WIDTH), row(384), row(WIDTH), row(WIDTH), row(D_MODEL), row(D_MODEL),
                 pl.BlockSpec((tpb, 1, WIDTH), lambda i: (i, 0, 0)),
                 tile(WIDTH), tile(WIDTH), tile(HEADS * VROWS), tile(256), tile(VROWS), tile(WI_ROWS)]
    return pl.pallas_call(
        _inproj_kernel,
        grid=(nt // tpb,),
        in_specs=[row(D_MODEL), full(gn), full(wn), full(wt), full(ghn), full(ght)],
        out_specs=out_specs,
        out_shape=out_shapes,
        compiler_params=pltpu.CompilerParams(
            dimension_semantics=("arbitrary",), vmem_limit_bytes=VMEM_LIMIT),
        name="inproj",
    )(x2, gn, wn, wt, ghn, ght)


def _build_bias_tiles(rel_ref, bias_ref, head0):
    kpos = lax.broadcasted_iota(jnp.int32, (BLK, BLK), 0)
    qpos = lax.broadcasted_iota(jnp.int32, (BLK, BLK), 1)
    for kind in range(2):
        dist = qpos - kpos + BLK * kind
        for h in range(HEADS):
            t = jnp.full((BLK, BLK), rel_ref[0, head0 + h], F32)
            for k in range(1, REL_BUCKETS):
                t = jnp.where(dist >= _STARTS[k], rel_ref[k, head0 + h], t)
            if kind == 0:
                t = jnp.where(dist >= 0, t, NEG)
            bias_ref[3 * h + kind] = t
    for h in range(HEADS):
        bias_ref[3 * h + 2] = jnp.full((BLK, BLK), rel_ref[REL_BUCKETS - 1, head0 + h], F32)


def _store_masked_queries(qt_ref, qm_ref):
    rowh = lax.broadcasted_iota(jnp.int32, (LANES, BLK), 0)
    for p in range(HEADS // 2):
        q2 = qt_ref[0, LANES * p:LANES * (p + 1), :]
        for hh in range(2):
            qm_ref[2 * p + hh] = jnp.where((rowh >= HEAD_DIM) == bool(hh), q2, jnp.zeros_like(q2))


def _flash_exact(npair, score_fn, pv_fn, s_ref, acc_ref, sel_fn=None):
    def stage(t, ms):
        out = []
        for h in range(HEADS):
            m_new = ms[h]
            for u in range(2):
                j = 2 * t + u
                s = score_fn(j, h)
                s_ref[h, u * BLK:(u + 1) * BLK, :] = s
                cand = jnp.max(s, axis=0, keepdims=True)
                if sel_fn is not None:
                    cand = jnp.where(sel_fn(j, h), cand, M_INIT)
                m_new = jnp.maximum(m_new, cand)
            out.append(m_new)
        return tuple(out)

    def finish(t, ms_old, ms_new):
        for h in range(HEADS):
            acc = jnp.exp(ms_old[h] - ms_new[h]) * acc_ref[h]
            for u in range(2):
                j = 2 * t + u
                p = jnp.exp(s_ref[h, u * BLK:(u + 1) * BLK, :] - ms_new[h])
                pv = pv_fn(j, h, p.astype(BF16))
                acc = acc + (pv if sel_fn is None else jnp.where(sel_fn(j, h), pv, 0.0))
            acc_ref[h] = acc

    def step(t, ms):
        m_new = stage(t, ms)
        finish(t, ms, m_new)
        return m_new

    lax.fori_loop(0, npair, step, tuple(jnp.full((1, BLK), M_INIT, F32) for _ in range(HEADS)))


def _flash_bounded(pairs, score_fn, pv_fn, p_ref, acc_ref, sel_fn=None, side_fn=None):
    def step(t, c):
        for h in range(HEADS):
            for u in range(2):
                p_ref[2 * h + u] = jnp.exp(score_fn(2 * t + u, h)).astype(BF16)
        if side_fn is not None:
            side_fn(t)
        for h in range(HEADS):
            acc = acc_ref[h]
            for u in range(2):
                j = 2 * t + u
                pv = pv_fn(j, h, p_ref[2 * h + u])
                acc = acc + (pv if sel_fn is None else jnp.where(sel_fn(j, h), pv, 0.0))
            acc_ref[h] = acc
        return c

    lax.fori_loop(pairs[0], pairs[1], step, 0)


def _logit_bound(q_gain, k_gain, bias):
    return (1.01 * HEAD_DIM ** 0.5 * jnp.max(jnp.abs(q_gain)) * jnp.max(jnp.abs(k_gain))
            + jnp.max(jnp.abs(bias)))


def _write_output(acc_ref, sg_ref, o_ref):
    def head_out(h):
        return acc_ref[h, 0:HEAD_DIM, :] / acc_ref[h, HEAD_DIM:HEAD_DIM + 1, :]

    for p in range(HEADS // 2):
        ot = jnp.concatenate([head_out(2 * p), head_out(2 * p + 1)], axis=0)
        cs = slice(LANES * p, LANES * (p + 1))
        o_ref[0, :, cs] = (ot.T * sg_ref[0, :, cs]).astype(BF16)


def _bit_planes(u):
    x = [u[8 * a:8 * a + 8, :] for a in range(KEY_BITS)]
    j, m = 16, 0x0000FFFF
    while j:
        for k in range(KEY_BITS):
            if not k & j:
                t = (x[k] ^ jnp.right_shift(x[k + j], j)) & m
                x[k] = x[k] ^ t
                x[k + j] = x[k + j] ^ jnp.left_shift(t, j)
        j >>= 1
        m ^= (m << j) & 0xFFFFFFFF
    return x


def _attn_kernel(rel_ref, bounded_ref,
                 qat_ref, ka_ref, vat_ref, kmean_ref, sga_ref,
                 qt_ref, qit_ref, wit_ref, kn_ref, vt_ref, sg_ref,
                 oa_ref, o_ref,
                 biasa_ref, bias_ref, qma_ref, qm_ref, acca_ref, acc_ref, s_ref, p_ref,
                 key_ref, plane_ref, alive_ref, madd_ref, *, topk):
    b = pl.program_id(0)
    i = pl.program_id(1)
    nblk = i + 1
    nb = key_ref.shape[0]
    bounded_a = bounded_ref[0] == 1
    bounded_b = bounded_ref[1] == 1

    @pl.when((b == 0) & (i == 0))
    def _():
        _build_bias_tiles(rel_ref, biasa_ref, 0)
        _build_bias_tiles(rel_ref, bias_ref, HEADS)
        plane_ref[...] = jnp.zeros_like(plane_ref)

    _store_masked_queries(qat_ref, qma_ref)
    _store_masked_queries(qt_ref, qm_ref)
    acca_ref[...] = jnp.zeros_like(acca_ref)
    acc_ref[...] = jnp.zeros_like(acc_ref)

    nidx = lax.broadcasted_iota(jnp.int32, (nb, BLK), 0)
    sel_bits = []
    for h in range(HEADS):
        cs = slice(LANES * (h // 2), LANES * (h // 2 + 1))
        gate = _dot(kmean_ref[0, :, cs].astype(BF16), qma_ref[h])
        g = jnp.where(nidx < i, gate, -jnp.inf)
        bits = jnp.left_shift(1, jnp.full((1, BLK), i, jnp.int32))
        for _ in range(MOBA_TOPK):
            mx = jnp.max(g, axis=0, keepdims=True)
            am = jnp.min(jnp.where(g == mx, nidx, nb), axis=0, keepdims=True)
            bits = bits | jnp.where(mx > -jnp.inf, jnp.left_shift(1, am), 0)
            g = jnp.where(nidx == am, -jnp.inf, g)
        sel_bits.append(bits)

    def score_a(j, h):
        rows = pl.ds(pl.multiple_of(j * BLK, BLK), BLK)
        cs = slice(LANES * (h // 2), LANES * (h // 2 + 1))
        return _dot(ka_ref[0, rows, cs], qma_ref[h]) + biasa_ref[3 * h + jnp.clip(i - j, 0, 2)]

    def pv_a(j, h, p):
        return _dot(vat_ref[0, j, h * VROWS:(h + 1) * VROWS, :], p)

    def sel_a(j, h):
        return (jnp.right_shift(sel_bits[h], j) & 1) == 1

    kpos = lax.broadcasted_iota(jnp.int32, (BLK, BLK), 0)
    qpos = lax.broadcasted_iota(jnp.int32, (BLK, BLK), 1)

    qit = qit_ref[0]
    wit = wit_ref[0]
    qims = [jnp.where((kpos // IDX_DIM) == h, qit, jnp.zeros_like(qit)) for h in range(IDX_HEADS)]
    whs = [wit[h:h + 1, :] for h in range(IDX_HEADS)]

    def score_block(j, causal):
        rows = pl.ds(pl.multiple_of(j * BLK, BLK), BLK)
        ki4 = kn_ref[0, rows, 128:384]
        score = jnp.zeros((BLK, BLK), F32)
        for h in range(IDX_HEADS):
            score = score + jnp.maximum(_dot(ki4, qims[h]), 0.0) * whs[h]
        bits = lax.bitcast_convert_type(score, jnp.int32)
        key = bits ^ (jnp.right_shift(bits, 31) & 0x7FFFFFFF)
        if causal:
            key = jnp.where(kpos <= qpos, key, INT_MIN)
        key_ref[j] = key
        planes = _bit_planes(key ^ INT_MIN)
        for r in range(KEY_BITS):
            plane_ref[j, r] = planes[r]

    pad = jnp.minimum(nblk, nb - 1)
    key_ref[pad] = jnp.full((BLK, BLK), INT_MIN, jnp.int32)

    def score_pair(t, c=0):
        score_block(2 * t, False)
        score_block(2 * t + 1, False)
        return c

    def score_single(_, c):
        score_block(i - 1, False)
        return c

    npair = (nblk + 1) // 2
    full_pairs = i // 2

    @pl.when(bounded_a)
    def _():
        _flash_bounded((0, full_pairs), score_a, pv_a, p_ref, acca_ref, sel_a, side_fn=score_pair)
        _flash_bounded((full_pairs, npair), score_a, pv_a, p_ref, acca_ref, sel_a)

    @pl.when(jnp.logical_not(bounded_a))
    def _():
        _flash_exact(npair, score_a, pv_a, s_ref, acca_ref, sel_a)
        lax.fori_loop(0, full_pairs, score_pair, 0)

    _write_output(acca_ref, sga_ref, oa_ref)
    lax.fori_loop(0, i % 2, score_single, 0)
    score_block(i, True)

    for j in range(nb):
        alive_ref[j] = jnp.where(j < nblk, jnp.full((8, BLK), -1, jnp.int32), 0)

    def radix_round(r, carry):
        above, thr_u = carry
        ones_here = jnp.zeros((8, BLK), jnp.int32)
        for j in range(nb):
            ones_here = ones_here + lax.population_count(alive_ref[j] & plane_ref[j, r])
        ones_here = jnp.sum(ones_here, axis=0, keepdims=True)
        take = (above + ones_here) >= topk
        flip = jnp.where(take, 0, -1)
        for j in range(nb):
            alive_ref[j] = alive_ref[j] & (plane_ref[j, r] ^ flip)
        return (jnp.where(take, above, above + ones_here),
                thr_u | jnp.where(take, jnp.left_shift(1, KEY_BITS - 1 - r), 0))

    zero = jnp.zeros((1, BLK), jnp.int32)
    above, thr_u = lax.fori_loop(0, KEY_BITS, radix_round, (zero, zero))
    thr = thr_u ^ INT_MIN

    need = jnp.where(thr > INT_MIN, topk - above, 0).astype(F32)
    lower = jnp.where(qpos <= kpos, 1.0, 0.0).astype(BF16)

    def mask_body(t, carry):
        for j in (2 * t, 2 * t + 1):
            k = key_ref[j]
            eq = k == thr
            pref = _dot(lower, jnp.where(eq, 1.0, 0.0).astype(BF16))
            sel = (k > thr) | (eq & ((carry + pref) <= need))
            madd_ref[j] = jnp.where(sel, 0.0, NEG)
            carry = carry + pref[BLK - 1:BLK, :]
        return carry

    lax.fori_loop(0, npair, mask_body, jnp.zeros((1, BLK), F32))

    def score(j, h):
        rows = pl.ds(pl.multiple_of(j * BLK, BLK), BLK)
        return (_dot(kn_ref[0, rows, 0:128], qm_ref[h])
                + bias_ref[3 * h + jnp.clip(i - j, 0, 2)] + madd_ref[j])

    def pv(j, h, p):
        return _dot(vt_ref[0, j], p)

    @pl.when(bounded_b)
    def _():
        _flash_bounded((0, npair), score, pv, p_ref, acc_ref)

    @pl.when(jnp.logical_not(bounded_b))
    def _():
        _flash_exact(npair, score, pv, s_ref, acc_ref)

    _write_output(acc_ref, sg_ref, o_ref)


def _attention(rel_bias, bounded, qat, ka, vat, kmean, sga, qbt, qit, wit, kn, vbt, sgb):
    bsz, s, _ = kn.shape
    nb = s // BLK
    topk = min(DSA_TOPK_MAX, s // 4)
    smem = pl.BlockSpec(memory_space=pltpu.SMEM)
    qspec = pl.BlockSpec((1, BLK, WIDTH), lambda b, i: (b, i, 0))
    tile = lambda n: pl.BlockSpec((1, n, BLK), lambda b, i: (b * nb + i, 0, 0))
    row = lambda *shape: pl.BlockSpec((1,) + shape, lambda b, i: (b,) + (0,) * len(shape),
                                      pipeline_mode=pl.Buffered(1))
    out = jax.ShapeDtypeStruct((bsz, s, WIDTH), BF16)
    return pl.pallas_call(
        functools.partial(_attn_kernel, topk=topk),
        grid=(bsz, nb),
        in_specs=[smem, smem,
                  tile(WIDTH), row(s, WIDTH), row(nb, HEADS * VROWS, BLK), row(nb, WIDTH), qspec,
                  tile(WIDTH), tile(256), tile(WI_ROWS), row(s, 384), row(nb, VROWS, BLK), qspec],
        out_specs=[qspec, qspec],
        out_shape=[out, out],
        scratch_shapes=[pltpu.VMEM((3 * HEADS, BLK, BLK), F32),
                        pltpu.VMEM((3 * HEADS, BLK, BLK), F32),
                        pltpu.VMEM((HEADS, LANES, BLK), BF16),
                        pltpu.VMEM((HEADS, LANES, BLK), BF16),
                        pltpu.VMEM((HEADS, VROWS, BLK), F32),
                        pltpu.VMEM((HEADS, VROWS, BLK), F32),
                        pltpu.VMEM((HEADS, 2 * BLK, BLK), F32),
                        pltpu.VMEM((2 * HEADS, BLK, BLK), BF16),
                        pltpu.VMEM((nb, BLK, BLK), jnp.int32),
                        pltpu.VMEM((nb, KEY_BITS, 8, BLK), jnp.int32),
                        pltpu.VMEM((nb, 8, BLK), jnp.int32),
                        pltpu.VMEM((nb, BLK, BLK), F32)],
        compiler_params=pltpu.CompilerParams(
            dimension_semantics=("arbitrary", "arbitrary"), vmem_limit_bytes=VMEM_LIMIT),
        name="attention",
    )(rel_bias, bounded, qat, ka, vat.reshape(bsz, nb, HEADS * VROWS, BLK), kmean, sga,
      qbt, qit, wit, kn, vbt.reshape(bsz, nb, VROWS, BLK), sgb)


def _outproj_kernel(x_ref, ya_ref, yb_ref, sma_ref, smb_ref, wa_ref, wb_ref, wo_ref, o_ref):
    merged = (sma_ref[...] * _dot(ya_ref[...], wa_ref[...])
              + smb_ref[...] * _dot(yb_ref[...], wb_ref[...]))
    o_ref[...] = x_ref[...] + _dot(merged.astype(BF16), wo_ref[...])


def _outproj(x2, ya, yb, sma, smb, wa, wb, wo):
    m = x2.shape[0]
    row = lambda n: pl.BlockSpec((ROWS_OUT, n), lambda i: (i, 0))
    full = lambda a: pl.BlockSpec(a.shape, lambda i: (0,) * a.ndim)
    return pl.pallas_call(
        _outproj_kernel,
        grid=(m // ROWS_OUT,),
        in_specs=[row(D_MODEL), row(WIDTH), row(WIDTH), row(D_MODEL), row(D_MODEL),
                  full(wa), full(wb), full(wo)],
        out_specs=row(D_MODEL),
        out_shape=jax.ShapeDtypeStruct((m, D_MODEL), F32),
        compiler_params=pltpu.CompilerParams(
            dimension_semantics=("arbitrary",), vmem_limit_bytes=VMEM_LIMIT),
        name="outproj",
    )(x2, ya, yb, sma, smb, wa, wb, wo)


def _pack_w_in(w):
    o = 0
    parts = {}
    for name, n in (("qa", WIDTH), ("ka", WIDTH), ("va", WIDTH), ("ga", WIDTH), ("qb", WIDTH),
                    ("kb", HEAD_DIM), ("vb", HEAD_DIM), ("gb", WIDTH), ("qi", IDX_HEADS * IDX_DIM),
                    ("ki", IDX_DIM), ("wi", IDX_HEADS), ("ma", D_MODEL), ("mb", D_MODEL)):
        parts[name] = w[:, o:o + n]
        o += n
    wn = jnp.concatenate([parts["ka"], parts["kb"], parts["kb"],
                          parts["ki"], parts["ki"], parts["ki"], parts["ki"],
                          parts["ga"], parts["gb"], parts["ma"], parts["mb"]], axis=1).astype(BF16)
    pad = jnp.zeros((w.shape[0], N_TR - RT_WI - IDX_HEADS), w.dtype)
    wt = jnp.concatenate([parts["qa"], parts["qb"], parts["va"], parts["qi"], parts["vb"],
                          parts["wi"], pad], axis=1).T.astype(BF16)
    assert wn.shape[1] == N_NAT and wt.shape[0] == N_TR
    return wn, wt


def _layer(x2, bsz, s, gn, w_in, qn_a, kn_a, qn_b, kn_b, w_br_a, w_br_b, w_out, rel_bias):
    wn, wt = _pack_w_in(w_in)
    ghn = jnp.concatenate([jnp.tile(kn_a, HEADS), jnp.tile(kn_b, 2)])[None, :]
    ght = jnp.concatenate([jnp.tile(qn_a, HEADS), jnp.tile(qn_b, HEADS)])[:, None]
    (ka, kn, sga, sgb, sma, smb, kmean, qat, qbt, vat, qit, vbt, wit) = _inproj(
        x2, gn[None, :], wn, wt, ghn, ght)
    r3 = lambda a: a.reshape(bsz, s, a.shape[-1])
    bounds = jnp.stack([_logit_bound(qn_a, kn_a, rel_bias[:, :HEADS]),
                        _logit_bound(qn_b, kn_b, rel_bias[:, HEADS:])])
    ya, yb = _attention(rel_bias, (bounds <= MAX_PLAIN_LOGIT).astype(jnp.int32),
                        qat, r3(ka), vat, kmean.reshape(bsz, s // BLK, WIDTH), r3(sga),
                        qbt, qit, wit, r3(kn), vbt, r3(sgb))
    return _outproj(x2, ya.reshape(-1, WIDTH), yb.reshape(-1, WIDTH), sma, smb,
                    w_br_a.astype(BF16), w_br_b.astype(BF16), w_out.astype(BF16))


def kernel(x, norm_g, w_in, q_norm_a, k_norm_a, q_norm_b, k_norm_b,
           w_branch_a, w_branch_b, w_out, rel_bias):
    bsz, s, d = x.shape
    assert d == D_MODEL and s % BLK == 0 and s // BLK <= 31
    h = x.reshape(bsz * s, d)
    for l in range(norm_g.shape[0]):
        h = _layer(h, bsz, s, norm_g[l], w_in[l], q_norm_a[l], k_norm_a[l], q_norm_b[l],
                   k_norm_b[l], w_branch_a[l], w_branch_b[l], w_out[l], rel_bias)
    return h.reshape(bsz, s, d)
```

```python
import functools
import math

import numpy as np
import jax
import jax.numpy as jnp
from jax import lax
from jax.experimental import pallas as pl
from jax.experimental.pallas import tpu as pltpu

D_MODEL = 1024
HEADS = 8
HEAD_DIM = 64
WIDTH = HEADS * HEAD_DIM
BLK = 256
MOBA_TOPK = 3
IDX_HEADS = 4
IDX_DIM = 64
DSA_TOPK_MAX = 256
REL_BUCKETS = 32
REL_MAX_DIST = 128
EPS = 1e-6
NEG = -1e30
M_INIT = -1e20
MAX_PLAIN_LOGIT = 40.0
INT_MIN = -2 ** 31
KEY_BITS = 32
LANES = 128
VMEM_LIMIT = 56 * 1024 * 1024

F32 = jnp.float32
BF16 = jnp.bfloat16

CN_KA, CN_KB2, CN_KI4, CN_GA, CN_GB, CN_MA, CN_MB = 0, 512, 640, 896, 1408, 1920, 2944
N_NAT = 3968
RT_QA, RT_QB, RT_VA, RT_QI, RT_VB, RT_WI = 0, 512, 1024, 1536, 1792, 1856
N_TR = 1872
WI_ROWS = 8
ROWS_IN = 512
ROWS_OUT = 512
ONES_ROWS = 16
VROWS = HEAD_DIM + ONES_ROWS


def _bucket_starts():
    n = np.arange(0, 2 * REL_MAX_DIST + 2)
    max_exact = REL_BUCKETS // 2
    nf = np.maximum(n, 1).astype(np.float64)
    v = np.log(nf / max_exact) / math.log(REL_MAX_DIST / max_exact) * (REL_BUCKETS - max_exact)
    large = np.minimum(max_exact + np.trunc(v).astype(np.int64), REL_BUCKETS - 1)
    bucket = np.where(n < max_exact, n, large)
    frac = np.abs(v - np.round(v))[(n > max_exact) & (n < REL_MAX_DIST)]
    assert frac.min() > 1e-3
    return [int(np.argmax(bucket >= k)) for k in range(REL_BUCKETS)]


_STARTS = _bucket_starts()


def _dot(a, b):
    return jnp.dot(a, b, preferred_element_type=F32)


def _inproj_kernel(x_ref, gn_ref, wn_ref, wt_ref, ghn_ref, ght_ref,
                   ka_ref, kn_ref, sga_ref, sgb_ref, sma_ref, smb_ref, kmean_ref,
                   qat_ref, qbt_ref, vat_ref, qit_ref, vbt_ref, wit_ref):
    x = x_ref[...]
    r = lax.rsqrt(jnp.mean(x * x, axis=-1, keepdims=True) + EPS)
    hf = (x * r) * gn_ref[...]
    h = hf.astype(BF16)
    ht = hf.T.astype(BF16)

    gr = lax.broadcasted_iota(jnp.int32, (BLK, BLK), 0) // HEAD_DIM
    gc = lax.broadcasted_iota(jnp.int32, (BLK, BLK), 1) // HEAD_DIM
    group = jnp.where(gr == gc, 1.0, 0.0).astype(BF16)

    def proj(c0, n):
        return _dot(h, wn_ref[:, c0:c0 + n])

    def proj_t(r0, n):
        return _dot(wt_ref[r0:r0 + n, :], ht)

    def headnorm(y, c0):
        n = y.shape[-1]
        sq = y * y
        hi = sq.astype(BF16)
        lo = (sq - hi.astype(F32)).astype(BF16)
        g = group[:n, :n]
        ss = _dot(hi, g) + _dot(lo, g)
        return (y * lax.rsqrt(ss * (1.0 / HEAD_DIM) + EPS)) * ghn_ref[:, c0:c0 + n]

    def headnorm_t(yt, r0):
        ss = jnp.sum(yt * yt, axis=0, keepdims=True)
        return (yt * lax.rsqrt(ss * (1.0 / HEAD_DIM) + EPS)) * ght_ref[r0:r0 + HEAD_DIM, :]

    for c in range(0, WIDTH, BLK):
        kn = headnorm(proj(CN_KA + c, BLK), CN_KA + c)
        ka_ref[:, c:c + BLK] = kn.astype(BF16)
        for t in range(kn.shape[0] // BLK):
            kmean_ref[t, :, c:c + BLK] = jnp.mean(kn[t * BLK:(t + 1) * BLK], axis=0, keepdims=True)
        sga_ref[:, c:c + BLK] = jax.nn.silu(proj(CN_GA + c, BLK))
        sgb_ref[:, c:c + BLK] = jax.nn.silu(proj(CN_GB + c, BLK))
    kn_ref[:, 0:128] = headnorm(proj(CN_KB2, 128), CN_KB2).astype(BF16)
    kn_ref[:, 128:384] = proj(CN_KI4, 256).astype(BF16)
    for c in range(0, D_MODEL, BLK):
        sma_ref[:, c:c + BLK] = jax.nn.sigmoid(proj(CN_MA + c, BLK))
        smb_ref[:, c:c + BLK] = jax.nn.sigmoid(proj(CN_MB + c, BLK))

    def put_t(ref, r0, val):
        for t in range(ROWS_IN // BLK):
            ref[t, r0:r0 + val.shape[0], :] = val[:, t * BLK:(t + 1) * BLK]

    scale = HEAD_DIM ** -0.5
    ones = jnp.ones((ONES_ROWS, ROWS_IN), BF16)
    qat = proj_t(RT_QA, WIDTH)
    qbt = proj_t(RT_QB, WIDTH)
    vat = proj_t(RT_VA, WIDTH)
    for hd in range(HEADS):
        r0 = hd * HEAD_DIM
        put_t(qat_ref, r0, (headnorm_t(qat[r0:r0 + HEAD_DIM], RT_QA + r0) * scale).astype(BF16))
        put_t(qbt_ref, r0, (headnorm_t(qbt[r0:r0 + HEAD_DIM], RT_QB + r0) * scale).astype(BF16))
        put_t(vat_ref, hd * VROWS, vat[r0:r0 + HEAD_DIM].astype(BF16))
        put_t(vat_ref, hd * VROWS + HEAD_DIM, ones)
    put_t(qit_ref, 0, (proj_t(RT_QI, 256) * (IDX_DIM ** -0.5)).astype(BF16))
    tail = proj_t(RT_VB, N_TR - RT_VB)
    put_t(vbt_ref, 0, tail[0:HEAD_DIM].astype(BF16))
    put_t(vbt_ref, HEAD_DIM, ones)
    put_t(wit_ref, 0, tail[RT_WI - RT_VB:RT_WI - RT_VB + WI_ROWS] * (IDX_HEADS ** -0.5))


def _inproj(x2, gn, wn, wt, ghn, ght):
    m = x2.shape[0]
    nt = m // BLK
    tpb = ROWS_IN // BLK
    row = lambda n: pl.BlockSpec((ROWS_IN, n), lambda i: (i, 0))
    full = lambda a: pl.BlockSpec(a.shape, lambda i: (0,) * a.ndim)
    tile = lambda n: pl.BlockSpec((tpb, n, BLK), lambda i: (i, 0, 0))
    out_shapes = [
        jax.ShapeDtypeStruct((m, WIDTH), BF16),
        jax.ShapeDtypeStruct((m, 384), BF16),
        jax.ShapeDtypeStruct((m, WIDTH), F32),
        jax.ShapeDtypeStruct((m, WIDTH), F32),
        jax.ShapeDtypeStruct((m, D_MODEL), F32),
        jax.ShapeDtypeStruct((m, D_MODEL), F32),
        jax.ShapeDtypeStruct((nt, 1, WIDTH), F32),
        jax.ShapeDtypeStruct((nt, WIDTH, BLK), BF16),
        jax.ShapeDtypeStruct((nt, WIDTH, BLK), BF16),
        jax.ShapeDtypeStruct((nt, HEADS * VROWS, BLK), BF16),
        jax.ShapeDtypeStruct((nt, 256, BLK), BF16),
        jax.ShapeDtypeStruct((nt, VROWS, BLK), BF16),
        jax.ShapeDtypeStruct((nt, WI_ROWS, BLK), F32),
    ]
    out_specs = [row(WIDTH), row(384), row(WIDTH), row(WIDTH), row(D_MODEL), row(D_MODEL),
                 pl.BlockSpec((tpb, 1, WIDTH), lambda i: (i, 0, 0)),
                 tile(WIDTH), tile(WIDTH), tile(HEADS * VROWS), tile(256), tile(VROWS), tile(WI_ROWS)]
    return pl.pallas_call(
        _inproj_kernel,
        grid=(nt // tpb,),
        in_specs=[row(D_MODEL), full(gn), full(wn), full(wt), full(ghn), full(ght)],
        out_specs=out_specs,
        out_shape=out_shapes,
        compiler_params=pltpu.CompilerParams(
            dimension_semantics=("arbitrary",), vmem_limit_bytes=VMEM_LIMIT),
        name="inproj",
    )(x2, gn, wn, wt, ghn, ght)


def _build_bias_tiles(rel_ref, bias_ref, head0):
    kpos = lax.broadcasted_iota(jnp.int32, (BLK, BLK), 0)
    qpos = lax.broadcasted_iota(jnp.int32, (BLK, BLK), 1)
    for kind in range(2):
        dist = qpos - kpos + BLK * kind
        for h in range(HEADS):
            t = jnp.full((BLK, BLK), rel_ref[0, head0 + h], F32)
            for k in range(1, REL_BUCKETS):
                t = jnp.where(dist >= _STARTS[k], rel_ref[k, head0 + h], t)
            if kind == 0:
                t = jnp.where(dist >= 0, t, NEG)
            bias_ref[3 * h + kind] = t
    for h in range(HEADS):
        bias_ref[3 * h + 2] = jnp.full((BLK, BLK), rel_ref[REL_BUCKETS - 1, head0 + h], F32)


def _store_masked_queries(qt_ref, qm_ref):
    rowh = lax.broadcasted_iota(jnp.int32, (LANES, BLK), 0)
    for p in range(HEADS // 2):
        q2 = qt_ref[0, LANES * p:LANES * (p + 1), :]
        for hh in range(2):
            qm_ref[2 * p + hh] = jnp.where((rowh >= HEAD_DIM) == bool(hh), q2, jnp.zeros_like(q2))


def _flash_exact(npair, score_fn, pv_fn, s_ref, acc_ref, sel_fn=None):
    def stage(t, ms):
        out = []
        for h in range(HEADS):
            m_new = ms[h]
            for u in range(2):
                j = 2 * t + u
                s = score_fn(j, h)
                s_ref[h, u * BLK:(u + 1) * BLK, :] = s
                cand = jnp.max(s, axis=0, keepdims=True)
                if sel_fn is not None:
                    cand = jnp.where(sel_fn(j, h), cand, M_INIT)
                m_new = jnp.maximum(m_new, cand)
            out.append(m_new)
        return tuple(out)

    def finish(t, ms_old, ms_new):
        for h in range(HEADS):
            acc = jnp.exp(ms_old[h] - ms_new[h]) * acc_ref[h]
            for u in range(2):
                j = 2 * t + u
                p = jnp.exp(s_ref[h, u * BLK:(u + 1) * BLK, :] - ms_new[h])
                pv = pv_fn(j, h, p.astype(BF16))
                acc = acc + (pv if sel_fn is None else jnp.where(sel_fn(j, h), pv, 0.0))
            acc_ref[h] = acc

    def step(t, ms):
        m_new = stage(t, ms)
        finish(t, ms, m_new)
        return m_new

    lax.fori_loop(0, npair, step, tuple(jnp.full((1, BLK), M_INIT, F32) for _ in range(HEADS)))


def _flash_bounded(pairs, score_fn, pv_fn, p_ref, acc_ref, sel_fn=None, side_fn=None, carry=0,
                   side_first=False):
    def step(t, c):
        if side_fn is not None and side_first:
            c = side_fn(t, c)
        for h in range(HEADS):
            for u in range(2):
                p_ref[2 * h + u] = jnp.exp(score_fn(2 * t + u, h)).astype(BF16)
        if side_fn is not None and not side_first:
            c = side_fn(t, c)
        for h in range(HEADS):
            acc = acc_ref[h]
            for u in range(2):
                j = 2 * t + u
                pv = pv_fn(j, h, p_ref[2 * h + u])
                acc = acc + (pv if sel_fn is None else jnp.where(sel_fn(j, h), pv, 0.0))
            acc_ref[h] = acc
        return c

    return lax.fori_loop(pairs[0], pairs[1], step, carry)


def _logit_bound(q_gain, k_gain, bias):
    return (1.01 * HEAD_DIM ** 0.5 * jnp.max(jnp.abs(q_gain)) * jnp.max(jnp.abs(k_gain))
            + jnp.max(jnp.abs(bias)))


def _write_output(acc_ref, sg_ref, o_ref):
    def head_out(h):
        return acc_ref[h, 0:HEAD_DIM, :] / acc_ref[h, HEAD_DIM:HEAD_DIM + 1, :]

    for p in range(HEADS // 2):
        ot = jnp.concatenate([head_out(2 * p), head_out(2 * p + 1)], axis=0)
        cs = slice(LANES * p, LANES * (p + 1))
        o_ref[0, :, cs] = (ot.T * sg_ref[0, :, cs]).astype(BF16)


def _bit_planes(u):
    x = [u[8 * a:8 * a + 8, :] for a in range(KEY_BITS)]
    j, m = 16, 0x0000FFFF
    while j:
        for k in range(KEY_BITS):
            if not k & j:
                t = (x[k] ^ jnp.right_shift(x[k + j], j)) & m
                x[k] = x[k] ^ t
                x[k + j] = x[k + j] ^ jnp.left_shift(t, j)
        j >>= 1
        m ^= (m << j) & 0xFFFFFFFF
    return x


def _attn_kernel(rel_ref, bounded_ref,
                 qat_ref, ka_ref, vat_ref, kmean_ref, sga_ref,
                 qt_ref, qit_ref, wit_ref, kn_ref, vt_ref, sg_ref,
                 oa_ref, o_ref,
                 biasa_ref, bias_ref, qma_ref, qm_ref, acca_ref, acc_ref, s_ref, p_ref,
                 key_ref, plane_ref, alive_ref, thr_ref, madd_ref, *, topk):
    b = pl.program_id(0)
    i = pl.program_id(1)
    nblk = i + 1
    nb = key_ref.shape[0]
    bounded_a = bounded_ref[0] == 1
    bounded_b = bounded_ref[1] == 1

    @pl.when((b == 0) & (i == 0))
    def _():
        _build_bias_tiles(rel_ref, biasa_ref, 0)
        _build_bias_tiles(rel_ref, bias_ref, HEADS)
        plane_ref[...] = jnp.zeros_like(plane_ref)

    _store_masked_queries(qat_ref, qma_ref)
    _store_masked_queries(qt_ref, qm_ref)
    acca_ref[...] = jnp.zeros_like(acca_ref)
    acc_ref[...] = jnp.zeros_like(acc_ref)

    nidx = lax.broadcasted_iota(jnp.int32, (nb, BLK), 0)
    sel_bits = []
    for h in range(HEADS):
        cs = slice(LANES * (h // 2), LANES * (h // 2 + 1))
        gate = _dot(kmean_ref[0, :, cs].astype(BF16), qma_ref[h])
        g = jnp.where(nidx < i, gate, -jnp.inf)
        bits = jnp.left_shift(1, jnp.full((1, BLK), i, jnp.int32))
        for _ in range(MOBA_TOPK):
            mx = jnp.max(g, axis=0, keepdims=True)
            am = jnp.min(jnp.where(g == mx, nidx, nb), axis=0, keepdims=True)
            bits = bits | jnp.where(mx > -jnp.inf, jnp.left_shift(1, am), 0)
            g = jnp.where(nidx == am, -jnp.inf, g)
        sel_bits.append(bits)

    def score_a(j, h):
        rows = pl.ds(pl.multiple_of(j * BLK, BLK), BLK)
        cs = slice(LANES * (h // 2), LANES * (h // 2 + 1))
        return _dot(ka_ref[0, rows, cs], qma_ref[h]) + biasa_ref[3 * h + jnp.clip(i - j, 0, 2)]

    def pv_a(j, h, p):
        return _dot(vat_ref[0, j, h * VROWS:(h + 1) * VROWS, :], p)

    def sel_a(j, h):
        return (jnp.right_shift(sel_bits[h], j) & 1) == 1

    kpos = lax.broadcasted_iota(jnp.int32, (BLK, BLK), 0)
    qpos = lax.broadcasted_iota(jnp.int32, (BLK, BLK), 1)

    qit = qit_ref[0]
    wit = wit_ref[0]
    qims = [jnp.where((kpos // IDX_DIM) == h, qit, jnp.zeros_like(qit)) for h in range(IDX_HEADS)]
    whs = [wit[h:h + 1, :] for h in range(IDX_HEADS)]

    def score_block(j, causal):
        rows = pl.ds(pl.multiple_of(j * BLK, BLK), BLK)
        ki4 = kn_ref[0, rows, 128:384]
        score = jnp.zeros((BLK, BLK), F32)
        for h in range(IDX_HEADS):
            score = score + jnp.maximum(_dot(ki4, qims[h]), 0.0) * whs[h]
        bits = lax.bitcast_convert_type(score, jnp.int32)
        key = bits ^ (jnp.right_shift(bits, 31) & 0x7FFFFFFF)
        if causal:
            key = jnp.where(kpos <= qpos, key, INT_MIN)
        key_ref[j] = key
        planes = _bit_planes(key ^ INT_MIN)
        for r in range(KEY_BITS):
            plane_ref[j, r] = planes[r]

    pad = jnp.minimum(nblk, nb - 1)
    key_ref[pad] = jnp.full((BLK, BLK), INT_MIN, jnp.int32)

    def score_pair(t, c=0):
        score_block(2 * t, False)
        score_block(2 * t + 1, False)
        return c

    def score_single(_, c):
        score_block(i - 1, False)
        return c

    npair = (nblk + 1) // 2
    full_pairs = i // 2

    @pl.when(bounded_a)
    def _():
        _flash_bounded((0, full_pairs), score_a, pv_a, p_ref, acca_ref, sel_a, side_fn=score_pair)

    @pl.when(jnp.logical_not(bounded_a))
    def _():
        _flash_exact(npair, score_a, pv_a, s_ref, acca_ref, sel_a)
        lax.fori_loop(0, full_pairs, score_pair, 0)

    lax.fori_loop(0, i % 2, score_single, 0)
    score_block(i, True)

    for j in range(nb):
        alive_ref[j] = jnp.where(j < nblk, jnp.full((8, BLK), -1, jnp.int32), 0)

    def radix_round(r, carry):
        above, thr_u = carry
        ones_here = jnp.zeros((8, BLK), jnp.int32)
        for j in range(nb):
            ones_here = ones_here + lax.population_count(alive_ref[j] & plane_ref[j, r])
        ones_here = jnp.sum(ones_here, axis=0, keepdims=True)
        take = (above + ones_here) >= topk
        flip = jnp.where(take, 0, -1)
        for j in range(nb):
            alive_ref[j] = alive_ref[j] & (plane_ref[j, r] ^ flip)
        return (jnp.where(take, above, above + ones_here),
                thr_u | jnp.where(take, jnp.left_shift(1, KEY_BITS - 1 - r), 0))

    def radix_all(_, carry):
        for r in range(KEY_BITS):
            carry = radix_round(r, carry)
        return carry

    zero = jnp.zeros((1, BLK), jnp.int32)

    @pl.when(bounded_a)
    def _():
        found = _flash_bounded((full_pairs, npair), score_a, pv_a, p_ref, acca_ref, sel_a,
                               side_fn=radix_all, carry=(zero, zero))
        thr_ref[0:1, :], thr_ref[1:2, :] = found

    @pl.when(jnp.logical_not(bounded_a))
    def _():
        thr_ref[0:1, :], thr_ref[1:2, :] = lax.fori_loop(0, KEY_BITS, radix_round, (zero, zero))

    _write_output(acca_ref, sga_ref, oa_ref)
    above = thr_ref[0:1, :]
    thr = thr_ref[1:2, :] ^ INT_MIN

    need = jnp.where(thr > INT_MIN, topk - above, 0).astype(F32)
    lower = jnp.where(qpos <= kpos, 1.0, 0.0).astype(BF16)

    def mask_pair(t, carry):
        for j in (2 * t, 2 * t + 1):
            k = key_ref[j]
            eq = k == thr
            pref = _dot(lower, jnp.where(eq, 1.0, 0.0).astype(BF16))
            sel = (k > thr) | (eq & ((carry + pref) <= need))
            madd_ref[j] = jnp.where(sel, 0.0, NEG)
            carry = carry + pref[BLK - 1:BLK, :]
        return carry

    no_ties_yet = jnp.zeros((1, BLK), F32)

    def score(j, h):
        rows = pl.ds(pl.multiple_of(j * BLK, BLK), BLK)
        return (_dot(kn_ref[0, rows, 0:128], qm_ref[h])
                + bias_ref[3 * h + jnp.clip(i - j, 0, 2)] + madd_ref[j])

    def pv(j, h, p):
        return _dot(vt_ref[0, j], p)

    @pl.when(bounded_b)
    def _():
        _flash_bounded((0, npair), score, pv, p_ref, acc_ref, side_fn=mask_pair, carry=no_ties_yet,
                       side_first=True)

    @pl.when(jnp.logical_not(bounded_b))
    def _():
        lax.fori_loop(0, npair, mask_pair, no_ties_yet)
        _flash_exact(npair, score, pv, s_ref, acc_ref)

    _write_output(acc_ref, sg_ref, o_ref)


def _attention(rel_bias, bounded, qat, ka, vat, kmean, sga, qbt, qit, wit, kn, vbt, sgb):
    bsz, s, _ = kn.shape
    nb = s // BLK
    topk = min(DSA_TOPK_MAX, s // 4)
    smem = pl.BlockSpec(memory_space=pltpu.SMEM)
    qspec = pl.BlockSpec((1, BLK, WIDTH), lambda b, i: (b, i, 0))
    tile = lambda n: pl.BlockSpec((1, n, BLK), lambda b, i: (b * nb + i, 0, 0))
    row = lambda *shape: pl.BlockSpec((1,) + shape, lambda b, i: (b,) + (0,) * len(shape),
                                      pipeline_mode=pl.Buffered(1))
    out = jax.ShapeDtypeStruct((bsz, s, WIDTH), BF16)
    return pl.pallas_call(
        functools.partial(_attn_kernel, topk=topk),
        grid=(bsz, nb),
        in_specs=[smem, smem,
                  tile(WIDTH), row(s, WIDTH), row(nb, HEADS * VROWS, BLK), row(nb, WIDTH), qspec,
                  tile(WIDTH), tile(256), tile(WI_ROWS), row(s, 384), row(nb, VROWS, BLK), qspec],
        out_specs=[qspec, qspec],
        out_shape=[out, out],
        scratch_shapes=[pltpu.VMEM((3 * HEADS, BLK, BLK), F32),
                        pltpu.VMEM((3 * HEADS, BLK, BLK), F32),
                        pltpu.VMEM((HEADS, LANES, BLK), BF16),
                        pltpu.VMEM((HEADS, LANES, BLK), BF16),
                        pltpu.VMEM((HEADS, VROWS, BLK), F32),
                        pltpu.VMEM((HEADS, VROWS, BLK), F32),
                        pltpu.VMEM((HEADS, 2 * BLK, BLK), F32),
                        pltpu.VMEM((2 * HEADS, BLK, BLK), BF16),
                        pltpu.VMEM((nb, BLK, BLK), jnp.int32),
                        pltpu.VMEM((nb, KEY_BITS, 8, BLK), jnp.int32),
                        pltpu.VMEM((nb, 8, BLK), jnp.int32),
                        pltpu.VMEM((8, BLK), jnp.int32),
                        pltpu.VMEM((nb, BLK, BLK), F32)],
        compiler_params=pltpu.CompilerParams(
            dimension_semantics=("arbitrary", "arbitrary"), vmem_limit_bytes=VMEM_LIMIT),
        name="attention",
    )(rel_bias, bounded, qat, ka, vat.reshape(bsz, nb, HEADS * VROWS, BLK), kmean, sga,
      qbt, qit, wit, kn, vbt.reshape(bsz, nb, VROWS, BLK), sgb)


def _outproj_kernel(x_ref, ya_ref, yb_ref, sma_ref, smb_ref, wa_ref, wb_ref, wo_ref, o_ref):
    merged = (sma_ref[...] * _dot(ya_ref[...], wa_ref[...])
              + smb_ref[...] * _dot(yb_ref[...], wb_ref[...]))
    o_ref[...] = x_ref[...] + _dot(merged.astype(BF16), wo_ref[...])


def _outproj(x2, ya, yb, sma, smb, wa, wb, wo):
    m = x2.shape[0]
    row = lambda n: pl.BlockSpec((ROWS_OUT, n), lambda i: (i, 0))
    full = lambda a: pl.BlockSpec(a.shape, lambda i: (0,) * a.ndim)
    return pl.pallas_call(
        _outproj_kernel,
        grid=(m // ROWS_OUT,),
        in_specs=[row(D_MODEL), row(WIDTH), row(WIDTH), row(D_MODEL), row(D_MODEL),
                  full(wa), full(wb), full(wo)],
        out_specs=row(D_MODEL),
        out_shape=jax.ShapeDtypeStruct((m, D_MODEL), F32),
        compiler_params=pltpu.CompilerParams(
            dimension_semantics=("arbitrary",), vmem_limit_bytes=VMEM_LIMIT),
        name="outproj",
    )(x2, ya, yb, sma, smb, wa, wb, wo)


def _pack_w_in(w):
    o = 0
    parts = {}
    for name, n in (("qa", WIDTH), ("ka", WIDTH), ("va", WIDTH), ("ga", WIDTH), ("qb", WIDTH),
                    ("kb", HEAD_DIM), ("vb", HEAD_DIM), ("gb", WIDTH), ("qi", IDX_HEADS * IDX_DIM),
                    ("ki", IDX_DIM), ("wi", IDX_HEADS), ("ma", D_MODEL), ("mb", D_MODEL)):
        parts[name] = w[:, o:o + n]
        o += n
    wn = jnp.concatenate([parts["ka"], parts["kb"], parts["kb"],
                          parts["ki"], parts["ki"], parts["ki"], parts["ki"],
                          parts["ga"], parts["gb"], parts["ma"], parts["mb"]], axis=1).astype(BF16)
    pad = jnp.zeros((w.shape[0], N_TR - RT_WI - IDX_HEADS), w.dtype)
    wt = jnp.concatenate([parts["qa"], parts["qb"], parts["va"], parts["qi"], parts["vb"],
                          parts["wi"], pad], axis=1).T.astype(BF16)
    assert wn.shape[1] == N_NAT and wt.shape[0] == N_TR
    return wn, wt


def _layer(x2, bsz, s, gn, w_in, qn_a, kn_a, qn_b, kn_b, w_br_a, w_br_b, w_out, rel_bias):
    wn, wt = _pack_w_in(w_in)
    ghn = jnp.concatenate([jnp.tile(kn_a, HEADS), jnp.tile(kn_b, 2)])[None, :]
    ght = jnp.concatenate([jnp.tile(qn_a, HEADS), jnp.tile(qn_b, HEADS)])[:, None]
    (ka, kn, sga, sgb, sma, smb, kmean, qat, qbt, vat, qit, vbt, wit) = _inproj(
        x2, gn[None, :], wn, wt, ghn, ght)
    r3 = lambda a: a.reshape(bsz, s, a.shape[-1])
    bounds = jnp.stack([_logit_bound(qn_a, kn_a, rel_bias[:, :HEADS]),
                        _logit_bound(qn_b, kn_b, rel_bias[:, HEADS:])])
    ya, yb = _attention(rel_bias, (bounds <= MAX_PLAIN_LOGIT).astype(jnp.int32),
                        qat, r3(ka), vat, kmean.reshape(bsz, s // BLK, WIDTH), r3(sga),
                        qbt, qit, wit, r3(kn), vbt, r3(sgb))
    return _outproj(x2, ya.reshape(-1, WIDTH), yb.reshape(-1, WIDTH), sma, smb,
                    w_br_a.astype(BF16), w_br_b.astype(BF16), w_out.astype(BF16))


def kernel(x, norm_g, w_in, q_norm_a, k_norm_a, q_norm_b, k_norm_b,
           w_branch_a, w_branch_b, w_out, rel_bias):
    bsz, s, d = x.shape
    assert d == D_MODEL and s % BLK == 0 and s // BLK <= 31
    h = x.reshape(bsz * s, d)
    for l in range(norm_g.shape[0]):
        h = _layer(h, bsz, s, norm_g[l], w_in[l], q_norm_a[l], k_norm_a[l], q_norm_b[l],
                   k_norm_b[l], w_branch_a[l], w_branch_b[l], w_out[l], rel_bias)
    return h.reshape(bsz, s, d)
```

```python
import functools
import math

import numpy as np
import jax
import jax.numpy as jnp
from jax import lax
from jax.experimental import pallas as pl
from jax.experimental.pallas import tpu as pltpu

D_MODEL = 1024
HEADS = 8
HEAD_DIM = 64
WIDTH = HEADS * HEAD_DIM
BLK = 256
MOBA_TOPK = 3
IDX_HEADS = 4
IDX_DIM = 64
DSA_TOPK_MAX = 256
REL_BUCKETS = 32
REL_MAX_DIST = 128
EPS = 1e-6
NEG = -1e30
M_INIT = -1e20
MAX_PLAIN_LOGIT = 40.0
INT_MIN = -2 ** 31
KEY_BITS = 32
LANES = 128
VMEM_LIMIT = 56 * 1024 * 1024

F32 = jnp.float32
BF16 = jnp.bfloat16

CN_KA, CN_KB2, CN_KI4, CN_GA, CN_GB, CN_MA, CN_MB = 0, 512, 640, 896, 1408, 1920, 2944
N_NAT = 3968
RT_QA, RT_QB, RT_VA, RT_QI, RT_VB, RT_WI = 0, 512, 1024, 1536, 1792, 1856
N_TR = 1872
WI_ROWS = 8
ROWS_IN = 512
ROWS_OUT = 512
ONES_ROWS = 16
VROWS = HEAD_DIM + ONES_ROWS


def _bucket_starts():
    n = np.arange(0, 2 * REL_MAX_DIST + 2)
    max_exact = REL_BUCKETS // 2
    nf = np.maximum(n, 1).astype(np.float64)
    v = np.log(nf / max_exact) / math.log(REL_MAX_DIST / max_exact) * (REL_BUCKETS - max_exact)
    large = np.minimum(max_exact + np.trunc(v).astype(np.int64), REL_BUCKETS - 1)
    bucket = np.where(n < max_exact, n, large)
    frac = np.abs(v - np.round(v))[(n > max_exact) & (n < REL_MAX_DIST)]
    assert frac.min() > 1e-3
    return [int(np.argmax(bucket >= k)) for k in range(REL_BUCKETS)]


_STARTS = _bucket_starts()


def _dot(a, b):
    return jnp.dot(a, b, preferred_element_type=F32)


def _inproj_kernel(x_ref, gn_ref, wn_ref, wt_ref, ghn_ref, ght_ref,
                   ka_ref, kn_ref, sga_ref, sgb_ref, sma_ref, smb_ref, kmean_ref,
                   qat_ref, qbt_ref, vat_ref, qit_ref, vbt_ref, wit_ref):
    x = x_ref[...]
    r = lax.rsqrt(jnp.mean(x * x, axis=-1, keepdims=True) + EPS)
    hf = (x * r) * gn_ref[...]
    h = hf.astype(BF16)
    ht = hf.T.astype(BF16)

    gr = lax.broadcasted_iota(jnp.int32, (BLK, BLK), 0) // HEAD_DIM
    gc = lax.broadcasted_iota(jnp.int32, (BLK, BLK), 1) // HEAD_DIM
    group = jnp.where(gr == gc, 1.0, 0.0).astype(BF16)

    def proj(c0, n):
        return _dot(h, wn_ref[:, c0:c0 + n])

    def proj_t(r0, n):
        return _dot(wt_ref[r0:r0 + n, :], ht)

    def headnorm(y, c0):
        n = y.shape[-1]
        sq = y * y
        hi = sq.astype(BF16)
        lo = (sq - hi.astype(F32)).astype(BF16)
        g = group[:n, :n]
        ss = _dot(hi, g) + _dot(lo, g)
        return (y * lax.rsqrt(ss * (1.0 / HEAD_DIM) + EPS)) * ghn_ref[:, c0:c0 + n]

    def headnorm_t(yt, r0):
        ss = jnp.sum(yt * yt, axis=0, keepdims=True)
        return (yt * lax.rsqrt(ss * (1.0 / HEAD_DIM) + EPS)) * ght_ref[r0:r0 + HEAD_DIM, :]

    for c in range(0, WIDTH, BLK):
        kn = headnorm(proj(CN_KA + c, BLK), CN_KA + c)
        ka_ref[:, c:c + BLK] = kn.astype(BF16)
        for t in range(kn.shape[0] // BLK):
            kmean_ref[t, :, c:c + BLK] = jnp.mean(kn[t * BLK:(t + 1) * BLK], axis=0, keepdims=True)
        sga_ref[:, c:c + BLK] = jax.nn.silu(proj(CN_GA + c, BLK))
        sgb_ref[:, c:c + BLK] = jax.nn.silu(proj(CN_GB + c, BLK))
    kn_ref[:, 0:128] = headnorm(proj(CN_KB2, 128), CN_KB2).astype(BF16)
    kn_ref[:, 128:384] = proj(CN_KI4, 256).astype(BF16)
    for c in range(0, D_MODEL, BLK):
        sma_ref[:, c:c + BLK] = jax.nn.sigmoid(proj(CN_MA + c, BLK))
        smb_ref[:, c:c + BLK] = jax.nn.sigmoid(proj(CN_MB + c, BLK))

    def put_t(ref, r0, val):
        for t in range(ROWS_IN // BLK):
            ref[t, r0:r0 + val.shape[0], :] = val[:, t * BLK:(t + 1) * BLK]

    scale = HEAD_DIM ** -0.5
    ones = jnp.ones((ONES_ROWS, ROWS_IN), BF16)
    qat = proj_t(RT_QA, WIDTH)
    qbt = proj_t(RT_QB, WIDTH)
    vat = proj_t(RT_VA, WIDTH)
    for hd in range(HEADS):
        r0 = hd * HEAD_DIM
        put_t(qat_ref, r0, (headnorm_t(qat[r0:r0 + HEAD_DIM], RT_QA + r0) * scale).astype(BF16))
        put_t(qbt_ref, r0, (headnorm_t(qbt[r0:r0 + HEAD_DIM], RT_QB + r0) * scale).astype(BF16))
        put_t(vat_ref, hd * VROWS, vat[r0:r0 + HEAD_DIM].astype(BF16))
        put_t(vat_ref, hd * VROWS + HEAD_DIM, ones)
    put_t(qit_ref, 0, (proj_t(RT_QI, 256) * (IDX_DIM ** -0.5)).astype(BF16))
    tail = proj_t(RT_VB, N_TR - RT_VB)
    put_t(vbt_ref, 0, tail[0:HEAD_DIM].astype(BF16))
    put_t(vbt_ref, HEAD_DIM, ones)
    put_t(wit_ref, 0, tail[RT_WI - RT_VB:RT_WI - RT_VB + WI_ROWS] * (IDX_HEADS ** -0.5))


def _inproj(x2, gn, wn, wt, ghn, ght):
    m = x2.shape[0]
    nt = m // BLK
    tpb = ROWS_IN // BLK
    row = lambda n: pl.BlockSpec((ROWS_IN, n), lambda i: (i, 0))
    full = lambda a: pl.BlockSpec(a.shape, lambda i: (0,) * a.ndim)
    tile = lambda n: pl.BlockSpec((tpb, n, BLK), lambda i: (i, 0, 0))
    out_shapes = [
        jax.ShapeDtypeStruct((m, WIDTH), BF16),
        jax.ShapeDtypeStruct((m, 384), BF16),
        jax.ShapeDtypeStruct((m, WIDTH), F32),
        jax.ShapeDtypeStruct((m, WIDTH), F32),
        jax.ShapeDtypeStruct((m, D_MODEL), F32),
        jax.ShapeDtypeStruct((m, D_MODEL), F32),
        jax.ShapeDtypeStruct((nt, 1, WIDTH), F32),
        jax.ShapeDtypeStruct((nt, WIDTH, BLK), BF16),
        jax.ShapeDtypeStruct((nt, WIDTH, BLK), BF16),
        jax.ShapeDtypeStruct((nt, HEADS * VROWS, BLK), BF16),
        jax.ShapeDtypeStruct((nt, 256, BLK), BF16),
        jax.ShapeDtypeStruct((nt, VROWS, BLK), BF16),
        jax.ShapeDtypeStruct((nt, WI_ROWS, BLK), F32),
    ]
    out_specs = [row(WIDTH), row(384), row(WIDTH), row(WIDTH), row(D_MODEL), row(D_MODEL),
                 pl.BlockSpec((tpb, 1, WIDTH), lambda i: (i, 0, 0)),
                 tile(WIDTH), tile(WIDTH), tile(HEADS * VROWS), tile(256), tile(VROWS), tile(WI_ROWS)]
    return pl.pallas_call(
        _inproj_kernel,
        grid=(nt // tpb,),
        in_specs=[row(D_MODEL), full(gn), full(wn), full(wt), full(ghn), full(ght)],
        out_specs=out_specs,
        out_shape=out_shapes,
        compiler_params=pltpu.CompilerParams(
            dimension_semantics=("arbitrary",), vmem_limit_bytes=VMEM_LIMIT),
        name="inproj",
    )(x2, gn, wn, wt, ghn, ght)


def _build_bias_tiles(rel_ref, bias_ref, head0):
    kpos = lax.broadcasted_iota(jnp.int32, (BLK, BLK), 0)
    qpos = lax.broadcasted_iota(jnp.int32, (BLK, BLK), 1)
    for kind in range(2):
        dist = qpos - kpos + BLK * kind
        for h in range(HEADS):
            t = jnp.full((BLK, BLK), rel_ref[0, head0 + h], F32)
            for k in range(1, REL_BUCKETS):
                t = jnp.where(dist >= _STARTS[k], rel_ref[k, head0 + h], t)
            if kind == 0:
                t = jnp.where(dist >= 0, t, NEG)
            bias_ref[3 * h + kind] = t
    for h in range(HEADS):
        bias_ref[3 * h + 2] = jnp.full((BLK, BLK), rel_ref[REL_BUCKETS - 1, head0 + h], F32)


def _store_masked_queries(qt_ref, qm_ref):
    rowh = lax.broadcasted_iota(jnp.int32, (LANES, BLK), 0)
    for p in range(HEADS // 2):
        q2 = qt_ref[0, LANES * p:LANES * (p + 1), :]
        for hh in range(2):
            qm_ref[2 * p + hh] = jnp.where((rowh >= HEAD_DIM) == bool(hh), q2, jnp.zeros_like(q2))


def _flash_exact(npair, score_fn, pv_fn, s_ref, acc_ref, sel_fn=None):
    def stage(t, ms):
        out = []
        for h in range(HEADS):
            m_new = ms[h]
            for u in range(2):
                j = 2 * t + u
                s = score_fn(j, h)
                s_ref[h, u * BLK:(u + 1) * BLK, :] = s
                cand = jnp.max(s, axis=0, keepdims=True)
                if sel_fn is not None:
                    cand = jnp.where(sel_fn(j, h), cand, M_INIT)
                m_new = jnp.maximum(m_new, cand)
            out.append(m_new)
        return tuple(out)

    def finish(t, ms_old, ms_new):
        for h in range(HEADS):
            acc = jnp.exp(ms_old[h] - ms_new[h]) * acc_ref[h]
            for u in range(2):
                j = 2 * t + u
                p = jnp.exp(s_ref[h, u * BLK:(u + 1) * BLK, :] - ms_new[h])
                pv = pv_fn(j, h, p.astype(BF16))
                acc = acc + (pv if sel_fn is None else jnp.where(sel_fn(j, h), pv, 0.0))
            acc_ref[h] = acc

    def step(t, ms):
        m_new = stage(t, ms)
        finish(t, ms, m_new)
        return m_new

    lax.fori_loop(0, npair, step, tuple(jnp.full((1, BLK), M_INIT, F32) for _ in range(HEADS)))


def _flash_bounded(pairs, score_fn, pv_fn, p_ref, acc_ref, sel_fn=None, side_fn=None, pv_scale=None):
    def step(t, c):
        for h in range(HEADS):
            for u in range(2):
                p_ref[2 * h + u] = jnp.exp(score_fn(2 * t + u, h)).astype(BF16)
        if side_fn is not None:
            side_fn(t)
        for h in range(HEADS):
            new = None
            for u in range(2):
                j = 2 * t + u
                pv = pv_fn(j, h, p_ref[2 * h + u])
                pv = pv if sel_fn is None else jnp.where(sel_fn(j, h), pv, 0.0)
                new = pv if new is None else new + pv
            acc_ref[h] = acc_ref[h] + (new if pv_scale is None else new * pv_scale[h])
        return c

    lax.fori_loop(pairs[0], pairs[1], step, 0)


def _logit_bound(q_gain, k_gain, bias):
    return (1.01 * HEAD_DIM ** 0.5 * jnp.max(jnp.abs(q_gain)) * jnp.max(jnp.abs(k_gain))
            + jnp.max(jnp.abs(bias)))


def _write_output(acc_ref, sg_ref, o_ref):
    def head_out(h):
        return acc_ref[h, 0:HEAD_DIM, :] / acc_ref[h, HEAD_DIM:HEAD_DIM + 1, :]

    for p in range(HEADS // 2):
        ot = jnp.concatenate([head_out(2 * p), head_out(2 * p + 1)], axis=0)
        cs = slice(LANES * p, LANES * (p + 1))
        o_ref[0, :, cs] = (ot.T * sg_ref[0, :, cs]).astype(BF16)


def _bit_planes(u):
    x = [u[8 * a:8 * a + 8, :] for a in range(KEY_BITS)]
    j, m = 16, 0x0000FFFF
    while j:
        for k in range(KEY_BITS):
            if not k & j:
                t = (x[k] ^ jnp.right_shift(x[k + j], j)) & m
                x[k] = x[k] ^ t
                x[k + j] = x[k + j] ^ jnp.left_shift(t, j)
        j >>= 1
        m ^= (m << j) & 0xFFFFFFFF
    return x


def _attn_kernel(rel_ref, bounded_ref,
                 qat_ref, ka_ref, vat_ref, kmean_ref, sga_ref,
                 qt_ref, qit_ref, wit_ref, kn_ref, vt_ref, sg_ref,
                 oa_ref, o_ref,
                 biasa_ref, bias_ref, qma_ref, qm_ref, acca_ref, acc_ref, s_ref, p_ref,
                 key_ref, plane_ref, alive_ref, madd_ref, *, topk):
    b = pl.program_id(0)
    i = pl.program_id(1)
    nblk = i + 1
    nb = key_ref.shape[0]
    bounded_a = bounded_ref[0] == 1
    bounded_b = bounded_ref[1] == 1

    @pl.when((b == 0) & (i == 0))
    def _():
        _build_bias_tiles(rel_ref, biasa_ref, 0)
        _build_bias_tiles(rel_ref, bias_ref, HEADS)
        plane_ref[...] = jnp.zeros_like(plane_ref)

    _store_masked_queries(qat_ref, qma_ref)
    _store_masked_queries(qt_ref, qm_ref)
    acca_ref[...] = jnp.zeros_like(acca_ref)
    acc_ref[...] = jnp.zeros_like(acc_ref)

    nidx = lax.broadcasted_iota(jnp.int32, (nb, BLK), 0)
    sel_bits = []
    for h in range(HEADS):
        cs = slice(LANES * (h // 2), LANES * (h // 2 + 1))
        gate = _dot(kmean_ref[0, :, cs].astype(BF16), qma_ref[h])
        g = jnp.where(nidx < i, gate, -jnp.inf)
        bits = jnp.left_shift(1, jnp.full((1, BLK), i, jnp.int32))
        for _ in range(MOBA_TOPK):
            mx = jnp.max(g, axis=0, keepdims=True)
            am = jnp.min(jnp.where(g == mx, nidx, nb), axis=0, keepdims=True)
            bits = bits | jnp.where(mx > -jnp.inf, jnp.left_shift(1, am), 0)
            g = jnp.where(nidx == am, -jnp.inf, g)
        sel_bits.append(bits)

    def score_a(j, h):
        rows = pl.ds(pl.multiple_of(j * BLK, BLK), BLK)
        cs = slice(LANES * (h // 2), LANES * (h // 2 + 1))
        return _dot(ka_ref[0, rows, cs], qma_ref[h]) + biasa_ref[3 * h + jnp.clip(i - j, 0, 2)]

    def pv_a(j, h, p):
        return _dot(vat_ref[0, j, h * VROWS:(h + 1) * VROWS, :], p)

    def sel_a(j, h):
        return (jnp.right_shift(sel_bits[h], j) & 1) == 1

    kpos = lax.broadcasted_iota(jnp.int32, (BLK, BLK), 0)
    qpos = lax.broadcasted_iota(jnp.int32, (BLK, BLK), 1)

    qit = qit_ref[0]
    wit = wit_ref[0]
    qims = [jnp.where((kpos // IDX_DIM) == h, qit, jnp.zeros_like(qit)) for h in range(IDX_HEADS)]
    whs = [wit[h:h + 1, :] for h in range(IDX_HEADS)]

    def score_block(j, causal):
        rows = pl.ds(pl.multiple_of(j * BLK, BLK), BLK)
        ki4 = kn_ref[0, rows, 128:384]
        score = jnp.zeros((BLK, BLK), F32)
        for h in range(IDX_HEADS):
            score = score + jnp.maximum(_dot(ki4, qims[h]), 0.0) * whs[h]
        bits = lax.bitcast_convert_type(score, jnp.int32)
        key = bits ^ (jnp.right_shift(bits, 31) & 0x7FFFFFFF)
        if causal:
            key = jnp.where(kpos <= qpos, key, INT_MIN)
        key_ref[j] = key
        planes = _bit_planes(key ^ INT_MIN)
        for r in range(KEY_BITS):
            plane_ref[j, r] = planes[r]

    pad = jnp.minimum(nblk, nb - 1)
    key_ref[pad] = jnp.full((BLK, BLK), INT_MIN, jnp.int32)

    def score_pair(t, c=0):
        score_block(2 * t, False)
        score_block(2 * t + 1, False)
        return c

    def score_single(_, c):
        score_block(i - 1, False)
        return c

    npair = (nblk + 1) // 2
    full_pairs = i // 2

    @pl.when(bounded_a)
    def _():
        _flash_bounded((0, full_pairs), score_a, pv_a, p_ref, acca_ref, sel_a, side_fn=score_pair)
        _flash_bounded((full_pairs, npair), score_a, pv_a, p_ref, acca_ref, sel_a)

    @pl.when(jnp.logical_not(bounded_a))
    def _():
        _flash_exact(npair, score_a, pv_a, s_ref, acca_ref, sel_a)
        lax.fori_loop(0, full_pairs, score_pair, 0)

    _write_output(acca_ref, sga_ref, oa_ref)
    lax.fori_loop(0, i % 2, score_single, 0)
    score_block(i, True)

    for j in range(nb):
        alive_ref[j] = jnp.where(j < nblk, jnp.full((8, BLK), -1, jnp.int32), 0)

    def radix_round(r, carry):
        above, thr_u = carry
        ones_here = jnp.zeros((8, BLK), jnp.int32)
        for j in range(nb):
            ones_here = ones_here + lax.population_count(alive_ref[j] & plane_ref[j, r])
        ones_here = jnp.sum(ones_here, axis=0, keepdims=True)
        take = (above + ones_here) >= topk
        flip = jnp.where(take, 0, -1)
        for j in range(nb):
            alive_ref[j] = alive_ref[j] & (plane_ref[j, r] ^ flip)
        return (jnp.where(take, above, above + ones_here),
                thr_u | jnp.where(take, jnp.left_shift(1, KEY_BITS - 1 - r), 0))

    zero = jnp.zeros((1, BLK), jnp.int32)
    above, thr_u = lax.fori_loop(0, KEY_BITS, radix_round, (zero, zero))
    thr = thr_u ^ INT_MIN

    need = jnp.where(thr > INT_MIN, topk - above, 0).astype(F32)
    lower = jnp.where(qpos <= kpos, 1.0, 0.0).astype(BF16)

    def mask_body(t, carry):
        for j in (2 * t, 2 * t + 1):
            k = key_ref[j]
            eq = k == thr
            pref = _dot(lower, jnp.where(eq, 1.0, 0.0).astype(BF16))
            sel = (k > thr) | (eq & ((carry + pref) <= need))
            madd_ref[j] = jnp.where(sel, 0.0, NEG)
            carry = carry + pref[BLK - 1:BLK, :]
        return carry

    lax.fori_loop(0, npair, mask_body, jnp.zeros((1, BLK), F32))

    def score(j, h):
        rows = pl.ds(pl.multiple_of(j * BLK, BLK), BLK)
        return (_dot(kn_ref[0, rows, 0:128], qm_ref[h])
                + bias_ref[3 * h + jnp.clip(i - j, 0, 2)] + madd_ref[j])

    def pv(j, h, p):
        return _dot(vt_ref[0, j], p)

    far_pairs = jnp.maximum(i - 1, 0) // 2

    def score_far(j, h):
        rows = pl.ds(pl.multiple_of(j * BLK, BLK), BLK)
        return _dot(kn_ref[0, rows, 0:128], qm_ref[h]) + madd_ref[j]

    @pl.when(bounded_b)
    def _():
        far_factor = [jnp.exp(jnp.full((1, BLK), rel_ref[REL_BUCKETS - 1, HEADS + h], F32))
                      for h in range(HEADS)]
        _flash_bounded((0, far_pairs), score_far, pv, p_ref, acc_ref, pv_scale=far_factor)
        _flash_bounded((far_pairs, npair), score, pv, p_ref, acc_ref)

    @pl.when(jnp.logical_not(bounded_b))
    def _():
        _flash_exact(npair, score, pv, s_ref, acc_ref)

    _write_output(acc_ref, sg_ref, o_ref)


def _attention(rel_bias, bounded, qat, ka, vat, kmean, sga, qbt, qit, wit, kn, vbt, sgb):
    bsz, s, _ = kn.shape
    nb = s // BLK
    topk = min(DSA_TOPK_MAX, s // 4)
    smem = pl.BlockSpec(memory_space=pltpu.SMEM)
    qspec = pl.BlockSpec((1, BLK, WIDTH), lambda b, i: (b, i, 0))
    tile = lambda n: pl.BlockSpec((1, n, BLK), lambda b, i: (b * nb + i, 0, 0))
    row = lambda *shape: pl.BlockSpec((1,) + shape, lambda b, i: (b,) + (0,) * len(shape),
                                      pipeline_mode=pl.Buffered(1))
    out = jax.ShapeDtypeStruct((bsz, s, WIDTH), BF16)
    return pl.pallas_call(
        functools.partial(_attn_kernel, topk=topk),
        grid=(bsz, nb),
        in_specs=[smem, smem,
                  tile(WIDTH), row(s, WIDTH), row(nb, HEADS * VROWS, BLK), row(nb, WIDTH), qspec,
                  tile(WIDTH), tile(256), tile(WI_ROWS), row(s, 384), row(nb, VROWS, BLK), qspec],
        out_specs=[qspec, qspec],
        out_shape=[out, out],
        scratch_shapes=[pltpu.VMEM((3 * HEADS, BLK, BLK), F32),
                        pltpu.VMEM((3 * HEADS, BLK, BLK), F32),
                        pltpu.VMEM((HEADS, LANES, BLK), BF16),
                        pltpu.VMEM((HEADS, LANES, BLK), BF16),
                        pltpu.VMEM((HEADS, VROWS, BLK), F32),
                        pltpu.VMEM((HEADS, VROWS, BLK), F32),
                        pltpu.VMEM((HEADS, 2 * BLK, BLK), F32),
                        pltpu.VMEM((2 * HEADS, BLK, BLK), BF16),
                        pltpu.VMEM((nb, BLK, BLK), jnp.int32),
                        pltpu.VMEM((nb, KEY_BITS, 8, BLK), jnp.int32),
                        pltpu.VMEM((nb, 8, BLK), jnp.int32),
                        pltpu.VMEM((nb, BLK, BLK), F32)],
        compiler_params=pltpu.CompilerParams(
            dimension_semantics=("arbitrary", "arbitrary"), vmem_limit_bytes=VMEM_LIMIT),
        name="attention",
    )(rel_bias, bounded, qat, ka, vat.reshape(bsz, nb, HEADS * VROWS, BLK), kmean, sga,
      qbt, qit, wit, kn, vbt.reshape(bsz, nb, VROWS, BLK), sgb)


def _outproj_kernel(x_ref, ya_ref, yb_ref, sma_ref, smb_ref, wa_ref, wb_ref, wo_ref, o_ref):
    merged = (sma_ref[...] * _dot(ya_ref[...], wa_ref[...])
              + smb_ref[...] * _dot(yb_ref[...], wb_ref[...]))
    o_ref[...] = x_ref[...] + _dot(merged.astype(BF16), wo_ref[...])


def _outproj(x2, ya, yb, sma, smb, wa, wb, wo):
    m = x2.shape[0]
    row = lambda n: pl.BlockSpec((ROWS_OUT, n), lambda i: (i, 0))
    full = lambda a: pl.BlockSpec(a.shape, lambda i: (0,) * a.ndim)
    return pl.pallas_call(
        _outproj_kernel,
        grid=(m // ROWS_OUT,),
        in_specs=[row(D_MODEL), row(WIDTH), row(WIDTH), row(D_MODEL), row(D_MODEL),
                  full(wa), full(wb), full(wo)],
        out_specs=row(D_MODEL),
        out_shape=jax.ShapeDtypeStruct((m, D_MODEL), F32),
        compiler_params=pltpu.CompilerParams(
            dimension_semantics=("arbitrary",), vmem_limit_bytes=VMEM_LIMIT),
        name="outproj",
    )(x2, ya, yb, sma, smb, wa, wb, wo)


def _pack_w_in(w):
    o = 0
    parts = {}
    for name, n in (("qa", WIDTH), ("ka", WIDTH), ("va", WIDTH), ("ga", WIDTH), ("qb", WIDTH),
                    ("kb", HEAD_DIM), ("vb", HEAD_DIM), ("gb", WIDTH), ("qi", IDX_HEADS * IDX_DIM),
                    ("ki", IDX_DIM), ("wi", IDX_HEADS), ("ma", D_MODEL), ("mb", D_MODEL)):
        parts[name] = w[:, o:o + n]
        o += n
    wn = jnp.concatenate([parts["ka"], parts["kb"], parts["kb"],
                          parts["ki"], parts["ki"], parts["ki"], parts["ki"],
                          parts["ga"], parts["gb"], parts["ma"], parts["mb"]], axis=1).astype(BF16)
    pad = jnp.zeros((w.shape[0], N_TR - RT_WI - IDX_HEADS), w.dtype)
    wt = jnp.concatenate([parts["qa"], parts["qb"], parts["va"], parts["qi"], parts["vb"],
                          parts["wi"], pad], axis=1).T.astype(BF16)
    assert wn.shape[1] == N_NAT and wt.shape[0] == N_TR
    return wn, wt


def _layer(x2, bsz, s, gn, w_in, qn_a, kn_a, qn_b, kn_b, w_br_a, w_br_b, w_out, rel_bias):
    wn, wt = _pack_w_in(w_in)
    ghn = jnp.concatenate([jnp.tile(kn_a, HEADS), jnp.tile(kn_b, 2)])[None, :]
    ght = jnp.concatenate([jnp.tile(qn_a, HEADS), jnp.tile(qn_b, HEADS)])[:, None]
    (ka, kn, sga, sgb, sma, smb, kmean, qat, qbt, vat, qit, vbt, wit) = _inproj(
        x2, gn[None, :], wn, wt, ghn, ght)
    r3 = lambda a: a.reshape(bsz, s, a.shape[-1])
    bounds = jnp.stack([_logit_bound(qn_a, kn_a, rel_bias[:, :HEADS]),
                        _logit_bound(qn_b, kn_b, rel_bias[:, HEADS:])])
    ya, yb = _attention(rel_bias, (bounds <= MAX_PLAIN_LOGIT).astype(jnp.int32),
                        qat, r3(ka), vat, kmean.reshape(bsz, s // BLK, WIDTH), r3(sga),
                        qbt, qit, wit, r3(kn), vbt, r3(sgb))
    return _outproj(x2, ya.reshape(-1, WIDTH), yb.reshape(-1, WIDTH), sma, smb,
                    w_br_a.astype(BF16), w_br_b.astype(BF16), w_out.astype(BF16))


def kernel(x, norm_g, w_in, q_norm_a, k_norm_a, q_norm_b, k_norm_b,
           w_branch_a, w_branch_b, w_out, rel_bias):
    bsz, s, d = x.shape
    assert d == D_MODEL and s % BLK == 0 and s // BLK <= 31
    h = x.reshape(bsz * s, d)
    for l in range(norm_g.shape[0]):
        h = _layer(h, bsz, s, norm_g[l], w_in[l], q_norm_a[l], k_norm_a[l], q_norm_b[l],
                   k_norm_b[l], w_branch_a[l], w_branch_b[l], w_out[l], rel_bias)
    return h.reshape(bsz, s, d)
```

```python
import functools
import math

import numpy as np
import jax
import jax.numpy as jnp
from jax import lax
from jax.experimental import pallas as pl
from jax.experimental.pallas import tpu as pltpu

D_MODEL = 1024
HEADS = 8
HEAD_DIM = 64
WIDTH = HEADS * HEAD_DIM
BLK = 256
MOBA_TOPK = 3
IDX_HEADS = 4
IDX_DIM = 64
DSA_TOPK_MAX = 256
REL_BUCKETS = 32
REL_MAX_DIST = 128
EPS = 1e-6
NEG = -1e30
M_INIT = -1e20
MAX_PLAIN_LOGIT = 40.0
INT_MIN = -2 ** 31
KEY_BITS = 32
LANES = 128
VMEM_LIMIT = 56 * 1024 * 1024

F32 = jnp.float32
BF16 = jnp.bfloat16

CN_KA, CN_KB2, CN_KI4, CN_GA, CN_GB, CN_MA, CN_MB = 0, 512, 640, 896, 1408, 1920, 2944
N_NAT = 3968
RT_QA, RT_QB, RT_VA, RT_QI, RT_VB, RT_WI = 0, 512, 1024, 1536, 1792, 1856
N_TR = 1872
WI_ROWS = 8
ROWS_IN = 512
ROWS_OUT = 512
ONES_ROWS = 16
VROWS = HEAD_DIM + ONES_ROWS


def _bucket_starts():
    n = np.arange(0, 2 * REL_MAX_DIST + 2)
    max_exact = REL_BUCKETS // 2
    nf = np.maximum(n, 1).astype(np.float64)
    v = np.log(nf / max_exact) / math.log(REL_MAX_DIST / max_exact) * (REL_BUCKETS - max_exact)
    large = np.minimum(max_exact + np.trunc(v).astype(np.int64), REL_BUCKETS - 1)
    bucket = np.where(n < max_exact, n, large)
    frac = np.abs(v - np.round(v))[(n > max_exact) & (n < REL_MAX_DIST)]
    assert frac.min() > 1e-3
    return [int(np.argmax(bucket >= k)) for k in range(REL_BUCKETS)]


_STARTS = _bucket_starts()


def _dot(a, b):
    return jnp.dot(a, b, preferred_element_type=F32)


def _inproj_kernel(x_ref, gn_ref, wn_ref, wt_ref, ghn_ref, ght_ref,
                   ka_ref, kn_ref, sga_ref, sgb_ref, sma_ref, smb_ref, kmean_ref,
                   qat_ref, qbt_ref, vat_ref, qit_ref, vbt_ref, wit_ref):
    x = x_ref[...]
    r = lax.rsqrt(jnp.mean(x * x, axis=-1, keepdims=True) + EPS)
    hf = (x * r) * gn_ref[...]
    h = hf.astype(BF16)
    ht = hf.T.astype(BF16)

    gr = lax.broadcasted_iota(jnp.int32, (BLK, BLK), 0) // HEAD_DIM
    gc = lax.broadcasted_iota(jnp.int32, (BLK, BLK), 1) // HEAD_DIM
    group = jnp.where(gr == gc, 1.0, 0.0).astype(BF16)

    def proj(c0, n):
        return _dot(h, wn_ref[:, c0:c0 + n])

    def proj_t(r0, n):
        return _dot(wt_ref[r0:r0 + n, :], ht)

    def headnorm(y, c0):
        n = y.shape[-1]
        sq = y * y
        hi = sq.astype(BF16)
        lo = (sq - hi.astype(F32)).astype(BF16)
        g = group[:n, :n]
        ss = _dot(hi, g) + _dot(lo, g)
        return (y * lax.rsqrt(ss * (1.0 / HEAD_DIM) + EPS)) * ghn_ref[:, c0:c0 + n]

    def headnorm_t(yt, r0):
        ss = jnp.sum(yt * yt, axis=0, keepdims=True)
        return (yt * lax.rsqrt(ss * (1.0 / HEAD_DIM) + EPS)) * ght_ref[r0:r0 + HEAD_DIM, :]

    ka = proj(CN_KA, WIDTH)
    for c in range(0, WIDTH, BLK):
        kn = headnorm(ka[:, c:c + BLK], CN_KA + c)
        ka_ref[:, c:c + BLK] = kn.astype(BF16)
        for t in range(kn.shape[0] // BLK):
            kmean_ref[t, :, c:c + BLK] = jnp.mean(kn[t * BLK:(t + 1) * BLK], axis=0, keepdims=True)
    sga_ref[...] = jax.nn.silu(proj(CN_GA, WIDTH))
    sgb_ref[...] = jax.nn.silu(proj(CN_GB, WIDTH))
    small = proj(CN_KB2, 384)
    kn_ref[:, 0:128] = headnorm(small[:, 0:128], CN_KB2).astype(BF16)
    kn_ref[:, 128:384] = small[:, 128:384].astype(BF16)
    for c in range(0, D_MODEL, WIDTH):
        sma_ref[:, c:c + WIDTH] = jax.nn.sigmoid(proj(CN_MA + c, WIDTH))
        smb_ref[:, c:c + WIDTH] = jax.nn.sigmoid(proj(CN_MB + c, WIDTH))

    def put_t(ref, r0, val):
        for t in range(ROWS_IN // BLK):
            ref[t, r0:r0 + val.shape[0], :] = val[:, t * BLK:(t + 1) * BLK]

    scale = HEAD_DIM ** -0.5
    ones = jnp.ones((ONES_ROWS, ROWS_IN), BF16)
    qat = proj_t(RT_QA, WIDTH)
    qbt = proj_t(RT_QB, WIDTH)
    vat = proj_t(RT_VA, WIDTH)
    for hd in range(HEADS):
        r0 = hd * HEAD_DIM
        put_t(qat_ref, r0, (headnorm_t(qat[r0:r0 + HEAD_DIM], RT_QA + r0) * scale).astype(BF16))
        put_t(qbt_ref, r0, (headnorm_t(qbt[r0:r0 + HEAD_DIM], RT_QB + r0) * scale).astype(BF16))
        put_t(vat_ref, hd * VROWS, vat[r0:r0 + HEAD_DIM].astype(BF16))
        put_t(vat_ref, hd * VROWS + HEAD_DIM, ones)
    put_t(qit_ref, 0, (proj_t(RT_QI, 256) * (IDX_DIM ** -0.5)).astype(BF16))
    tail = proj_t(RT_VB, N_TR - RT_VB)
    put_t(vbt_ref, 0, tail[0:HEAD_DIM].astype(BF16))
    put_t(vbt_ref, HEAD_DIM, ones)
    put_t(wit_ref, 0, tail[RT_WI - RT_VB:RT_WI - RT_VB + WI_ROWS] * (IDX_HEADS ** -0.5))


def _inproj(x2, gn, wn, wt, ghn, ght):
    m = x2.shape[0]
    nt = m // BLK
    tpb = ROWS_IN // BLK
    row = lambda n: pl.BlockSpec((ROWS_IN, n), lambda i: (i, 0))
    full = lambda a: pl.BlockSpec(a.shape, lambda i: (0,) * a.ndim)
    tile = lambda n: pl.BlockSpec((tpb, n, BLK), lambda i: (i, 0, 0))
    out_shapes = [
        jax.ShapeDtypeStruct((m, WIDTH), BF16),
        jax.ShapeDtypeStruct((m, 384), BF16),
        jax.ShapeDtypeStruct((m, WIDTH), F32),
        jax.ShapeDtypeStruct((m, WIDTH), F32),
        jax.ShapeDtypeStruct((m, D_MODEL), F32),
        jax.ShapeDtypeStruct((m, D_MODEL), F32),
        jax.ShapeDtypeStruct((nt, 1, WIDTH), F32),
        jax.ShapeDtypeStruct((nt, WIDTH, BLK), BF16),
        jax.ShapeDtypeStruct((nt, WIDTH, BLK), BF16),
        jax.ShapeDtypeStruct((nt, HEADS * VROWS, BLK), BF16),
        jax.ShapeDtypeStruct((nt, 256, BLK), BF16),
        jax.ShapeDtypeStruct((nt, VROWS, BLK), BF16),
        jax.ShapeDtypeStruct((nt, WI_ROWS, BLK), F32),
    ]
    out_specs = [row(WIDTH), row(384), row(WIDTH), row(WIDTH), row(D_MODEL), row(D_MODEL),
                 pl.BlockSpec((tpb, 1, WIDTH), lambda i: (i, 0, 0)),
                 tile(WIDTH), tile(WIDTH), tile(HEADS * VROWS), tile(256), tile(VROWS), tile(WI_ROWS)]
    return pl.pallas_call(
        _inproj_kernel,
        grid=(nt // tpb,),
        in_specs=[row(D_MODEL), full(gn), full(wn), full(wt), full(ghn), full(ght)],
        out_specs=out_specs,
        out_shape=out_shapes,
        compiler_params=pltpu.CompilerParams(
            dimension_semantics=("arbitrary",), vmem_limit_bytes=VMEM_LIMIT),
        name="inproj",
    )(x2, gn, wn, wt, ghn, ght)


def _build_bias_tiles(rel_ref, bias_ref, head0):
    kpos = lax.broadcasted_iota(jnp.int32, (BLK, BLK), 0)
    qpos = lax.broadcasted_iota(jnp.int32, (BLK, BLK), 1)
    for kind in range(2):
        dist = qpos - kpos + BLK * kind
        for h in range(HEADS):
            t = jnp.full((BLK, BLK), rel_ref[0, head0 + h], F32)
            for k in range(1, REL_BUCKETS):
                t = jnp.where(dist >= _STARTS[k], rel_ref[k, head0 + h], t)
            if kind == 0:
                t = jnp.where(dist >= 0, t, NEG)
            bias_ref[3 * h + kind] = t
    for h in range(HEADS):
        bias_ref[3 * h + 2] = jnp.full((BLK, BLK), rel_ref[REL_BUCKETS - 1, head0 + h], F32)


def _store_masked_queries(qt_ref, qm_ref):
    rowh = lax.broadcasted_iota(jnp.int32, (LANES, BLK), 0)
    for p in range(HEADS // 2):
        q2 = qt_ref[0, LANES * p:LANES * (p + 1), :]
        for hh in range(2):
            qm_ref[2 * p + hh] = jnp.where((rowh >= HEAD_DIM) == bool(hh), q2, jnp.zeros_like(q2))


def _flash_exact(npair, score_fn, pv_fn, s_ref, acc_ref, sel_fn=None):
    def stage(t, ms):
        out = []
        for h in range(HEADS):
            m_new = ms[h]
            for u in range(2):
                j = 2 * t + u
                s = score_fn(j, h)
                s_ref[h, u * BLK:(u + 1) * BLK, :] = s
                cand = jnp.max(s, axis=0, keepdims=True)
                if sel_fn is not None:
                    cand = jnp.where(sel_fn(j, h), cand, M_INIT)
                m_new = jnp.maximum(m_new, cand)
            out.append(m_new)
        return tuple(out)

    def finish(t, ms_old, ms_new):
        for h in range(HEADS):
            acc = jnp.exp(ms_old[h] - ms_new[h]) * acc_ref[h]
            for u in range(2):
                j = 2 * t + u
                p = jnp.exp(s_ref[h, u * BLK:(u + 1) * BLK, :] - ms_new[h])
                pv = pv_fn(j, h, p.astype(BF16))
                acc = acc + (pv if sel_fn is None else jnp.where(sel_fn(j, h), pv, 0.0))
            acc_ref[h] = acc

    def step(t, ms):
        m_new = stage(t, ms)
        finish(t, ms, m_new)
        return m_new

    lax.fori_loop(0, npair, step, tuple(jnp.full((1, BLK), M_INIT, F32) for _ in range(HEADS)))


def _flash_bounded(pairs, score_fn, pv_fn, p_ref, acc_ref, sel_fn=None, side_fn=None, pv_scale=None):
    def step(t, c):
        for h in range(HEADS):
            for u in range(2):
                p_ref[2 * h + u] = jnp.exp(score_fn(2 * t + u, h)).astype(BF16)
        if side_fn is not None:
            side_fn(t)
        for h in range(HEADS):
            new = None
            for u in range(2):
                j = 2 * t + u
                pv = pv_fn(j, h, p_ref[2 * h + u])
                pv = pv if sel_fn is None else jnp.where(sel_fn(j, h), pv, 0.0)
                new = pv if new is None else new + pv
            acc_ref[h] = acc_ref[h] + (new if pv_scale is None else new * pv_scale[h])
        return c

    lax.fori_loop(pairs[0], pairs[1], step, 0)


def _logit_bound(q_gain, k_gain, bias):
    return (1.01 * HEAD_DIM ** 0.5 * jnp.max(jnp.abs(q_gain)) * jnp.max(jnp.abs(k_gain))
            + jnp.max(jnp.abs(bias)))


def _write_output(acc_ref, sg_ref, o_ref):
    def head_out(h):
        return acc_ref[h, 0:HEAD_DIM, :] / acc_ref[h, HEAD_DIM:HEAD_DIM + 1, :]

    for p in range(HEADS // 2):
        ot = jnp.concatenate([head_out(2 * p), head_out(2 * p + 1)], axis=0)
        cs = slice(LANES * p, LANES * (p + 1))
        o_ref[0, :, cs] = (ot.T * sg_ref[0, :, cs]).astype(BF16)


def _bit_planes(u):
    x = [u[8 * a:8 * a + 8, :] for a in range(KEY_BITS)]
    j, m = 16, 0x0000FFFF
    while j:
        for k in range(KEY_BITS):
            if not k & j:
                t = (x[k] ^ jnp.right_shift(x[k + j], j)) & m
                x[k] = x[k] ^ t
                x[k + j] = x[k + j] ^ jnp.left_shift(t, j)
        j >>= 1
        m ^= (m << j) & 0xFFFFFFFF
    return x


def _attn_kernel(rel_ref, bounded_ref,
                 qat_ref, ka_ref, vat_ref, kmean_ref, sga_ref,
                 qt_ref, qit_ref, wit_ref, kn_ref, vt_ref, sg_ref,
                 oa_ref, o_ref,
                 biasa_ref, bias_ref, qma_ref, qm_ref, acca_ref, acc_ref, s_ref, p_ref,
                 key_ref, plane_ref, alive_ref, madd_ref, *, topk):
    b = pl.program_id(0)
    i = pl.program_id(1)
    nblk = i + 1
    nb = key_ref.shape[0]
    bounded_a = bounded_ref[0] == 1
    bounded_b = bounded_ref[1] == 1

    @pl.when((b == 0) & (i == 0))
    def _():
        _build_bias_tiles(rel_ref, biasa_ref, 0)
        _build_bias_tiles(rel_ref, bias_ref, HEADS)
        plane_ref[...] = jnp.zeros_like(plane_ref)

    _store_masked_queries(qat_ref, qma_ref)
    _store_masked_queries(qt_ref, qm_ref)
    acca_ref[...] = jnp.zeros_like(acca_ref)
    acc_ref[...] = jnp.zeros_like(acc_ref)

    nidx = lax.broadcasted_iota(jnp.int32, (nb, BLK), 0)
    sel_bits = []
    for h in range(HEADS):
        cs = slice(LANES * (h // 2), LANES * (h // 2 + 1))
        gate = _dot(kmean_ref[0, :, cs].astype(BF16), qma_ref[h])
        g = jnp.where(nidx < i, gate, -jnp.inf)
        bits = jnp.left_shift(1, jnp.full((1, BLK), i, jnp.int32))
        for _ in range(MOBA_TOPK):
            mx = jnp.max(g, axis=0, keepdims=True)
            am = jnp.min(jnp.where(g == mx, nidx, nb), axis=0, keepdims=True)
            bits = bits | jnp.where(mx > -jnp.inf, jnp.left_shift(1, am), 0)
            g = jnp.where(nidx == am, -jnp.inf, g)
        sel_bits.append(bits)

    def score_a(j, h):
        rows = pl.ds(pl.multiple_of(j * BLK, BLK), BLK)
        cs = slice(LANES * (h // 2), LANES * (h // 2 + 1))
        return _dot(ka_ref[0, rows, cs], qma_ref[h]) + biasa_ref[3 * h + jnp.clip(i - j, 0, 2)]

    def pv_a(j, h, p):
        return _dot(vat_ref[0, j, h * VROWS:(h + 1) * VROWS, :], p)

    def sel_a(j, h):
        return (jnp.right_shift(sel_bits[h], j) & 1) == 1

    kpos = lax.broadcasted_iota(jnp.int32, (BLK, BLK), 0)
    qpos = lax.broadcasted_iota(jnp.int32, (BLK, BLK), 1)

    qit = qit_ref[0]
    wit = wit_ref[0]
    qims = [jnp.where((kpos // IDX_DIM) == h, qit, jnp.zeros_like(qit)) for h in range(IDX_HEADS)]
    whs = [wit[h:h + 1, :] for h in range(IDX_HEADS)]

    def score_block(j, causal):
        rows = pl.ds(pl.multiple_of(j * BLK, BLK), BLK)
        ki4 = kn_ref[0, rows, 128:384]
        score = jnp.zeros((BLK, BLK), F32)
        for h in range(IDX_HEADS):
            score = score + jnp.maximum(_dot(ki4, qims[h]), 0.0) * whs[h]
        bits = lax.bitcast_convert_type(score, jnp.int32)
        key = bits ^ (jnp.right_shift(bits, 31) & 0x7FFFFFFF)
        if causal:
            key = jnp.where(kpos <= qpos, key, INT_MIN)
        key_ref[j] = key
        planes = _bit_planes(key ^ INT_MIN)
        for r in range(KEY_BITS):
            plane_ref[j, r] = planes[r]

    pad = jnp.minimum(nblk, nb - 1)
    key_ref[pad] = jnp.full((BLK, BLK), INT_MIN, jnp.int32)

    def score_pair(t, c=0):
        score_block(2 * t, False)
        score_block(2 * t + 1, False)
        return c

    def score_single(_, c):
        score_block(i - 1, False)
        return c

    npair = (nblk + 1) // 2
    full_pairs = i // 2

    @pl.when(bounded_a)
    def _():
        _flash_bounded((0, full_pairs), score_a, pv_a, p_ref, acca_ref, sel_a, side_fn=score_pair)
        _flash_bounded((full_pairs, npair), score_a, pv_a, p_ref, acca_ref, sel_a)

    @pl.when(jnp.logical_not(bounded_a))
    def _():
        _flash_exact(npair, score_a, pv_a, s_ref, acca_ref, sel_a)
        lax.fori_loop(0, full_pairs, score_pair, 0)

    _write_output(acca_ref, sga_ref, oa_ref)
    lax.fori_loop(0, i % 2, score_single, 0)
    score_block(i, True)

    for j in range(nb):
        alive_ref[j] = jnp.where(j < nblk, jnp.full((8, BLK), -1, jnp.int32), 0)

    def radix_round(r, carry):
        above, thr_u = carry
        ones_here = jnp.zeros((8, BLK), jnp.int32)
        for j in range(nb):
            ones_here = ones_here + lax.population_count(alive_ref[j] & plane_ref[j, r])
        ones_here = jnp.sum(ones_here, axis=0, keepdims=True)
        take = (above + ones_here) >= topk
        flip = jnp.where(take, 0, -1)
        for j in range(nb):
            alive_ref[j] = alive_ref[j] & (plane_ref[j, r] ^ flip)
        return (jnp.where(take, above, above + ones_here),
                thr_u | jnp.where(take, jnp.left_shift(1, KEY_BITS - 1 - r), 0))

    zero = jnp.zeros((1, BLK), jnp.int32)
    above, thr_u = lax.fori_loop(0, KEY_BITS, radix_round, (zero, zero))
    thr = thr_u ^ INT_MIN

    need = jnp.where(thr > INT_MIN, topk - above, 0).astype(F32)
    lower = jnp.where(qpos <= kpos, 1.0, 0.0).astype(BF16)

    def mask_body(t, carry):
        for j in (2 * t, 2 * t + 1):
            k = key_ref[j]
            eq = k == thr
            pref = _dot(lower, jnp.where(eq, 1.0, 0.0).astype(BF16))
            sel = (k > thr) | (eq & ((carry + pref) <= need))
            madd_ref[j] = jnp.where(sel, 0.0, NEG)
            carry = carry + pref[BLK - 1:BLK, :]
        return carry

    lax.fori_loop(0, npair, mask_body, jnp.zeros((1, BLK), F32))

    def score(j, h):
        rows = pl.ds(pl.multiple_of(j * BLK, BLK), BLK)
        return (_dot(kn_ref[0, rows, 0:128], qm_ref[h])
                + bias_ref[3 * h + jnp.clip(i - j, 0, 2)] + madd_ref[j])

    def pv(j, h, p):
        return _dot(vt_ref[0, j], p)

    far_pairs = jnp.maximum(i - 1, 0) // 2

    def score_far(j, h):
        rows = pl.ds(pl.multiple_of(j * BLK, BLK), BLK)
        return _dot(kn_ref[0, rows, 0:128], qm_ref[h]) + madd_ref[j]

    @pl.when(bounded_b)
    def _():
        far_factor = [jnp.exp(jnp.full((1, BLK), rel_ref[REL_BUCKETS - 1, HEADS + h], F32))
                      for h in range(HEADS)]
        _flash_bounded((0, far_pairs), score_far, pv, p_ref, acc_ref, pv_scale=far_factor)
        _flash_bounded((far_pairs, npair), score, pv, p_ref, acc_ref)

    @pl.when(jnp.logical_not(bounded_b))
    def _():
        _flash_exact(npair, score, pv, s_ref, acc_ref)

    _write_output(acc_ref, sg_ref, o_ref)


def _attention(rel_bias, bounded, qat, ka, vat, kmean, sga, qbt, qit, wit, kn, vbt, sgb):
    bsz, s, _ = kn.shape
    nb = s // BLK
    topk = min(DSA_TOPK_MAX, s // 4)
    smem = pl.BlockSpec(memory_space=pltpu.SMEM)
    qspec = pl.BlockSpec((1, BLK, WIDTH), lambda b, i: (b, i, 0))
    tile = lambda n: pl.BlockSpec((1, n, BLK), lambda b, i: (b * nb + i, 0, 0))
    row = lambda *shape: pl.BlockSpec((1,) + shape, lambda b, i: (b,) + (0,) * len(shape),
                                      pipeline_mode=pl.Buffered(1))
    out = jax.ShapeDtypeStruct((bsz, s, WIDTH), BF16)
    return pl.pallas_call(
        functools.partial(_attn_kernel, topk=topk),
        grid=(bsz, nb),
        in_specs=[smem, smem,
                  tile(WIDTH), row(s, WIDTH), row(nb, HEADS * VROWS, BLK), row(nb, WIDTH), qspec,
                  tile(WIDTH), tile(256), tile(WI_ROWS), row(s, 384), row(nb, VROWS, BLK), qspec],
        out_specs=[qspec, qspec],
        out_shape=[out, out],
        scratch_shapes=[pltpu.VMEM((3 * HEADS, BLK, BLK), F32),
                        pltpu.VMEM((3 * HEADS, BLK, BLK), F32),
                        pltpu.VMEM((HEADS, LANES, BLK), BF16),
                        pltpu.VMEM((HEADS, LANES, BLK), BF16),
                        pltpu.VMEM((HEADS, VROWS, BLK), F32),
                        pltpu.VMEM((HEADS, VROWS, BLK), F32),
                        pltpu.VMEM((HEADS, 2 * BLK, BLK), F32),
                        pltpu.VMEM((2 * HEADS, BLK, BLK), BF16),
                        pltpu.VMEM((nb, BLK, BLK), jnp.int32),
                        pltpu.VMEM((nb, KEY_BITS, 8, BLK), jnp.int32),
                        pltpu.VMEM((nb, 8, BLK), jnp.int32),
                        pltpu.VMEM((nb, BLK, BLK), F32)],
        compiler_params=pltpu.CompilerParams(
            dimension_semantics=("arbitrary", "arbitrary"), vmem_limit_bytes=VMEM_LIMIT),
        name="attention",
    )(rel_bias, bounded, qat, ka, vat.reshape(bsz, nb, HEADS * VROWS, BLK), kmean, sga,
      qbt, qit, wit, kn, vbt.reshape(bsz, nb, VROWS, BLK), sgb)


def _outproj_kernel(x_ref, ya_ref, yb_ref, sma_ref, smb_ref, wa_ref, wb_ref, wo_ref, o_ref):
    merged = (sma_ref[...] * _dot(ya_ref[...], wa_ref[...])
              + smb_ref[...] * _dot(yb_ref[...], wb_ref[...]))
    o_ref[...] = x_ref[...] + _dot(merged.astype(BF16), wo_ref[...])


def _outproj(x2, ya, yb, sma, smb, wa, wb, wo):
    m = x2.shape[0]
    row = lambda n: pl.BlockSpec((ROWS_OUT, n), lambda i: (i, 0))
    full = lambda a: pl.BlockSpec(a.shape, lambda i: (0,) * a.ndim)
    return pl.pallas_call(
        _outproj_kernel,
        grid=(m // ROWS_OUT,),
        in_specs=[row(D_MODEL), row(WIDTH), row(WIDTH), row(D_MODEL), row(D_MODEL),
                  full(wa), full(wb), full(wo)],
        out_specs=row(D_MODEL),
        out_shape=jax.ShapeDtypeStruct((m, D_MODEL), F32),
        compiler_params=pltpu.CompilerParams(
            dimension_semantics=("arbitrary",), vmem_limit_bytes=VMEM_LIMIT),
        name="outproj",
    )(x2, ya, yb, sma, smb, wa, wb, wo)


def _pack_w_in(w):
    o = 0
    parts = {}
    for name, n in (("qa", WIDTH), ("ka", WIDTH), ("va", WIDTH), ("ga", WIDTH), ("qb", WIDTH),
                    ("kb", HEAD_DIM), ("vb", HEAD_DIM), ("gb", WIDTH), ("qi", IDX_HEADS * IDX_DIM),
                    ("ki", IDX_DIM), ("wi", IDX_HEADS), ("ma", D_MODEL), ("mb", D_MODEL)):
        parts[name] = w[:, o:o + n]
        o += n
    wn = jnp.concatenate([parts["ka"], parts["kb"], parts["kb"],
                          parts["ki"], parts["ki"], parts["ki"], parts["ki"],
                          parts["ga"], parts["gb"], parts["ma"], parts["mb"]], axis=1).astype(BF16)
    pad = jnp.zeros((w.shape[0], N_TR - RT_WI - IDX_HEADS), w.dtype)
    wt = jnp.concatenate([parts["qa"], parts["qb"], parts["va"], parts["qi"], parts["vb"],
                          parts["wi"], pad], axis=1).T.astype(BF16)
    assert wn.shape[1] == N_NAT and wt.shape[0] == N_TR
    return wn, wt


def _layer(x2, bsz, s, gn, w_in, qn_a, kn_a, qn_b, kn_b, w_br_a, w_br_b, w_out, rel_bias):
    wn, wt = _pack_w_in(w_in)
    ghn = jnp.concatenate([jnp.tile(kn_a, HEADS), jnp.tile(kn_b, 2)])[None, :]
    ght = jnp.concatenate([jnp.tile(qn_a, HEADS), jnp.tile(qn_b, HEADS)])[:, None]
    (ka, kn, sga, sgb, sma, smb, kmean, qat, qbt, vat, qit, vbt, wit) = _inproj(
        x2, gn[None, :], wn, wt, ghn, ght)
    r3 = lambda a: a.reshape(bsz, s, a.shape[-1])
    bounds = jnp.stack([_logit_bound(qn_a, kn_a, rel_bias[:, :HEADS]),
                        _logit_bound(qn_b, kn_b, rel_bias[:, HEADS:])])
    ya, yb = _attention(rel_bias, (bounds <= MAX_PLAIN_LOGIT).astype(jnp.int32),
                        qat, r3(ka), vat, kmean.reshape(bsz, s // BLK, WIDTH), r3(sga),
                        qbt, qit, wit, r3(kn), vbt, r3(sgb))
    return _outproj(x2, ya.reshape(-1, WIDTH), yb.reshape(-1, WIDTH), sma, smb,
                    w_br_a.astype(BF16), w_br_b.astype(BF16), w_out.astype(BF16))


def kernel(x, norm_g, w_in, q_norm_a, k_norm_a, q_norm_b, k_norm_b,
           w_branch_a, w_branch_b, w_out, rel_bias):
    bsz, s, d = x.shape
    assert d == D_MODEL and s % BLK == 0 and s // BLK <= 31
    h = x.reshape(bsz * s, d)
    for l in range(norm_g.shape[0]):
        h = _layer(h, bsz, s, norm_g[l], w_in[l], q_norm_a[l], k_norm_a[l], q_norm_b[l],
                   k_norm_b[l], w_branch_a[l], w_branch_b[l], w_out[l], rel_bias)
    return h.reshape(bsz, s, d)
```

```python
import functools
import math

import numpy as np
import jax
import jax.numpy as jnp
from jax import lax
from jax.experimental import pallas as pl
from jax.experimental.pallas import tpu as pltpu

D_MODEL = 1024
HEADS = 8
HEAD_DIM = 64
WIDTH = HEADS * HEAD_DIM
BLK = 256
MOBA_TOPK = 3
IDX_HEADS = 4
IDX_DIM = 64
DSA_TOPK_MAX = 256
REL_BUCKETS = 32
REL_MAX_DIST = 128
EPS = 1e-6
NEG = -1e30
M_INIT = -1e20
MAX_PLAIN_LOGIT = 40.0
INT_MIN = -2 ** 31
KEY_BITS = 32
LANES = 128
VMEM_LIMIT = 56 * 1024 * 1024

F32 = jnp.float32
BF16 = jnp.bfloat16

CN_KA, CN_KB2, CN_KI4, CN_GA, CN_GB, CN_MA, CN_MB = 0, 512, 640, 896, 1408, 1920, 2944
N_NAT = 3968
RT_QA, RT_QB, RT_VA, RT_QI, RT_VB, RT_WI = 0, 512, 1024, 1536, 1792, 1856
N_TR = 1872
WI_ROWS = 8
ROWS_IN = 512
ROWS_OUT = 1024
ONES_ROWS = 16
VROWS = HEAD_DIM + ONES_ROWS


def _bucket_starts():
    n = np.arange(0, 2 * REL_MAX_DIST + 2)
    max_exact = REL_BUCKETS // 2
    nf = np.maximum(n, 1).astype(np.float64)
    v = np.log(nf / max_exact) / math.log(REL_MAX_DIST / max_exact) * (REL_BUCKETS - max_exact)
    large = np.minimum(max_exact + np.trunc(v).astype(np.int64), REL_BUCKETS - 1)
    bucket = np.where(n < max_exact, n, large)
    frac = np.abs(v - np.round(v))[(n > max_exact) & (n < REL_MAX_DIST)]
    assert frac.min() > 1e-3
    return [int(np.argmax(bucket >= k)) for k in range(REL_BUCKETS)]


_STARTS = _bucket_starts()


def _dot(a, b):
    return jnp.dot(a, b, preferred_element_type=F32)


def _inproj_kernel(x_ref, gn_ref, wn_ref, wt_ref, ghn_ref, ght_ref,
                   ka_ref, kn_ref, sga_ref, sgb_ref, sma_ref, smb_ref, kmean_ref,
                   qat_ref, qbt_ref, vat_ref, qit_ref, vbt_ref, wit_ref):
    x = x_ref[...]
    r = lax.rsqrt(jnp.mean(x * x, axis=-1, keepdims=True) + EPS)
    hf = (x * r) * gn_ref[...]
    h = hf.astype(BF16)
    ht = hf.T.astype(BF16)

    gr = lax.broadcasted_iota(jnp.int32, (BLK, BLK), 0) // HEAD_DIM
    gc = lax.broadcasted_iota(jnp.int32, (BLK, BLK), 1) // HEAD_DIM
    group = jnp.where(gr == gc, 1.0, 0.0).astype(BF16)

    def proj(c0, n):
        return _dot(h, wn_ref[:, c0:c0 + n])

    def proj_t(r0, n):
        return _dot(wt_ref[r0:r0 + n, :], ht)

    def headnorm(y, c0):
        n = y.shape[-1]
        sq = y * y
        hi = sq.astype(BF16)
        lo = (sq - hi.astype(F32)).astype(BF16)
        g = group[:n, :n]
        ss = _dot(hi, g) + _dot(lo, g)
        return (y * lax.rsqrt(ss * (1.0 / HEAD_DIM) + EPS)) * ghn_ref[:, c0:c0 + n]

    def headnorm_t(yt, r0):
        ss = jnp.sum(yt * yt, axis=0, keepdims=True)
        return (yt * lax.rsqrt(ss * (1.0 / HEAD_DIM) + EPS)) * ght_ref[r0:r0 + HEAD_DIM, :]

    ka = proj(CN_KA, WIDTH)
    for c in range(0, WIDTH, BLK):
        kn = headnorm(ka[:, c:c + BLK], CN_KA + c)
        ka_ref[:, c:c + BLK] = kn.astype(BF16)
        for t in range(kn.shape[0] // BLK):
            kmean_ref[t, :, c:c + BLK] = jnp.mean(kn[t * BLK:(t + 1) * BLK], axis=0, keepdims=True)
    sga_ref[...] = jax.nn.silu(proj(CN_GA, WIDTH))
    sgb_ref[...] = jax.nn.silu(proj(CN_GB, WIDTH))
    small = proj(CN_KB2, 384)
    kn_ref[:, 0:128] = headnorm(small[:, 0:128], CN_KB2).astype(BF16)
    kn_ref[:, 128:384] = small[:, 128:384].astype(BF16)
    for c in range(0, D_MODEL, WIDTH):
        sma_ref[:, c:c + WIDTH] = jax.nn.sigmoid(proj(CN_MA + c, WIDTH))
        smb_ref[:, c:c + WIDTH] = jax.nn.sigmoid(proj(CN_MB + c, WIDTH))

    def put_t(ref, r0, val):
        for t in range(ROWS_IN // BLK):
            ref[t, r0:r0 + val.shape[0], :] = val[:, t * BLK:(t + 1) * BLK]

    scale = HEAD_DIM ** -0.5
    ones = jnp.ones((ONES_ROWS, ROWS_IN), BF16)
    qat = proj_t(RT_QA, WIDTH)
    qbt = proj_t(RT_QB, WIDTH)
    vat = proj_t(RT_VA, WIDTH)
    for hd in range(HEADS):
        r0 = hd * HEAD_DIM
        put_t(qat_ref, r0, (headnorm_t(qat[r0:r0 + HEAD_DIM], RT_QA + r0) * scale).astype(BF16))
        put_t(qbt_ref, r0, (headnorm_t(qbt[r0:r0 + HEAD_DIM], RT_QB + r0) * scale).astype(BF16))
        put_t(vat_ref, hd * VROWS, vat[r0:r0 + HEAD_DIM].astype(BF16))
        put_t(vat_ref, hd * VROWS + HEAD_DIM, ones)
    put_t(qit_ref, 0, (proj_t(RT_QI, 256) * (IDX_DIM ** -0.5)).astype(BF16))
    tail = proj_t(RT_VB, N_TR - RT_VB)
    put_t(vbt_ref, 0, tail[0:HEAD_DIM].astype(BF16))
    put_t(vbt_ref, HEAD_DIM, ones)
    put_t(wit_ref, 0, tail[RT_WI - RT_VB:RT_WI - RT_VB + WI_ROWS] * (IDX_HEADS ** -0.5))


def _inproj(x2, gn, wn, wt, ghn, ght):
    m = x2.shape[0]
    nt = m // BLK
    tpb = ROWS_IN // BLK
    row = lambda n: pl.BlockSpec((ROWS_IN, n), lambda i: (i, 0))
    full = lambda a: pl.BlockSpec(a.shape, lambda i: (0,) * a.ndim)
    tile = lambda n: pl.BlockSpec((tpb, n, BLK), lambda i: (i, 0, 0))
    out_shapes = [
        jax.ShapeDtypeStruct((m, WIDTH), BF16),
        jax.ShapeDtypeStruct((m, 384), BF16),
        jax.ShapeDtypeStruct((m, WIDTH), F32),
        jax.ShapeDtypeStruct((m, WIDTH), F32),
        jax.ShapeDtypeStruct((m, D_MODEL), F32),
        jax.ShapeDtypeStruct((m, D_MODEL), F32),
        jax.ShapeDtypeStruct((nt, 1, WIDTH), F32),
        jax.ShapeDtypeStruct((nt, WIDTH, BLK), BF16),
        jax.ShapeDtypeStruct((nt, WIDTH, BLK), BF16),
        jax.ShapeDtypeStruct((nt, HEADS * VROWS, BLK), BF16),
        jax.ShapeDtypeStruct((nt, 256, BLK), BF16),
        jax.ShapeDtypeStruct((nt, VROWS, BLK), BF16),
        jax.ShapeDtypeStruct((nt, WI_ROWS, BLK), F32),
    ]
    out_specs = [row(WIDTH), row(384), row(WIDTH), row(WIDTH), row(D_MODEL), row(D_MODEL),
                 pl.BlockSpec((tpb, 1, WIDTH), lambda i: (i, 0, 0)),
                 tile(WIDTH), tile(WIDTH), tile(HEADS * VROWS), tile(256), tile(VROWS), tile(WI_ROWS)]
    return pl.pallas_call(
        _inproj_kernel,
        grid=(nt // tpb,),
        in_specs=[row(D_MODEL), full(gn), full(wn), full(wt), full(ghn), full(ght)],
        out_specs=out_specs,
        out_shape=out_shapes,
        compiler_params=pltpu.CompilerParams(
            dimension_semantics=("arbitrary",), vmem_limit_bytes=VMEM_LIMIT),
        name="inproj",
    )(x2, gn, wn, wt, ghn, ght)


def _build_bias_tiles(rel_ref, bias_ref, head0):
    kpos = lax.broadcasted_iota(jnp.int32, (BLK, BLK), 0)
    qpos = lax.broadcasted_iota(jnp.int32, (BLK, BLK), 1)
    for kind in range(2):
        dist = qpos - kpos + BLK * kind
        for h in range(HEADS):
            t = jnp.full((BLK, BLK), rel_ref[0, head0 + h], F32)
            for k in range(1, REL_BUCKETS):
                t = jnp.where(dist >= _STARTS[k], rel_ref[k, head0 + h], t)
            if kind == 0:
                t = jnp.where(dist >= 0, t, NEG)
            bias_ref[3 * h + kind] = t
    for h in range(HEADS):
        bias_ref[3 * h + 2] = jnp.full((BLK, BLK), rel_ref[REL_BUCKETS - 1, head0 + h], F32)


def _store_masked_queries(qt_ref, qm_ref):
    rowh = lax.broadcasted_iota(jnp.int32, (LANES, BLK), 0)
    for p in range(HEADS // 2):
        q2 = qt_ref[0, LANES * p:LANES * (p + 1), :]
        for hh in range(2):
            qm_ref[2 * p + hh] = jnp.where((rowh >= HEAD_DIM) == bool(hh), q2, jnp.zeros_like(q2))


def _flash_exact(npair, score_fn, pv_fn, s_ref, acc_ref, sel_fn=None):
    def stage(t, ms):
        out = []
        for h in range(HEADS):
            m_new = ms[h]
            for u in range(2):
                j = 2 * t + u
                s = score_fn(j, h)
                s_ref[h, u * BLK:(u + 1) * BLK, :] = s
                cand = jnp.max(s, axis=0, keepdims=True)
                if sel_fn is not None:
                    cand = jnp.where(sel_fn(j, h), cand, M_INIT)
                m_new = jnp.maximum(m_new, cand)
            out.append(m_new)
        return tuple(out)

    def finish(t, ms_old, ms_new):
        for h in range(HEADS):
            acc = jnp.exp(ms_old[h] - ms_new[h]) * acc_ref[h]
            for u in range(2):
                j = 2 * t + u
                p = jnp.exp(s_ref[h, u * BLK:(u + 1) * BLK, :] - ms_new[h])
                pv = pv_fn(j, h, p.astype(BF16))
                acc = acc + (pv if sel_fn is None else jnp.where(sel_fn(j, h), pv, 0.0))
            acc_ref[h] = acc

    def step(t, ms):
        m_new = stage(t, ms)
        finish(t, ms, m_new)
        return m_new

    lax.fori_loop(0, npair, step, tuple(jnp.full((1, BLK), M_INIT, F32) for _ in range(HEADS)))


def _flash_bounded(pairs, score_fn, pv_fn, p_ref, acc_ref, sel_fn=None, side_fn=None, pv_scale=None):
    def step(t, c):
        for h in range(HEADS):
            for u in range(2):
                p_ref[2 * h + u] = jnp.exp(score_fn(2 * t + u, h)).astype(BF16)
        if side_fn is not None:
            side_fn(t)
        for h in range(HEADS):
            new = None
            for u in range(2):
                j = 2 * t + u
                pv = pv_fn(j, h, p_ref[2 * h + u])
                pv = pv if sel_fn is None else jnp.where(sel_fn(j, h), pv, 0.0)
                new = pv if new is None else new + pv
            acc_ref[h] = acc_ref[h] + (new if pv_scale is None else new * pv_scale[h])
        return c

    lax.fori_loop(pairs[0], pairs[1], step, 0)


def _logit_bound(q_gain, k_gain, bias):
    return (1.01 * HEAD_DIM ** 0.5 * jnp.max(jnp.abs(q_gain)) * jnp.max(jnp.abs(k_gain))
            + jnp.max(jnp.abs(bias)))


def _write_output(acc_ref, sg_ref, o_ref):
    def head_out(h):
        return acc_ref[h, 0:HEAD_DIM, :] / acc_ref[h, HEAD_DIM:HEAD_DIM + 1, :]

    for p in range(HEADS // 2):
        ot = jnp.concatenate([head_out(2 * p), head_out(2 * p + 1)], axis=0)
        cs = slice(LANES * p, LANES * (p + 1))
        o_ref[0, :, cs] = (ot.T * sg_ref[0, :, cs]).astype(BF16)


def _bit_planes(u):
    x = [u[8 * a:8 * a + 8, :] for a in range(KEY_BITS)]
    j, m = 16, 0x0000FFFF
    while j:
        for k in range(KEY_BITS):
            if not k & j:
                t = (x[k] ^ jnp.right_shift(x[k + j], j)) & m
                x[k] = x[k] ^ t
                x[k + j] = x[k + j] ^ jnp.left_shift(t, j)
        j >>= 1
        m ^= (m << j) & 0xFFFFFFFF
    return x


def _attn_kernel(rel_ref, bounded_ref,
                 qat_ref, ka_ref, vat_ref, kmean_ref, sga_ref,
                 qt_ref, qit_ref, wit_ref, kn_ref, vt_ref, sg_ref,
                 oa_ref, o_ref,
                 biasa_ref, bias_ref, qma_ref, qm_ref, acca_ref, acc_ref, s_ref, p_ref,
                 key_ref, plane_ref, alive_ref, madd_ref, *, topk):
    b = pl.program_id(0)
    i = pl.program_id(1)
    nblk = i + 1
    nb = key_ref.shape[0]
    bounded_a = bounded_ref[0] == 1
    bounded_b = bounded_ref[1] == 1

    @pl.when((b == 0) & (i == 0))
    def _():
        _build_bias_tiles(rel_ref, biasa_ref, 0)
        _build_bias_tiles(rel_ref, bias_ref, HEADS)
        plane_ref[...] = jnp.zeros_like(plane_ref)

    _store_masked_queries(qat_ref, qma_ref)
    _store_masked_queries(qt_ref, qm_ref)
    acca_ref[...] = jnp.zeros_like(acca_ref)
    acc_ref[...] = jnp.zeros_like(acc_ref)

    nidx = lax.broadcasted_iota(jnp.int32, (nb, BLK), 0)
    sel_bits = []
    for h in range(HEADS):
        cs = slice(LANES * (h // 2), LANES * (h // 2 + 1))
        gate = _dot(kmean_ref[0, :, cs].astype(BF16), qma_ref[h])
        g = jnp.where(nidx < i, gate, -jnp.inf)
        bits = jnp.left_shift(1, jnp.full((1, BLK), i, jnp.int32))
        for _ in range(MOBA_TOPK):
            mx = jnp.max(g, axis=0, keepdims=True)
            am = jnp.min(jnp.where(g == mx, nidx, nb), axis=0, keepdims=True)
            bits = bits | jnp.where(mx > -jnp.inf, jnp.left_shift(1, am), 0)
            g = jnp.where(nidx == am, -jnp.inf, g)
        sel_bits.append(bits)

    def score_a(j, h):
        rows = pl.ds(pl.multiple_of(j * BLK, BLK), BLK)
        cs = slice(LANES * (h // 2), LANES * (h // 2 + 1))
        return _dot(ka_ref[0, rows, cs], qma_ref[h]) + biasa_ref[3 * h + jnp.clip(i - j, 0, 2)]

    def pv_a(j, h, p):
        return _dot(vat_ref[0, j, h * VROWS:(h + 1) * VROWS, :], p)

    def sel_a(j, h):
        return (jnp.right_shift(sel_bits[h], j) & 1) == 1

    kpos = lax.broadcasted_iota(jnp.int32, (BLK, BLK), 0)
    qpos = lax.broadcasted_iota(jnp.int32, (BLK, BLK), 1)

    qit = qit_ref[0]
    wit = wit_ref[0]
    qims = [jnp.where((kpos // IDX_DIM) == h, qit, jnp.zeros_like(qit)) for h in range(IDX_HEADS)]
    whs = [wit[h:h + 1, :] for h in range(IDX_HEADS)]

    def score_block(j, causal):
        rows = pl.ds(pl.multiple_of(j * BLK, BLK), BLK)
        ki4 = kn_ref[0, rows, 128:384]
        score = jnp.zeros((BLK, BLK), F32)
        for h in range(IDX_HEADS):
            score = score + jnp.maximum(_dot(ki4, qims[h]), 0.0) * whs[h]
        bits = lax.bitcast_convert_type(score, jnp.int32)
        key = bits ^ (jnp.right_shift(bits, 31) & 0x7FFFFFFF)
        if causal:
            key = jnp.where(kpos <= qpos, key, INT_MIN)
        key_ref[j] = key
        planes = _bit_planes(key ^ INT_MIN)
        for r in range(KEY_BITS):
            plane_ref[j, r] = planes[r]

    pad = jnp.minimum(nblk, nb - 1)
    key_ref[pad] = jnp.full((BLK, BLK), INT_MIN, jnp.int32)

    def score_pair(t, c=0):
        score_block(2 * t, False)
        score_block(2 * t + 1, False)
        return c

    def score_single(_, c):
        score_block(i - 1, False)
        return c

    npair = (nblk + 1) // 2
    full_pairs = i // 2

    @pl.when(bounded_a)
    def _():
        _flash_bounded((0, full_pairs), score_a, pv_a, p_ref, acca_ref, sel_a, side_fn=score_pair)
        _flash_bounded((full_pairs, npair), score_a, pv_a, p_ref, acca_ref, sel_a)

    @pl.when(jnp.logical_not(bounded_a))
    def _():
        _flash_exact(npair, score_a, pv_a, s_ref, acca_ref, sel_a)
        lax.fori_loop(0, full_pairs, score_pair, 0)

    _write_output(acca_ref, sga_ref, oa_ref)
    lax.fori_loop(0, i % 2, score_single, 0)
    score_block(i, True)

    for j in range(nb):
        alive_ref[j] = jnp.where(j < nblk, jnp.full((8, BLK), -1, jnp.int32), 0)

    def radix_round(tiles, r, carry):
        above, thr_u = carry
        ones_here = jnp.zeros((8, BLK), jnp.int32)
        for j in range(tiles):
            ones_here = ones_here + lax.population_count(alive_ref[j] & plane_ref[j, r])
        ones_here = jnp.sum(ones_here, axis=0, keepdims=True)
        take = (above + ones_here) >= topk
        flip = jnp.where(take, 0, -1)
        for j in range(tiles):
            alive_ref[j] = alive_ref[j] & (plane_ref[j, r] ^ flip)
        return (jnp.where(take, above, above + ones_here),
                thr_u | jnp.where(take, jnp.left_shift(1, KEY_BITS - 1 - r), 0))

    def radix_select(tiles):
        zero = jnp.zeros((1, BLK), jnp.int32)
        return lambda: lax.fori_loop(0, KEY_BITS, functools.partial(radix_round, tiles), (zero, zero))

    above, thr_u = lax.cond(nblk <= nb // 2, radix_select(nb // 2), radix_select(nb))
    thr = thr_u ^ INT_MIN

    need = jnp.where(thr > INT_MIN, topk - above, 0).astype(F32)
    lower = jnp.where(qpos <= kpos, 1.0, 0.0).astype(BF16)

    def mask_body(t, carry):
        for j in (2 * t, 2 * t + 1):
            k = key_ref[j]
            eq = k == thr
            pref = _dot(lower, jnp.where(eq, 1.0, 0.0).astype(BF16))
            sel = (k > thr) | (eq & ((carry + pref) <= need))
            madd_ref[j] = jnp.where(sel, 0.0, NEG)
            carry = carry + pref[BLK - 1:BLK, :]
        return carry

    lax.fori_loop(0, npair, mask_body, jnp.zeros((1, BLK), F32))

    def score(j, h):
        rows = pl.ds(pl.multiple_of(j * BLK, BLK), BLK)
        return (_dot(kn_ref[0, rows, 0:128], qm_ref[h])
                + bias_ref[3 * h + jnp.clip(i - j, 0, 2)] + madd_ref[j])

    def pv(j, h, p):
        return _dot(vt_ref[0, j], p)

    far_pairs = jnp.maximum(i - 1, 0) // 2

    def score_far(j, h):
        rows = pl.ds(pl.multiple_of(j * BLK, BLK), BLK)
        return _dot(kn_ref[0, rows, 0:128], qm_ref[h]) + madd_ref[j]

    @pl.when(bounded_b)
    def _():
        far_factor = [jnp.exp(jnp.full((1, BLK), rel_ref[REL_BUCKETS - 1, HEADS + h], F32))
                      for h in range(HEADS)]
        _flash_bounded((0, far_pairs), score_far, pv, p_ref, acc_ref, pv_scale=far_factor)
        _flash_bounded((far_pairs, npair), score, pv, p_ref, acc_ref)

    @pl.when(jnp.logical_not(bounded_b))
    def _():
        _flash_exact(npair, score, pv, s_ref, acc_ref)

    _write_output(acc_ref, sg_ref, o_ref)


def _attention(rel_bias, bounded, qat, ka, vat, kmean, sga, qbt, qit, wit, kn, vbt, sgb):
    bsz, s, _ = kn.shape
    nb = s // BLK
    topk = min(DSA_TOPK_MAX, s // 4)
    smem = pl.BlockSpec(memory_space=pltpu.SMEM)
    qspec = pl.BlockSpec((1, BLK, WIDTH), lambda b, i: (b, i, 0))
    tile = lambda n: pl.BlockSpec((1, n, BLK), lambda b, i: (b * nb + i, 0, 0))
    row = lambda *shape: pl.BlockSpec((1,) + shape, lambda b, i: (b,) + (0,) * len(shape),
                                      pipeline_mode=pl.Buffered(1))
    out = jax.ShapeDtypeStruct((bsz, s, WIDTH), BF16)
    return pl.pallas_call(
        functools.partial(_attn_kernel, topk=topk),
        grid=(bsz, nb),
        in_specs=[smem, smem,
                  tile(WIDTH), row(s, WIDTH), row(nb, HEADS * VROWS, BLK), row(nb, WIDTH), qspec,
                  tile(WIDTH), tile(256), tile(WI_ROWS), row(s, 384), row(nb, VROWS, BLK), qspec],
        out_specs=[qspec, qspec],
        out_shape=[out, out],
        scratch_shapes=[pltpu.VMEM((3 * HEADS, BLK, BLK), F32),
                        pltpu.VMEM((3 * HEADS, BLK, BLK), F32),
                        pltpu.VMEM((HEADS, LANES, BLK), BF16),
                        pltpu.VMEM((HEADS, LANES, BLK), BF16),
                        pltpu.VMEM((HEADS, VROWS, BLK), F32),
                        pltpu.VMEM((HEADS, VROWS, BLK), F32),
                        pltpu.VMEM((HEADS, 2 * BLK, BLK), F32),
                        pltpu.VMEM((2 * HEADS, BLK, BLK), BF16),
                        pltpu.VMEM((nb, BLK, BLK), jnp.int32),
                        pltpu.VMEM((nb, KEY_BITS, 8, BLK), jnp.int32),
                        pltpu.VMEM((nb, 8, BLK), jnp.int32),
                        pltpu.VMEM((nb, BLK, BLK), F32)],
        compiler_params=pltpu.CompilerParams(
            dimension_semantics=("arbitrary", "arbitrary"), vmem_limit_bytes=VMEM_LIMIT),
        name="attention",
    )(rel_bias, bounded, qat, ka, vat.reshape(bsz, nb, HEADS * VROWS, BLK), kmean, sga,
      qbt, qit, wit, kn, vbt.reshape(bsz, nb, VROWS, BLK), sgb)


def _outproj_kernel(x_ref, ya_ref, yb_ref, sma_ref, smb_ref, wa_ref, wb_ref, wo_ref, o_ref):
    merged = (sma_ref[...] * _dot(ya_ref[...], wa_ref[...])
              + smb_ref[...] * _dot(yb_ref[...], wb_ref[...]))
    o_ref[...] = x_ref[...] + _dot(merged.astype(BF16), wo_ref[...])


def _outproj(x2, ya, yb, sma, smb, wa, wb, wo):
    m = x2.shape[0]
    row = lambda n: pl.BlockSpec((ROWS_OUT, n), lambda i: (i, 0))
    full = lambda a: pl.BlockSpec(a.shape, lambda i: (0,) * a.ndim)
    return pl.pallas_call(
        _outproj_kernel,
        grid=(m // ROWS_OUT,),
        in_specs=[row(D_MODEL), row(WIDTH), row(WIDTH), row(D_MODEL), row(D_MODEL),
                  full(wa), full(wb), full(wo)],
        out_specs=row(D_MODEL),
        out_shape=jax.ShapeDtypeStruct((m, D_MODEL), F32),
        compiler_params=pltpu.CompilerParams(
            dimension_semantics=("arbitrary",), vmem_limit_bytes=VMEM_LIMIT),
        name="outproj",
    )(x2, ya, yb, sma, smb, wa, wb, wo)


def _pack_w_in(w):
    o = 0
    parts = {}
    for name, n in (("qa", WIDTH), ("ka", WIDTH), ("va", WIDTH), ("ga", WIDTH), ("qb", WIDTH),
                    ("kb", HEAD_DIM), ("vb", HEAD_DIM), ("gb", WIDTH), ("qi", IDX_HEADS * IDX_DIM),
                    ("ki", IDX_DIM), ("wi", IDX_HEADS), ("ma", D_MODEL), ("mb", D_MODEL)):
        parts[name] = w[:, o:o + n]
        o += n
    wn = jnp.concatenate([parts["ka"], parts["kb"], parts["kb"],
                          parts["ki"], parts["ki"], parts["ki"], parts["ki"],
                          parts["ga"], parts["gb"], parts["ma"], parts["mb"]], axis=1).astype(BF16)
    pad = jnp.zeros((w.shape[0], N_TR - RT_WI - IDX_HEADS), w.dtype)
    wt = jnp.concatenate([parts["qa"], parts["qb"], parts["va"], parts["qi"], parts["vb"],
                          parts["wi"], pad], axis=1).T.astype(BF16)
    assert wn.shape[1] == N_NAT and wt.shape[0] == N_TR
    return wn, wt


def _layer(x2, bsz, s, gn, w_in, qn_a, kn_a, qn_b, kn_b, w_br_a, w_br_b, w_out, rel_bias):
    wn, wt = _pack_w_in(w_in)
    ghn = jnp.concatenate([jnp.tile(kn_a, HEADS), jnp.tile(kn_b, 2)])[None, :]
    ght = jnp.concatenate([jnp.tile(qn_a, HEADS), jnp.tile(qn_b, HEADS)])[:, None]
    (ka, kn, sga, sgb, sma, smb, kmean, qat, qbt, vat, qit, vbt, wit) = _inproj(
        x2, gn[None, :], wn, wt, ghn, ght)
    r3 = lambda a: a.reshape(bsz, s, a.shape[-1])
    bounds = jnp.stack([_logit_bound(qn_a, kn_a, rel_bias[:, :HEADS]),
                        _logit_bound(qn_b, kn_b, rel_bias[:, HEADS:])])
    ya, yb = _attention(rel_bias, (bounds <= MAX_PLAIN_LOGIT).astype(jnp.int32),
                        qat, r3(ka), vat, kmean.reshape(bsz, s // BLK, WIDTH), r3(sga),
                        qbt, qit, wit, r3(kn), vbt, r3(sgb))
    return _outproj(x2, ya.reshape(-1, WIDTH), yb.reshape(-1, WIDTH), sma, smb,
                    w_br_a.astype(BF16), w_br_b.astype(BF16), w_out.astype(BF16))


def kernel(x, norm_g, w_in, q_norm_a, k_norm_a, q_norm_b, k_norm_b,
           w_branch_a, w_branch_b, w_out, rel_bias):
    bsz, s, d = x.shape
    assert d == D_MODEL and s % BLK == 0 and s // BLK <= 31
    h = x.reshape(bsz * s, d)
    for l in range(norm_g.shape[0]):
        h = _layer(h, bsz, s, norm_g[l], w_in[l], q_norm_a[l], k_norm_a[l], q_norm_b[l],
                   k_norm_b[l], w_branch_a[l], w_branch_b[l], w_out[l], rel_bias)
    return h.reshape(bsz, s, d)
```

```python
import functools
import math

import numpy as np
import jax
import jax.numpy as jnp
from jax import lax
from jax.experimental import pallas as pl
from jax.experimental.pallas import tpu as pltpu

D_MODEL = 1024
HEADS = 8
HEAD_DIM = 64
WIDTH = HEADS * HEAD_DIM
BLK = 256
MOBA_TOPK = 3
IDX_HEADS = 4
IDX_DIM = 64
DSA_TOPK_MAX = 256
REL_BUCKETS = 32
REL_MAX_DIST = 128
EPS = 1e-6
NEG = -1e30
M_INIT = -1e20
MAX_PLAIN_LOGIT = 40.0
INT_MIN = -2 ** 31
KEY_BITS = 32
LANES = 128
VMEM_LIMIT = 56 * 1024 * 1024

F32 = jnp.float32
BF16 = jnp.bfloat16

CN_KA, CN_KB2, CN_KI4, CN_GA, CN_GB, CN_MA, CN_MB = 0, 512, 640, 896, 1408, 1920, 2944
N_NAT = 3968
RT_QA, RT_QB, RT_VA, RT_QI, RT_VB, RT_WI = 0, 512, 1024, 1536, 1792, 1856
N_TR = 1872
WI_ROWS = 8
ROWS_IN = 512
ROWS_OUT = 1024
ONES_ROWS = 16
VROWS = HEAD_DIM + ONES_ROWS


def _bucket_starts():
    n = np.arange(0, 2 * REL_MAX_DIST + 2)
    max_exact = REL_BUCKETS // 2
    nf = np.maximum(n, 1).astype(np.float64)
    v = np.log(nf / max_exact) / math.log(REL_MAX_DIST / max_exact) * (REL_BUCKETS - max_exact)
    large = np.minimum(max_exact + np.trunc(v).astype(np.int64), REL_BUCKETS - 1)
    bucket = np.where(n < max_exact, n, large)
    frac = np.abs(v - np.round(v))[(n > max_exact) & (n < REL_MAX_DIST)]
    assert frac.min() > 1e-3
    return [int(np.argmax(bucket >= k)) for k in range(REL_BUCKETS)]


_STARTS = _bucket_starts()


def _dot(a, b):
    return jnp.dot(a, b, preferred_element_type=F32)


def _inproj_kernel(x_ref, gn_ref, wn_ref, wt_ref, ghn_ref, ght_ref,
                   ka_ref, kn_ref, sga_ref, sgb_ref, sma_ref, smb_ref, kmean_ref,
                   qat_ref, qbt_ref, vat_ref, qit_ref, vbt_ref, wit_ref):
    x = x_ref[...]
    r = lax.rsqrt(jnp.mean(x * x, axis=-1, keepdims=True) + EPS)
    hf = (x * r) * gn_ref[...]
    h = hf.astype(BF16)
    ht = hf.T.astype(BF16)

    gr = lax.broadcasted_iota(jnp.int32, (BLK, BLK), 0) // HEAD_DIM
    gc = lax.broadcasted_iota(jnp.int32, (BLK, BLK), 1) // HEAD_DIM
    group = jnp.where(gr == gc, 1.0, 0.0).astype(BF16)

    def proj(c0, n):
        return _dot(h, wn_ref[:, c0:c0 + n])

    def proj_t(r0, n):
        return _dot(wt_ref[r0:r0 + n, :], ht)

    def headnorm(y, c0):
        n = y.shape[-1]
        sq = y * y
        hi = sq.astype(BF16)
        lo = (sq - hi.astype(F32)).astype(BF16)
        g = group[:n, :n]
        ss = _dot(hi, g) + _dot(lo, g)
        return (y * lax.rsqrt(ss * (1.0 / HEAD_DIM) + EPS)) * ghn_ref[:, c0:c0 + n]

    def headnorm_t(yt, r0):
        ss = jnp.sum(yt * yt, axis=0, keepdims=True)
        return (yt * lax.rsqrt(ss * (1.0 / HEAD_DIM) + EPS)) * ght_ref[r0:r0 + HEAD_DIM, :]

    ka = proj(CN_KA, WIDTH)
    for c in range(0, WIDTH, BLK):
        kn = headnorm(ka[:, c:c + BLK], CN_KA + c)
        ka_ref[:, c:c + BLK] = kn.astype(BF16)
        for t in range(kn.shape[0] // BLK):
            kmean_ref[t, :, c:c + BLK] = jnp.mean(kn[t * BLK:(t + 1) * BLK], axis=0, keepdims=True)
    sga_ref[...] = jax.nn.silu(proj(CN_GA, WIDTH))
    sgb_ref[...] = jax.nn.silu(proj(CN_GB, WIDTH))
    small = proj(CN_KB2, 384)
    kn_ref[:, 0:128] = headnorm(small[:, 0:128], CN_KB2).astype(BF16)
    kn_ref[:, 128:384] = small[:, 128:384].astype(BF16)
    for c in range(0, D_MODEL, WIDTH):
        sma_ref[:, c:c + WIDTH] = jax.nn.sigmoid(proj(CN_MA + c, WIDTH))
        smb_ref[:, c:c + WIDTH] = jax.nn.sigmoid(proj(CN_MB + c, WIDTH))

    def put_t(ref, r0, val):
        for t in range(ROWS_IN // BLK):
            ref[t, r0:r0 + val.shape[0], :] = val[:, t * BLK:(t + 1) * BLK]

    scale = HEAD_DIM ** -0.5
    ones = jnp.ones((ONES_ROWS, ROWS_IN), BF16)
    qat = proj_t(RT_QA, WIDTH)
    qbt = proj_t(RT_QB, WIDTH)
    vat = proj_t(RT_VA, WIDTH)
    for hd in range(HEADS):
        r0 = hd * HEAD_DIM
        put_t(qat_ref, r0, (headnorm_t(qat[r0:r0 + HEAD_DIM], RT_QA + r0) * scale).astype(BF16))
        put_t(qbt_ref, r0, (headnorm_t(qbt[r0:r0 + HEAD_DIM], RT_QB + r0) * scale).astype(BF16))
        put_t(vat_ref, hd * VROWS, vat[r0:r0 + HEAD_DIM].astype(BF16))
        put_t(vat_ref, hd * VROWS + HEAD_DIM, ones)
    put_t(qit_ref, 0, (proj_t(RT_QI, 256) * (IDX_DIM ** -0.5)).astype(BF16))
    tail = proj_t(RT_VB, N_TR - RT_VB)
    put_t(vbt_ref, 0, tail[0:HEAD_DIM].astype(BF16))
    put_t(vbt_ref, HEAD_DIM, ones)
    put_t(wit_ref, 0, tail[RT_WI - RT_VB:RT_WI - RT_VB + WI_ROWS] * (IDX_HEADS ** -0.5))


def _inproj(x2, gn, wn, wt, ghn, ght):
    m = x2.shape[0]
    nt = m // BLK
    tpb = ROWS_IN // BLK
    row = lambda n: pl.BlockSpec((ROWS_IN, n), lambda i: (i, 0))
    full = lambda a: pl.BlockSpec(a.shape, lambda i: (0,) * a.ndim)
    tile = lambda n: pl.BlockSpec((tpb, n, BLK), lambda i: (i, 0, 0))
    out_shapes = [
        jax.ShapeDtypeStruct((m, WIDTH), BF16),
        jax.ShapeDtypeStruct((m, 384), BF16),
        jax.ShapeDtypeStruct((m, WIDTH), F32),
        jax.ShapeDtypeStruct((m, WIDTH), F32),
        jax.ShapeDtypeStruct((m, D_MODEL), F32),
        jax.ShapeDtypeStruct((m, D_MODEL), F32),
        jax.ShapeDtypeStruct((nt, 1, WIDTH), F32),
        jax.ShapeDtypeStruct((nt, WIDTH, BLK), BF16),
        jax.ShapeDtypeStruct((nt, WIDTH, BLK), BF16),
        jax.ShapeDtypeStruct((nt, HEADS * VROWS, BLK), BF16),
        jax.ShapeDtypeStruct((nt, 256, BLK), BF16),
        jax.ShapeDtypeStruct((nt, VROWS, BLK), BF16),
        jax.ShapeDtypeStruct((nt, WI_ROWS, BLK), F32),
    ]
    out_specs = [row(WIDTH), row(384), row(WIDTH), row(WIDTH), row(D_MODEL), row(D_MODEL),
                 pl.BlockSpec((tpb, 1, WIDTH), lambda i: (i, 0, 0)),
                 tile(WIDTH), tile(WIDTH), tile(HEADS * VROWS), tile(256), tile(VROWS), tile(WI_ROWS)]
    return pl.pallas_call(
        _inproj_kernel,
        grid=(nt // tpb,),
        in_specs=[row(D_MODEL), full(gn), full(wn), full(wt), full(ghn), full(ght)],
        out_specs=out_specs,
        out_shape=out_shapes,
        compiler_params=pltpu.CompilerParams(
            dimension_semantics=("arbitrary",), vmem_limit_bytes=VMEM_LIMIT),
        name="inproj",
    )(x2, gn, wn, wt, ghn, ght)


def _build_bias_tiles(rel_ref, bias_ref, head0):
    kpos = lax.broadcasted_iota(jnp.int32, (BLK, BLK), 0)
    qpos = lax.broadcasted_iota(jnp.int32, (BLK, BLK), 1)
    for kind in range(2):
        dist = qpos - kpos + BLK * kind
        for h in range(HEADS):
            t = jnp.full((BLK, BLK), rel_ref[0, head0 + h], F32)
            for k in range(1, REL_BUCKETS):
                t = jnp.where(dist >= _STARTS[k], rel_ref[k, head0 + h], t)
            if kind == 0:
                t = jnp.where(dist >= 0, t, NEG)
            bias_ref[3 * h + kind] = t
    for h in range(HEADS):
        bias_ref[3 * h + 2] = jnp.full((BLK, BLK), rel_ref[REL_BUCKETS - 1, head0 + h], F32)


def _store_masked_queries(qt_ref, qm_ref):
    rowh = lax.broadcasted_iota(jnp.int32, (LANES, BLK), 0)
    for p in range(HEADS // 2):
        q2 = qt_ref[0, LANES * p:LANES * (p + 1), :]
        for hh in range(2):
            qm_ref[2 * p + hh] = jnp.where((rowh >= HEAD_DIM) == bool(hh), q2, jnp.zeros_like(q2))


def _flash_exact(npair, score_fn, pv_fn, s_ref, acc_ref, sel_fn=None):
    def stage(t, ms):
        out = []
        for h in range(HEADS):
            m_new = ms[h]
            for u in range(2):
                j = 2 * t + u
                s = score_fn(j, h)
                s_ref[h, u * BLK:(u + 1) * BLK, :] = s
                cand = jnp.max(s, axis=0, keepdims=True)
                if sel_fn is not None:
                    cand = jnp.where(sel_fn(j, h), cand, M_INIT)
                m_new = jnp.maximum(m_new, cand)
            out.append(m_new)
        return tuple(out)

    def finish(t, ms_old, ms_new):
        for h in range(HEADS):
            acc = jnp.exp(ms_old[h] - ms_new[h]) * acc_ref[h]
            for u in range(2):
                j = 2 * t + u
                p = jnp.exp(s_ref[h, u * BLK:(u + 1) * BLK, :] - ms_new[h])
                pv = pv_fn(j, h, p.astype(BF16))
                acc = acc + (pv if sel_fn is None else jnp.where(sel_fn(j, h), pv, 0.0))
            acc_ref[h] = acc

    def step(t, ms):
        m_new = stage(t, ms)
        finish(t, ms, m_new)
        return m_new

    lax.fori_loop(0, npair, step, tuple(jnp.full((1, BLK), M_INIT, F32) for _ in range(HEADS)))


def _flash_bounded(pairs, score_fn, pv_fn, p_ref, acc_ref, sel_fn=None, side_fn=None, pv_scale=None):
    def step(t, c):
        for h in range(HEADS):
            for u in range(2):
                p_ref[2 * h + u] = jnp.exp(score_fn(2 * t + u, h)).astype(BF16)
        if side_fn is not None:
            side_fn(t)
        for h in range(HEADS):
            new = None
            for u in range(2):
                j = 2 * t + u
                pv = pv_fn(j, h, p_ref[2 * h + u])
                pv = pv if sel_fn is None else jnp.where(sel_fn(j, h), pv, 0.0)
                new = pv if new is None else new + pv
            acc_ref[h] = acc_ref[h] + (new if pv_scale is None else new * pv_scale[h])
        return c

    lax.fori_loop(pairs[0], pairs[1], step, 0)


def _logit_bound(q_gain, k_gain, bias):
    return (1.01 * HEAD_DIM ** 0.5 * jnp.max(jnp.abs(q_gain)) * jnp.max(jnp.abs(k_gain))
            + jnp.max(jnp.abs(bias)))


def _write_output(acc_ref, sg_ref, o_ref):
    def head_out(h):
        return acc_ref[h, 0:HEAD_DIM, :] / acc_ref[h, HEAD_DIM:HEAD_DIM + 1, :]

    for p in range(HEADS // 2):
        ot = jnp.concatenate([head_out(2 * p), head_out(2 * p + 1)], axis=0)
        cs = slice(LANES * p, LANES * (p + 1))
        o_ref[0, :, cs] = (ot.T * sg_ref[0, :, cs]).astype(BF16)


def _bit_planes(u):
    x = [u[8 * a:8 * a + 8, :] for a in range(KEY_BITS)]
    j, m = 16, 0x0000FFFF
    while j:
        for k in range(KEY_BITS):
            if not k & j:
                t = (x[k] ^ jnp.right_shift(x[k + j], j)) & m
                x[k] = x[k] ^ t
                x[k + j] = x[k + j] ^ jnp.left_shift(t, j)
        j >>= 1
        m ^= (m << j) & 0xFFFFFFFF
    return x


def _attn_kernel(rel_ref, bounded_ref,
                 qat_ref, ka_ref, vat_ref, kmean_ref, sga_ref,
                 qt_ref, qit_ref, wit_ref, kn_ref, vt_ref, sg_ref,
                 oa_ref, o_ref,
                 biasa_ref, bias_ref, qma_ref, qm_ref, acca_ref, acc_ref, s_ref, p_ref,
                 key_ref, plane_ref, alive_ref, madd_ref, *, topk):
    b = pl.program_id(0)
    i = pl.program_id(1)
    nblk = i + 1
    nb = key_ref.shape[0]
    bounded_a = bounded_ref[0] == 1
    bounded_b = bounded_ref[1] == 1

    @pl.when((b == 0) & (i == 0))
    def _():
        _build_bias_tiles(rel_ref, biasa_ref, 0)
        _build_bias_tiles(rel_ref, bias_ref, HEADS)
        plane_ref[...] = jnp.zeros_like(plane_ref)

    _store_masked_queries(qat_ref, qma_ref)
    _store_masked_queries(qt_ref, qm_ref)
    acca_ref[...] = jnp.zeros_like(acca_ref)
    acc_ref[...] = jnp.zeros_like(acc_ref)

    nidx = lax.broadcasted_iota(jnp.int32, (nb, BLK), 0)
    sel_bits = []
    for h in range(HEADS):
        cs = slice(LANES * (h // 2), LANES * (h // 2 + 1))
        gate = _dot(kmean_ref[0, :, cs].astype(BF16), qma_ref[h])
        g = jnp.where(nidx < i, gate, -jnp.inf)
        bits = jnp.left_shift(1, jnp.full((1, BLK), i, jnp.int32))
        for _ in range(MOBA_TOPK):
            mx = jnp.max(g, axis=0, keepdims=True)
            am = jnp.min(jnp.where(g == mx, nidx, nb), axis=0, keepdims=True)
            bits = bits | jnp.where(mx > -jnp.inf, jnp.left_shift(1, am), 0)
            g = jnp.where(nidx == am, -jnp.inf, g)
        sel_bits.append(bits)

    def score_a(j, h):
        rows = pl.ds(pl.multiple_of(j * BLK, BLK), BLK)
        cs = slice(LANES * (h // 2), LANES * (h // 2 + 1))
        return _dot(ka_ref[0, rows, cs], qma_ref[h]) + biasa_ref[3 * h + jnp.clip(i - j, 0, 2)]

    def pv_a(j, h, p):
        return _dot(vat_ref[0, j, h * VROWS:(h + 1) * VROWS, :], p)

    def sel_a(j, h):
        return (jnp.right_shift(sel_bits[h], j) & 1) == 1

    kpos = lax.broadcasted_iota(jnp.int32, (BLK, BLK), 0)
    qpos = lax.broadcasted_iota(jnp.int32, (BLK, BLK), 1)

    qit = qit_ref[0]
    wit = wit_ref[0]
    qims = [jnp.where((kpos // IDX_DIM) == h, qit, jnp.zeros_like(qit)) for h in range(IDX_HEADS)]
    whs = [wit[h:h + 1, :] for h in range(IDX_HEADS)]

    def score_block(j, causal):
        rows = pl.ds(pl.multiple_of(j * BLK, BLK), BLK)
        ki4 = kn_ref[0, rows, 128:384]
        score = jnp.zeros((BLK, BLK), F32)
        for h in range(IDX_HEADS):
            score = score + jnp.maximum(_dot(ki4, qims[h]), 0.0) * whs[h]
        bits = lax.bitcast_convert_type(score, jnp.int32)
        key = bits ^ (jnp.right_shift(bits, 31) & 0x7FFFFFFF)
        if causal:
            key = jnp.where(kpos <= qpos, key, INT_MIN)
        key_ref[j] = key
        planes = _bit_planes(key ^ INT_MIN)
        for r in range(KEY_BITS):
            plane_ref[j, r] = planes[r]

    for extra in range(3):
        key_ref[jnp.minimum(nblk + extra, nb - 1)] = jnp.full((BLK, BLK), INT_MIN, jnp.int32)

    def score_pair(t, c=0):
        score_block(2 * t, False)
        score_block(2 * t + 1, False)
        return c

    def score_single(_, c):
        score_block(i - 1, False)
        return c

    npair = (nblk + 1) // 2
    full_pairs = i // 2

    @pl.when(bounded_a)
    def _():
        _flash_bounded((0, full_pairs), score_a, pv_a, p_ref, acca_ref, sel_a, side_fn=score_pair)
        _flash_bounded((full_pairs, npair), score_a, pv_a, p_ref, acca_ref, sel_a)

    @pl.when(jnp.logical_not(bounded_a))
    def _():
        _flash_exact(npair, score_a, pv_a, s_ref, acca_ref, sel_a)
        lax.fori_loop(0, full_pairs, score_pair, 0)

    _write_output(acca_ref, sga_ref, oa_ref)
    lax.fori_loop(0, i % 2, score_single, 0)
    score_block(i, True)

    for j in range(nb):
        alive_ref[j] = jnp.where(j < nblk, jnp.full((8, BLK), -1, jnp.int32), 0)

    def radix_round(tiles, r, carry):
        above, thr_u = carry
        ones_here = jnp.zeros((8, BLK), jnp.int32)
        for j in range(tiles):
            ones_here = ones_here + lax.population_count(alive_ref[j] & plane_ref[j, r])
        ones_here = jnp.sum(ones_here, axis=0, keepdims=True)
        take = (above + ones_here) >= topk
        flip = jnp.where(take, 0, -1)
        for j in range(tiles):
            alive_ref[j] = alive_ref[j] & (plane_ref[j, r] ^ flip)
        return (jnp.where(take, above, above + ones_here),
                thr_u | jnp.where(take, jnp.left_shift(1, KEY_BITS - 1 - r), 0))

    def radix_select(tiles):
        zero = jnp.zeros((1, BLK), jnp.int32)
        return lambda: lax.fori_loop(0, KEY_BITS, functools.partial(radix_round, tiles), (zero, zero))

    above, thr_u = lax.cond(nblk <= nb // 2, radix_select(nb // 2), radix_select(nb))
    thr = thr_u ^ INT_MIN

    need = jnp.where(thr > INT_MIN, topk - above, 0).astype(F32)
    lower = jnp.where(qpos <= kpos, 1.0, 0.0).astype(BF16)

    def mask_body(t, carry):
        for j in [4 * t + u for u in range(4)]:
            k = key_ref[j]
            eq = k == thr
            pref = _dot(lower, jnp.where(eq, 1.0, 0.0).astype(BF16))
            sel = (k > thr) | (eq & ((carry + pref) <= need))
            madd_ref[j] = jnp.where(sel, 0.0, NEG)
            carry = carry + pref[BLK - 1:BLK, :]
        return carry

    lax.fori_loop(0, (nblk + 3) // 4, mask_body, jnp.zeros((1, BLK), F32))

    def score(j, h):
        rows = pl.ds(pl.multiple_of(j * BLK, BLK), BLK)
        return (_dot(kn_ref[0, rows, 0:128], qm_ref[h])
                + bias_ref[3 * h + jnp.clip(i - j, 0, 2)] + madd_ref[j])

    def pv(j, h, p):
        return _dot(vt_ref[0, j], p)

    far_pairs = jnp.maximum(i - 1, 0) // 2

    def score_far(j, h):
        rows = pl.ds(pl.multiple_of(j * BLK, BLK), BLK)
        return _dot(kn_ref[0, rows, 0:128], qm_ref[h]) + madd_ref[j]

    @pl.when(bounded_b)
    def _():
        far_factor = [jnp.exp(jnp.full((1, BLK), rel_ref[REL_BUCKETS - 1, HEADS + h], F32))
                      for h in range(HEADS)]
        _flash_bounded((0, far_pairs), score_far, pv, p_ref, acc_ref, pv_scale=far_factor)
        _flash_bounded((far_pairs, npair), score, pv, p_ref, acc_ref)

    @pl.when(jnp.logical_not(bounded_b))
    def _():
        _flash_exact(npair, score, pv, s_ref, acc_ref)

    _write_output(acc_ref, sg_ref, o_ref)


def _attention(rel_bias, bounded, qat, ka, vat, kmean, sga, qbt, qit, wit, kn, vbt, sgb):
    bsz, s, _ = kn.shape
    nb = s // BLK
    topk = min(DSA_TOPK_MAX, s // 4)
    smem = pl.BlockSpec(memory_space=pltpu.SMEM)
    qspec = pl.BlockSpec((1, BLK, WIDTH), lambda b, i: (b, i, 0))
    tile = lambda n: pl.BlockSpec((1, n, BLK), lambda b, i: (b * nb + i, 0, 0))
    row = lambda *shape: pl.BlockSpec((1,) + shape, lambda b, i: (b,) + (0,) * len(shape),
                                      pipeline_mode=pl.Buffered(1))
    out = jax.ShapeDtypeStruct((bsz, s, WIDTH), BF16)
    return pl.pallas_call(
        functools.partial(_attn_kernel, topk=topk),
        grid=(bsz, nb),
        in_specs=[smem, smem,
                  tile(WIDTH), row(s, WIDTH), row(nb, HEADS * VROWS, BLK), row(nb, WIDTH), qspec,
                  tile(WIDTH), tile(256), tile(WI_ROWS), row(s, 384), row(nb, VROWS, BLK), qspec],
        out_specs=[qspec, qspec],
        out_shape=[out, out],
        scratch_shapes=[pltpu.VMEM((3 * HEADS, BLK, BLK), F32),
                        pltpu.VMEM((3 * HEADS, BLK, BLK), F32),
                        pltpu.VMEM((HEADS, LANES, BLK), BF16),
                        pltpu.VMEM((HEADS, LANES, BLK), BF16),
                        pltpu.VMEM((HEADS, VROWS, BLK), F32),
                        pltpu.VMEM((HEADS, VROWS, BLK), F32),
                        pltpu.VMEM((HEADS, 2 * BLK, BLK), F32),
                        pltpu.VMEM((2 * HEADS, BLK, BLK), BF16),
                        pltpu.VMEM((nb, BLK, BLK), jnp.int32),
                        pltpu.VMEM((nb, KEY_BITS, 8, BLK), jnp.int32),
                        pltpu.VMEM((nb, 8, BLK), jnp.int32),
                        pltpu.VMEM((nb, BLK, BLK), F32)],
        compiler_params=pltpu.CompilerParams(
            dimension_semantics=("arbitrary", "arbitrary"), vmem_limit_bytes=VMEM_LIMIT),
        name="attention",
    )(rel_bias, bounded, qat, ka, vat.reshape(bsz, nb, HEADS * VROWS, BLK), kmean, sga,
      qbt, qit, wit, kn, vbt.reshape(bsz, nb, VROWS, BLK), sgb)


def _outproj_kernel(x_ref, ya_ref, yb_ref, sma_ref, smb_ref, wa_ref, wb_ref, wo_ref, o_ref):
    merged = (sma_ref[...] * _dot(ya_ref[...], wa_ref[...])
              + smb_ref[...] * _dot(yb_ref[...], wb_ref[...]))
    o_ref[...] = x_ref[...] + _dot(merged.astype(BF16), wo_ref[...])


def _outproj(x2, ya, yb, sma, smb, wa, wb, wo):
    m = x2.shape[0]
    row = lambda n: pl.BlockSpec((ROWS_OUT, n), lambda i: (i, 0))
    full = lambda a: pl.BlockSpec(a.shape, lambda i: (0,) * a.ndim)
    return pl.pallas_call(
        _outproj_kernel,
        grid=(m // ROWS_OUT,),
        in_specs=[row(D_MODEL), row(WIDTH), row(WIDTH), row(D_MODEL), row(D_MODEL),
                  full(wa), full(wb), full(wo)],
        out_specs=row(D_MODEL),
        out_shape=jax.ShapeDtypeStruct((m, D_MODEL), F32),
        compiler_params=pltpu.CompilerParams(
            dimension_semantics=("arbitrary",), vmem_limit_bytes=VMEM_LIMIT),
        name="outproj",
    )(x2, ya, yb, sma, smb, wa, wb, wo)


def _pack_w_in(w):
    o = 0
    parts = {}
    for name, n in (("qa", WIDTH), ("ka", WIDTH), ("va", WIDTH), ("ga", WIDTH), ("qb", WIDTH),
                    ("kb", HEAD_DIM), ("vb", HEAD_DIM), ("gb", WIDTH), ("qi", IDX_HEADS * IDX_DIM),
                    ("ki", IDX_DIM), ("wi", IDX_HEADS), ("ma", D_MODEL), ("mb", D_MODEL)):
        parts[name] = w[:, o:o + n]
        o += n
    wn = jnp.concatenate([parts["ka"], parts["kb"], parts["kb"],
                          parts["ki"], parts["ki"], parts["ki"], parts["ki"],
                          parts["ga"], parts["gb"], parts["ma"], parts["mb"]], axis=1).astype(BF16)
    pad = jnp.zeros((w.shape[0], N_TR - RT_WI - IDX_HEADS), w.dtype)
    wt = jnp.concatenate([parts["qa"], parts["qb"], parts["va"], parts["qi"], parts["vb"],
                          parts["wi"], pad], axis=1).T.astype(BF16)
    assert wn.shape[1] == N_NAT and wt.shape[0] == N_TR
    return wn, wt


def _layer(x2, bsz, s, gn, w_in, qn_a, kn_a, qn_b, kn_b, w_br_a, w_br_b, w_out, rel_bias):
    wn, wt = _pack_w_in(w_in)
    ghn = jnp.concatenate([jnp.tile(kn_a, HEADS), jnp.tile(kn_b, 2)])[None, :]
    ght = jnp.concatenate([jnp.tile(qn_a, HEADS), jnp.tile(qn_b, HEADS)])[:, None]
    (ka, kn, sga, sgb, sma, smb, kmean, qat, qbt, vat, qit, vbt, wit) = _inproj(
        x2, gn[None, :], wn, wt, ghn, ght)
    r3 = lambda a: a.reshape(bsz, s, a.shape[-1])
    bounds = jnp.stack([_logit_bound(qn_a, kn_a, rel_bias[:, :HEADS]),
                        _logit_bound(qn_b, kn_b, rel_bias[:, HEADS:])])
    ya, yb = _attention(rel_bias, (bounds <= MAX_PLAIN_LOGIT).astype(jnp.int32),
                        qat, r3(ka), vat, kmean.reshape(bsz, s // BLK, WIDTH), r3(sga),
                        qbt, qit, wit, r3(kn), vbt, r3(sgb))
    return _outproj(x2, ya.reshape(-1, WIDTH), yb.reshape(-1, WIDTH), sma, smb,
                    w_br_a.astype(BF16), w_br_b.astype(BF16), w_out.astype(BF16))


def kernel(x, norm_g, w_in, q_norm_a, k_norm_a, q_norm_b, k_norm_b,
           w_branch_a, w_branch_b, w_out, rel_bias):
    bsz, s, d = x.shape
    assert d == D_MODEL and s % (4 * BLK) == 0 and s // BLK <= 31
    h = x.reshape(bsz * s, d)
    for l in range(norm_g.shape[0]):
        h = _layer(h, bsz, s, norm_g[l], w_in[l], q_norm_a[l], k_norm_a[l], q_norm_b[l],
                   k_norm_b[l], w_branch_a[l], w_branch_b[l], w_out[l], rel_bias)
    return h.reshape(bsz, s, d)
```

```python
import functools
import math

import numpy as np
import jax
import jax.numpy as jnp
from jax import lax
from jax.experimental import pallas as pl
from jax.experimental.pallas import tpu as pltpu

D_MODEL = 1024
HEADS = 8
HEAD_DIM = 64
WIDTH = HEADS * HEAD_DIM
BLK = 256
MOBA_TOPK = 3
IDX_HEADS = 4
IDX_DIM = 64
DSA_TOPK_MAX = 256
REL_BUCKETS = 32
REL_MAX_DIST = 128
EPS = 1e-6
NEG = -1e30
M_INIT = -1e20
MAX_PLAIN_LOGIT = 40.0
INT_MIN = -2 ** 31
KEY_BITS = 32
LANES = 128
VMEM_LIMIT = 56 * 1024 * 1024

F32 = jnp.float32
BF16 = jnp.bfloat16

CN_KA, CN_KB2, CN_KI4, CN_GA, CN_GB, CN_MA, CN_MB = 0, 512, 640, 896, 1408, 1920, 2944
N_NAT = 3968
RT_QA, RT_QB, RT_VA, RT_QI, RT_VB, RT_WI = 0, 512, 1024, 1536, 1792, 1856
N_TR = 1872
WI_ROWS = 8
KB2 = 2 * HEAD_DIM
QI = IDX_HEADS * IDX_DIM
KN = KB2 + QI
ROWS_IN = 512
ROWS_OUT = 1024
ONES_ROWS = 16
VROWS = HEAD_DIM + ONES_ROWS


def _bucket_starts():
    n = np.arange(0, 2 * REL_MAX_DIST + 2)
    max_exact = REL_BUCKETS // 2
    nf = np.maximum(n, 1).astype(np.float64)
    v = np.log(nf / max_exact) / math.log(REL_MAX_DIST / max_exact) * (REL_BUCKETS - max_exact)
    large = np.minimum(max_exact + np.trunc(v).astype(np.int64), REL_BUCKETS - 1)
    bucket = np.where(n < max_exact, n, large)
    frac = np.abs(v - np.round(v))[(n > max_exact) & (n < REL_MAX_DIST)]
    assert frac.min() > 1e-3
    return [int(np.argmax(bucket >= k)) for k in range(REL_BUCKETS)]


_STARTS = _bucket_starts()


def _dot(a, b):
    return jnp.dot(a, b, preferred_element_type=F32)


def _inproj_kernel(x_ref, gn_ref, wn_ref, wt_ref, ghn_ref, ght_ref,
                   ka_ref, kn_ref, sga_ref, sgb_ref, sma_ref, smb_ref, kmean_ref,
                   qat_ref, qbt_ref, vat_ref, qit_ref, vbt_ref, wit_ref):
    x = x_ref[...]
    r = lax.rsqrt(jnp.mean(x * x, axis=-1, keepdims=True) + EPS)
    hf = (x * r) * gn_ref[...]
    h = hf.astype(BF16)
    ht = hf.T.astype(BF16)

    gr = lax.broadcasted_iota(jnp.int32, (BLK, BLK), 0) // HEAD_DIM
    gc = lax.broadcasted_iota(jnp.int32, (BLK, BLK), 1) // HEAD_DIM
    group = jnp.where(gr == gc, 1.0, 0.0).astype(BF16)

    def proj(c0, n):
        return _dot(h, wn_ref[:, c0:c0 + n])

    def proj_t(r0, n):
        return _dot(wt_ref[r0:r0 + n, :], ht)

    def headnorm(y, c0):
        n = y.shape[-1]
        sq = y * y
        hi = sq.astype(BF16)
        lo = (sq - hi.astype(F32)).astype(BF16)
        g = group[:n, :n]
        ss = _dot(hi, g) + _dot(lo, g)
        return (y * lax.rsqrt(ss * (1.0 / HEAD_DIM) + EPS)) * ghn_ref[:, c0:c0 + n]

    def headnorm_t(yt, r0):
        ss = jnp.sum(yt * yt, axis=0, keepdims=True)
        return (yt * lax.rsqrt(ss * (1.0 / HEAD_DIM) + EPS)) * ght_ref[r0:r0 + HEAD_DIM, :]

    ka = proj(CN_KA, WIDTH)
    for c in range(0, WIDTH, BLK):
        kn = headnorm(ka[:, c:c + BLK], CN_KA + c)
        ka_ref[:, c:c + BLK] = kn.astype(BF16)
        for t in range(kn.shape[0] // BLK):
            kmean_ref[t, :, c:c + BLK] = jnp.mean(kn[t * BLK:(t + 1) * BLK], axis=0, keepdims=True)
    sga_ref[...] = jax.nn.silu(proj(CN_GA, WIDTH))
    sgb_ref[...] = jax.nn.silu(proj(CN_GB, WIDTH))
    small = proj(CN_KB2, KN)
    kn_ref[:, 0:KB2] = headnorm(small[:, 0:KB2], CN_KB2).astype(BF16)
    kn_ref[:, KB2:KN] = small[:, KB2:KN].astype(BF16)
    for c in range(0, D_MODEL, WIDTH):
        sma_ref[:, c:c + WIDTH] = jax.nn.sigmoid(proj(CN_MA + c, WIDTH))
        smb_ref[:, c:c + WIDTH] = jax.nn.sigmoid(proj(CN_MB + c, WIDTH))

    def put_t(ref, r0, val):
        for t in range(ROWS_IN // BLK):
            ref[t, r0:r0 + val.shape[0], :] = val[:, t * BLK:(t + 1) * BLK]

    scale = HEAD_DIM ** -0.5
    ones = jnp.ones((ONES_ROWS, ROWS_IN), BF16)
    qat = proj_t(RT_QA, WIDTH)
    qbt = proj_t(RT_QB, WIDTH)
    vat = proj_t(RT_VA, WIDTH)
    for hd in range(HEADS):
        r0 = hd * HEAD_DIM
        put_t(qat_ref, r0, (headnorm_t(qat[r0:r0 + HEAD_DIM], RT_QA + r0) * scale).astype(BF16))
        put_t(qbt_ref, r0, (headnorm_t(qbt[r0:r0 + HEAD_DIM], RT_QB + r0) * scale).astype(BF16))
        put_t(vat_ref, hd * VROWS, vat[r0:r0 + HEAD_DIM].astype(BF16))
        put_t(vat_ref, hd * VROWS + HEAD_DIM, ones)
    put_t(qit_ref, 0, (proj_t(RT_QI, QI) * (IDX_DIM ** -0.5)).astype(BF16))
    tail = proj_t(RT_VB, N_TR - RT_VB)
    put_t(vbt_ref, 0, tail[0:HEAD_DIM].astype(BF16))
    put_t(vbt_ref, HEAD_DIM, ones)
    put_t(wit_ref, 0, tail[RT_WI - RT_VB:RT_WI - RT_VB + WI_ROWS] * (IDX_HEADS ** -0.5))


def _inproj(x2, gn, wn, wt, ghn, ght):
    m = x2.shape[0]
    nt = m // BLK
    tpb = ROWS_IN // BLK
    row = lambda n: pl.BlockSpec((ROWS_IN, n), lambda i: (i, 0))
    full = lambda a: pl.BlockSpec(a.shape, lambda i: (0,) * a.ndim)
    tile = lambda n: pl.BlockSpec((tpb, n, BLK), lambda i: (i, 0, 0))
    out_shapes = [
        jax.ShapeDtypeStruct((m, WIDTH), BF16),
        jax.ShapeDtypeStruct((m, KN), BF16),
        jax.ShapeDtypeStruct((m, WIDTH), F32),
        jax.ShapeDtypeStruct((m, WIDTH), F32),
        jax.ShapeDtypeStruct((m, D_MODEL), F32),
        jax.ShapeDtypeStruct((m, D_MODEL), F32),
        jax.ShapeDtypeStruct((nt, 1, WIDTH), F32),
        jax.ShapeDtypeStruct((nt, WIDTH, BLK), BF16),
        jax.ShapeDtypeStruct((nt, WIDTH, BLK), BF16),
        jax.ShapeDtypeStruct((nt, HEADS * VROWS, BLK), BF16),
        jax.ShapeDtypeStruct((nt, QI, BLK), BF16),
        jax.ShapeDtypeStruct((nt, VROWS, BLK), BF16),
        jax.ShapeDtypeStruct((nt, WI_ROWS, BLK), F32),
    ]
    out_specs = [row(WIDTH), row(KN), row(WIDTH), row(WIDTH), row(D_MODEL), row(D_MODEL),
                 pl.BlockSpec((tpb, 1, WIDTH), lambda i: (i, 0, 0)),
                 tile(WIDTH), tile(WIDTH), tile(HEADS * VROWS), tile(QI), tile(VROWS), tile(WI_ROWS)]
    return pl.pallas_call(
        _inproj_kernel,
        grid=(nt // tpb,),
        in_specs=[row(D_MODEL), full(gn), full(wn), full(wt), full(ghn), full(ght)],
        out_specs=out_specs,
        out_shape=out_shapes,
        compiler_params=pltpu.CompilerParams(
            dimension_semantics=("arbitrary",), vmem_limit_bytes=VMEM_LIMIT),
        name="inproj",
    )(x2, gn, wn, wt, ghn, ght)


def _build_bias_tiles(rel_ref, bias_ref, head0):
    kpos = lax.broadcasted_iota(jnp.int32, (BLK, BLK), 0)
    qpos = lax.broadcasted_iota(jnp.int32, (BLK, BLK), 1)
    for kind in range(2):
        dist = qpos - kpos + BLK * kind
        for h in range(HEADS):
            t = jnp.full((BLK, BLK), rel_ref[0, head0 + h], F32)
            for k in range(1, REL_BUCKETS):
                t = jnp.where(dist >= _STARTS[k], rel_ref[k, head0 + h], t)
            if kind == 0:
                t = jnp.where(dist >= 0, t, NEG)
            bias_ref[3 * h + kind] = t
    for h in range(HEADS):
        bias_ref[3 * h + 2] = jnp.full((BLK, BLK), rel_ref[REL_BUCKETS - 1, head0 + h], F32)


def _store_masked_queries(qt_ref, qm_ref):
    rowh = lax.broadcasted_iota(jnp.int32, (LANES, BLK), 0)
    for p in range(HEADS // 2):
        q2 = qt_ref[0, LANES * p:LANES * (p + 1), :]
        for hh in range(2):
            qm_ref[2 * p + hh] = jnp.where((rowh >= HEAD_DIM) == bool(hh), q2, jnp.zeros_like(q2))


def _flash_exact(npair, score_fn, pv_fn, s_ref, acc_ref, sel_fn=None):
    def stage(t, ms):
        out = []
        for h in range(HEADS):
            m_new = ms[h]
            for u in range(2):
                j = 2 * t + u
                s = score_fn(j, h)
                s_ref[h, u * BLK:(u + 1) * BLK, :] = s
                cand = jnp.max(s, axis=0, keepdims=True)
                if sel_fn is not None:
                    cand = jnp.where(sel_fn(j, h), cand, M_INIT)
                m_new = jnp.maximum(m_new, cand)
            out.append(m_new)
        return tuple(out)

    def finish(t, ms_old, ms_new):
        for h in range(HEADS):
            acc = jnp.exp(ms_old[h] - ms_new[h]) * acc_ref[h]
            for u in range(2):
                j = 2 * t + u
                p = jnp.exp(s_ref[h, u * BLK:(u + 1) * BLK, :] - ms_new[h])
                pv = pv_fn(j, h, p.astype(BF16))
                acc = acc + (pv if sel_fn is None else jnp.where(sel_fn(j, h), pv, 0.0))
            acc_ref[h] = acc

    def step(t, ms):
        m_new = stage(t, ms)
        finish(t, ms, m_new)
        return m_new

    lax.fori_loop(0, npair, step, tuple(jnp.full((1, BLK), M_INIT, F32) for _ in range(HEADS)))


def _flash_bounded(pairs, score_fn, pv_fn, p_ref, acc_ref, sel_fn=None, side_fn=None, pv_scale=None,
                   width=2):
    def step(t, c):
        for h in range(HEADS):
            for u in range(width):
                p_ref[width * h + u] = jnp.exp(score_fn(width * t + u, h)).astype(BF16)
        if side_fn is not None:
            side_fn(t)
        for h in range(HEADS):
            new = None
            for u in range(width):
                j = width * t + u
                pv = pv_fn(j, h, p_ref[width * h + u])
                pv = pv if sel_fn is None else jnp.where(sel_fn(j, h), pv, 0.0)
                new = pv if new is None else new + pv
            acc_ref[h] = acc_ref[h] + (new if pv_scale is None else new * pv_scale[h])
        return c

    lax.fori_loop(pairs[0], pairs[1], step, 0)


def _logit_bound(q_gain, k_gain, bias):
    return (1.01 * HEAD_DIM ** 0.5 * jnp.max(jnp.abs(q_gain)) * jnp.max(jnp.abs(k_gain))
            + jnp.max(jnp.abs(bias)))


def _write_output(acc_ref, sg_ref, o_ref):
    def head_out(h):
        return acc_ref[h, 0:HEAD_DIM, :] / acc_ref[h, HEAD_DIM:HEAD_DIM + 1, :]

    for p in range(HEADS // 2):
        ot = jnp.concatenate([head_out(2 * p), head_out(2 * p + 1)], axis=0)
        cs = slice(LANES * p, LANES * (p + 1))
        o_ref[0, :, cs] = (ot.T * sg_ref[0, :, cs]).astype(BF16)


def _bit_planes(u):
    x = [u[8 * a:8 * a + 8, :] for a in range(KEY_BITS)]
    j, m = 16, 0x0000FFFF
    while j:
        for k in range(KEY_BITS):
            if not k & j:
                t = (x[k] ^ jnp.right_shift(x[k + j], j)) & m
                x[k] = x[k] ^ t
                x[k + j] = x[k + j] ^ jnp.left_shift(t, j)
        j >>= 1
        m ^= (m << j) & 0xFFFFFFFF
    return x


def _attn_kernel(rel_ref, bounded_ref,
                 qat_ref, ka_ref, vat_ref, kmean_ref, sga_ref,
                 qt_ref, qit_ref, wit_ref, kn_ref, vt_ref, sg_ref,
                 oa_ref, o_ref,
                 biasa_ref, bias_ref, qma_ref, qm_ref, acca_ref, acc_ref, s_ref, p_ref,
                 key_ref, plane_ref, alive_ref, madd_ref, *, topk):
    b = pl.program_id(0)
    i = pl.program_id(1)
    nblk = i + 1
    nb = key_ref.shape[0]
    bounded_a = bounded_ref[0] == 1
    bounded_b = bounded_ref[1] == 1

    @pl.when((b == 0) & (i == 0))
    def _():
        _build_bias_tiles(rel_ref, biasa_ref, 0)
        _build_bias_tiles(rel_ref, bias_ref, HEADS)
        plane_ref[...] = jnp.zeros_like(plane_ref)

    _store_masked_queries(qat_ref, qma_ref)
    _store_masked_queries(qt_ref, qm_ref)
    acca_ref[...] = jnp.zeros_like(acca_ref)
    acc_ref[...] = jnp.zeros_like(acc_ref)

    nidx = lax.broadcasted_iota(jnp.int32, (nb, BLK), 0)
    sel_bits = []
    for h in range(HEADS):
        cs = slice(LANES * (h // 2), LANES * (h // 2 + 1))
        gate = _dot(kmean_ref[0, :, cs].astype(BF16), qma_ref[h])
        g = jnp.where(nidx < i, gate, -jnp.inf)
        bits = jnp.left_shift(1, jnp.full((1, BLK), i, jnp.int32))
        for _ in range(MOBA_TOPK):
            mx = jnp.max(g, axis=0, keepdims=True)
            am = jnp.min(jnp.where(g == mx, nidx, nb), axis=0, keepdims=True)
            bits = bits | jnp.where(mx > -jnp.inf, jnp.left_shift(1, am), 0)
            g = jnp.where(nidx == am, -jnp.inf, g)
        sel_bits.append(bits)

    def score_a(j, h):
        rows = pl.ds(pl.multiple_of(j * BLK, BLK), BLK)
        cs = slice(LANES * (h // 2), LANES * (h // 2 + 1))
        return _dot(ka_ref[0, rows, cs], qma_ref[h]) + biasa_ref[3 * h + jnp.clip(i - j, 0, 2)]

    def pv_a(j, h, p):
        return _dot(vat_ref[0, j, h * VROWS:(h + 1) * VROWS, :], p)

    def sel_a(j, h):
        return (jnp.right_shift(sel_bits[h], j) & 1) == 1

    kpos = lax.broadcasted_iota(jnp.int32, (BLK, BLK), 0)
    qpos = lax.broadcasted_iota(jnp.int32, (BLK, BLK), 1)

    qit = qit_ref[0]
    wit = wit_ref[0]
    qims = [jnp.where((kpos // IDX_DIM) == h, qit, jnp.zeros_like(qit)) for h in range(IDX_HEADS)]
    whs = [wit[h:h + 1, :] for h in range(IDX_HEADS)]

    def score_block(j, causal):
        rows = pl.ds(pl.multiple_of(j * BLK, BLK), BLK)
        ki4 = kn_ref[0, rows, KB2:KN]
        score = jnp.zeros((BLK, BLK), F32)
        for h in range(IDX_HEADS):
            score = score + jnp.maximum(_dot(ki4, qims[h]), 0.0) * whs[h]
        bits = lax.bitcast_convert_type(score, jnp.int32)
        key = bits ^ (jnp.right_shift(bits, 31) & 0x7FFFFFFF)
        if causal:
            key = jnp.where(kpos <= qpos, key, INT_MIN)
        key_ref[j] = key
        planes = _bit_planes(key ^ INT_MIN)
        for r in range(KEY_BITS):
            plane_ref[j, r] = planes[r]

    for extra in range(3):
        key_ref[jnp.minimum(nblk + extra, nb - 1)] = jnp.full((BLK, BLK), INT_MIN, jnp.int32)

    def score_pair(t, c=0):
        score_block(2 * t, False)
        score_block(2 * t + 1, False)
        return c

    def score_single(_, c):
        score_block(i - 1, False)
        return c

    npair = (nblk + 1) // 2
    full_pairs = i // 2

    def score_quad(t):
        score_pair(2 * t)
        score_pair(2 * t + 1)

    @pl.when(bounded_a)
    def _():
        full_quads = full_pairs // 2
        _flash_bounded((0, full_quads), score_a, pv_a, p_ref, acca_ref, sel_a, side_fn=score_quad,
                       width=4)
        _flash_bounded((2 * full_quads, full_pairs), score_a, pv_a, p_ref, acca_ref, sel_a,
                       side_fn=score_pair)
        _flash_bounded((full_pairs, npair), score_a, pv_a, p_ref, acca_ref, sel_a)

    @pl.when(jnp.logical_not(bounded_a))
    def _():
        _flash_exact(npair, score_a, pv_a, s_ref, acca_ref, sel_a)
        lax.fori_loop(0, full_pairs, score_pair, 0)

    _write_output(acca_ref, sga_ref, oa_ref)
    lax.fori_loop(0, i % 2, score_single, 0)
    score_block(i, True)

    for j in range(nb):
        alive_ref[j] = jnp.where(j < nblk, jnp.full((8, BLK), -1, jnp.int32), 0)

    def radix_round(tiles, r, carry):
        above, thr_u = carry
        ones_here = jnp.zeros((8, BLK), jnp.int32)
        for j in range(tiles):
            ones_here = ones_here + lax.population_count(alive_ref[j] & plane_ref[j, r])
        ones_here = jnp.sum(ones_here, axis=0, keepdims=True)
        take = (above + ones_here) >= topk
        flip = jnp.where(take, 0, -1)
        for j in range(tiles):
            alive_ref[j] = alive_ref[j] & (plane_ref[j, r] ^ flip)
        return (jnp.where(take, above, above + ones_here),
                thr_u | jnp.where(take, jnp.left_shift(1, KEY_BITS - 1 - r), 0))

    def radix_select(tiles):
        zero = jnp.zeros((1, BLK), jnp.int32)
        return lambda: lax.fori_loop(0, KEY_BITS, functools.partial(radix_round, tiles), (zero, zero))

    above, thr_u = lax.cond(nblk <= nb // 2, radix_select(nb // 2), radix_select(nb))
    thr = thr_u ^ INT_MIN

    need = jnp.where(thr > INT_MIN, topk - above, 0).astype(F32)
    lower = jnp.where(qpos <= kpos, 1.0, 0.0).astype(BF16)

    def mask_body(t, carry):
        for j in [4 * t + u for u in range(4)]:
            k = key_ref[j]
            eq = k == thr
            pref = _dot(lower, jnp.where(eq, 1.0, 0.0).astype(BF16))
            sel = (k > thr) | (eq & ((carry + pref) <= need))
            madd_ref[j] = jnp.where(sel, 0.0, NEG)
            carry = carry + pref[BLK - 1:BLK, :]
        return carry

    lax.fori_loop(0, (nblk + 3) // 4, mask_body, jnp.zeros((1, BLK), F32))

    def score(j, h):
        rows = pl.ds(pl.multiple_of(j * BLK, BLK), BLK)
        return (_dot(kn_ref[0, rows, 0:KB2], qm_ref[h])
                + bias_ref[3 * h + jnp.clip(i - j, 0, 2)] + madd_ref[j])

    def pv(j, h, p):
        return _dot(vt_ref[0, j], p)

    far_pairs = jnp.maximum(i - 1, 0) // 2

    def score_far(j, h):
        rows = pl.ds(pl.multiple_of(j * BLK, BLK), BLK)
        return _dot(kn_ref[0, rows, 0:KB2], qm_ref[h]) + madd_ref[j]

    @pl.when(bounded_b)
    def _():
        far_factor = [jnp.exp(jnp.full((1, BLK), rel_ref[REL_BUCKETS - 1, HEADS + h], F32))
                      for h in range(HEADS)]
        far_quads = far_pairs // 2
        _flash_bounded((0, far_quads), score_far, pv, p_ref, acc_ref, pv_scale=far_factor, width=4)
        _flash_bounded((2 * far_quads, far_pairs), score_far, pv, p_ref, acc_ref, pv_scale=far_factor)
        _flash_bounded((far_pairs, npair), score, pv, p_ref, acc_ref)

    @pl.when(jnp.logical_not(bounded_b))
    def _():
        _flash_exact(npair, score, pv, s_ref, acc_ref)

    _write_output(acc_ref, sg_ref, o_ref)


def _attention(rel_bias, bounded, qat, ka, vat, kmean, sga, qbt, qit, wit, kn, vbt, sgb):
    bsz, s, _ = kn.shape
    nb = s // BLK
    topk = min(DSA_TOPK_MAX, s // 4)
    smem = pl.BlockSpec(memory_space=pltpu.SMEM)
    qspec = pl.BlockSpec((1, BLK, WIDTH), lambda b, i: (b, i, 0))
    tile = lambda n: pl.BlockSpec((1, n, BLK), lambda b, i: (b * nb + i, 0, 0))
    row = lambda *shape: pl.BlockSpec((1,) + shape, lambda b, i: (b,) + (0,) * len(shape),
                                      pipeline_mode=pl.Buffered(1))
    out = jax.ShapeDtypeStruct((bsz, s, WIDTH), BF16)
    return pl.pallas_call(
        functools.partial(_attn_kernel, topk=topk),
        grid=(bsz, nb),
        in_specs=[smem, smem,
                  tile(WIDTH), row(s, WIDTH), row(nb, HEADS * VROWS, BLK), row(nb, WIDTH), qspec,
                  tile(WIDTH), tile(QI), tile(WI_ROWS), row(s, KN), row(nb, VROWS, BLK), qspec],
        out_specs=[qspec, qspec],
        out_shape=[out, out],
        scratch_shapes=[pltpu.VMEM((3 * HEADS, BLK, BLK), F32),
                        pltpu.VMEM((3 * HEADS, BLK, BLK), F32),
                        pltpu.VMEM((HEADS, LANES, BLK), BF16),
                        pltpu.VMEM((HEADS, LANES, BLK), BF16),
                        pltpu.VMEM((HEADS, VROWS, BLK), F32),
                        pltpu.VMEM((HEADS, VROWS, BLK), F32),
                        pltpu.VMEM((HEADS, 2 * BLK, BLK), F32),
                        pltpu.VMEM((4 * HEADS, BLK, BLK), BF16),
                        pltpu.VMEM((nb, BLK, BLK), jnp.int32),
                        pltpu.VMEM((nb, KEY_BITS, 8, BLK), jnp.int32),
                        pltpu.VMEM((nb, 8, BLK), jnp.int32),
                        pltpu.VMEM((nb, BLK, BLK), F32)],
        compiler_params=pltpu.CompilerParams(
            dimension_semantics=("arbitrary", "arbitrary"), vmem_limit_bytes=VMEM_LIMIT),
        name="attention",
    )(rel_bias, bounded, qat, ka, vat.reshape(bsz, nb, HEADS * VROWS, BLK), kmean, sga,
      qbt, qit, wit, kn, vbt.reshape(bsz, nb, VROWS, BLK), sgb)


def _outproj_kernel(x_ref, ya_ref, yb_ref, sma_ref, smb_ref, wa_ref, wb_ref, wo_ref, o_ref):
    merged = (sma_ref[...] * _dot(ya_ref[...], wa_ref[...])
              + smb_ref[...] * _dot(yb_ref[...], wb_ref[...]))
    o_ref[...] = x_ref[...] + _dot(merged.astype(BF16), wo_ref[...])


def _outproj(x2, ya, yb, sma, smb, wa, wb, wo):
    m = x2.shape[0]
    row = lambda n: pl.BlockSpec((ROWS_OUT, n), lambda i: (i, 0))
    full = lambda a: pl.BlockSpec(a.shape, lambda i: (0,) * a.ndim)
    return pl.pallas_call(
        _outproj_kernel,
        grid=(m // ROWS_OUT,),
        in_specs=[row(D_MODEL), row(WIDTH), row(WIDTH), row(D_MODEL), row(D_MODEL),
                  full(wa), full(wb), full(wo)],
        out_specs=row(D_MODEL),
        out_shape=jax.ShapeDtypeStruct((m, D_MODEL), F32),
        compiler_params=pltpu.CompilerParams(
            dimension_semantics=("arbitrary",), vmem_limit_bytes=VMEM_LIMIT),
        name="outproj",
    )(x2, ya, yb, sma, smb, wa, wb, wo)


def _pack_w_in(w):
    o = 0
    parts = {}
    for name, n in (("qa", WIDTH), ("ka", WIDTH), ("va", WIDTH), ("ga", WIDTH), ("qb", WIDTH),
                    ("kb", HEAD_DIM), ("vb", HEAD_DIM), ("gb", WIDTH), ("qi", IDX_HEADS * IDX_DIM),
                    ("ki", IDX_DIM), ("wi", IDX_HEADS), ("ma", D_MODEL), ("mb", D_MODEL)):
        parts[name] = w[:, o:o + n]
        o += n
    wn = jnp.concatenate([parts["ka"], parts["kb"], parts["kb"],
                          parts["ki"], parts["ki"], parts["ki"], parts["ki"],
                          parts["ga"], parts["gb"], parts["ma"], parts["mb"]], axis=1).astype(BF16)
    pad = jnp.zeros((w.shape[0], N_TR - RT_WI - IDX_HEADS), w.dtype)
    wt = jnp.concatenate([parts["qa"], parts["qb"], parts["va"], parts["qi"], parts["vb"],
                          parts["wi"], pad], axis=1).T.astype(BF16)
    assert wn.shape[1] == N_NAT and wt.shape[0] == N_TR
    return wn, wt


def _layer(x2, bsz, s, gn, w_in, qn_a, kn_a, qn_b, kn_b, w_br_a, w_br_b, w_out, rel_bias):
    wn, wt = _pack_w_in(w_in)
    ghn = jnp.concatenate([jnp.tile(kn_a, HEADS), jnp.tile(kn_b, 2)])[None, :]
    ght = jnp.concatenate([jnp.tile(qn_a, HEADS), jnp.tile(qn_b, HEADS)])[:, None]
    (ka, kn, sga, sgb, sma, smb, kmean, qat, qbt, vat, qit, vbt, wit) = _inproj(
        x2, gn[None, :], wn, wt, ghn, ght)
    r3 = lambda a: a.reshape(bsz, s, a.shape[-1])
    bounds = jnp.stack([_logit_bound(qn_a, kn_a, rel_bias[:, :HEADS]),
                        _logit_bound(qn_b, kn_b, rel_bias[:, HEADS:])])
    ya, yb = _attention(rel_bias, (bounds <= MAX_PLAIN_LOGIT).astype(jnp.int32),
                        qat, r3(ka), vat, kmean.reshape(bsz, s // BLK, WIDTH), r3(sga),
                        qbt, qit, wit, r3(kn), vbt, r3(sgb))
    return _outproj(x2, ya.reshape(-1, WIDTH), yb.reshape(-1, WIDTH), sma, smb,
                    w_br_a.astype(BF16), w_br_b.astype(BF16), w_out.astype(BF16))


def kernel(x, norm_g, w_in, q_norm_a, k_norm_a, q_norm_b, k_norm_b,
           w_branch_a, w_branch_b, w_out, rel_bias):
    bsz, s, d = x.shape
    assert d == D_MODEL and s % (4 * BLK) == 0 and s // BLK <= 31
    h = x.reshape(bsz * s, d)
    for l in range(norm_g.shape[0]):
        h = _layer(h, bsz, s, norm_g[l], w_in[l], q_norm_a[l], k_norm_a[l], q_norm_b[l],
                   k_norm_b[l], w_branch_a[l], w_branch_b[l], w_out[l], rel_bias)
    return h.reshape(bsz, s, d)
```

```python
import functools
import math

import numpy as np
import jax
import jax.numpy as jnp
from jax import lax
from jax.experimental import pallas as pl
from jax.experimental.pallas import tpu as pltpu

D_MODEL = 1024
HEADS = 8
HEAD_DIM = 64
WIDTH = HEADS * HEAD_DIM
BLK = 256
MOBA_TOPK = 3
IDX_HEADS = 4
IDX_DIM = 64
DSA_TOPK_MAX = 256
REL_BUCKETS = 32
REL_MAX_DIST = 128
EPS = 1e-6
NEG = -1e30
M_INIT = -1e20
MAX_PLAIN_LOGIT = 40.0
INT_MIN = -2 ** 31
KEY_BITS = 32
LANES = 128
VMEM_LIMIT = 56 * 1024 * 1024

F32 = jnp.float32
BF16 = jnp.bfloat16

CN_KA, CN_KB2, CN_KI4, CN_GA, CN_GB, CN_MA, CN_MB = 0, 512, 640, 896, 1408, 1920, 2944
N_NAT = 3968
RT_QA, RT_QB, RT_VA, RT_QI, RT_VB, RT_WI = 0, 512, 1024, 1536, 1792, 1856
N_TR = 1872
WI_ROWS = 8
KB2 = 2 * HEAD_DIM
QI = IDX_HEADS * IDX_DIM
KN = KB2 + QI
ROWS_IN = 512
ROWS_OUT = 1024
ONES_ROWS = 16
VROWS = HEAD_DIM + ONES_ROWS


def _bucket_starts():
    n = np.arange(0, 2 * REL_MAX_DIST + 2)
    max_exact = REL_BUCKETS // 2
    nf = np.maximum(n, 1).astype(np.float64)
    v = np.log(nf / max_exact) / math.log(REL_MAX_DIST / max_exact) * (REL_BUCKETS - max_exact)
    large = np.minimum(max_exact + np.trunc(v).astype(np.int64), REL_BUCKETS - 1)
    bucket = np.where(n < max_exact, n, large)
    frac = np.abs(v - np.round(v))[(n > max_exact) & (n < REL_MAX_DIST)]
    assert frac.min() > 1e-3
    return [int(np.argmax(bucket >= k)) for k in range(REL_BUCKETS)]


_STARTS = _bucket_starts()


def _dot(a, b):
    return jnp.dot(a, b, preferred_element_type=F32)


def _inproj_kernel(x_ref, gn_ref, wn_ref, wt_ref, ghn_ref, ght_ref,
                   ka_ref, kn_ref, sga_ref, sgb_ref, sma_ref, smb_ref, kmean_ref,
                   qat_ref, qbt_ref, vat_ref, qit_ref, vbt_ref, wit_ref):
    x = x_ref[...]
    r = lax.rsqrt(jnp.mean(x * x, axis=-1, keepdims=True) + EPS)
    hf = (x * r) * gn_ref[...]
    h = hf.astype(BF16)
    ht = hf.T.astype(BF16)

    gr = lax.broadcasted_iota(jnp.int32, (BLK, BLK), 0) // HEAD_DIM
    gc = lax.broadcasted_iota(jnp.int32, (BLK, BLK), 1) // HEAD_DIM
    group = jnp.where(gr == gc, 1.0, 0.0).astype(BF16)

    def proj(c0, n):
        return _dot(h, wn_ref[:, c0:c0 + n])

    def proj_t(r0, n):
        return _dot(wt_ref[r0:r0 + n, :], ht)

    def headnorm(y, c0):
        n = y.shape[-1]
        sq = y * y
        hi = sq.astype(BF16)
        lo = (sq - hi.astype(F32)).astype(BF16)
        g = group[:n, :n]
        ss = _dot(hi, g) + _dot(lo, g)
        return (y * lax.rsqrt(ss * (1.0 / HEAD_DIM) + EPS)) * ghn_ref[:, c0:c0 + n]

    def headnorm_t(yt, r0):
        ss = jnp.sum(yt * yt, axis=0, keepdims=True)
        return (yt * lax.rsqrt(ss * (1.0 / HEAD_DIM) + EPS)) * ght_ref[r0:r0 + HEAD_DIM, :]

    ka = proj(CN_KA, WIDTH)
    for c in range(0, WIDTH, BLK):
        kn = headnorm(ka[:, c:c + BLK], CN_KA + c)
        ka_ref[:, c:c + BLK] = kn.astype(BF16)
        for t in range(kn.shape[0] // BLK):
            kmean_ref[t, :, c:c + BLK] = jnp.mean(kn[t * BLK:(t + 1) * BLK], axis=0, keepdims=True)
    sga_ref[...] = jax.nn.silu(proj(CN_GA, WIDTH))
    sgb_ref[...] = jax.nn.silu(proj(CN_GB, WIDTH))
    small = proj(CN_KB2, KN)
    kn_ref[:, 0:KB2] = headnorm(small[:, 0:KB2], CN_KB2).astype(BF16)
    kn_ref[:, KB2:KN] = small[:, KB2:KN].astype(BF16)
    for c in range(0, D_MODEL, WIDTH):
        sma_ref[:, c:c + WIDTH] = jax.nn.sigmoid(proj(CN_MA + c, WIDTH))
        smb_ref[:, c:c + WIDTH] = jax.nn.sigmoid(proj(CN_MB + c, WIDTH))

    def put_t(ref, r0, val):
        for t in range(ROWS_IN // BLK):
            ref[t, r0:r0 + val.shape[0], :] = val[:, t * BLK:(t + 1) * BLK]

    scale = HEAD_DIM ** -0.5
    ones = jnp.ones((ONES_ROWS, ROWS_IN), BF16)
    qat = proj_t(RT_QA, WIDTH)
    qbt = proj_t(RT_QB, WIDTH)
    vat = proj_t(RT_VA, WIDTH)
    for hd in range(HEADS):
        r0 = hd * HEAD_DIM
        put_t(qat_ref, r0, (headnorm_t(qat[r0:r0 + HEAD_DIM], RT_QA + r0) * scale).astype(BF16))
        put_t(qbt_ref, r0, (headnorm_t(qbt[r0:r0 + HEAD_DIM], RT_QB + r0) * scale).astype(BF16))
        put_t(vat_ref, hd * VROWS, vat[r0:r0 + HEAD_DIM].astype(BF16))
        put_t(vat_ref, hd * VROWS + HEAD_DIM, ones)
    put_t(qit_ref, 0, (proj_t(RT_QI, QI) * (IDX_DIM ** -0.5)).astype(BF16))
    tail = proj_t(RT_VB, N_TR - RT_VB)
    put_t(vbt_ref, 0, tail[0:HEAD_DIM].astype(BF16))
    put_t(vbt_ref, HEAD_DIM, ones)
    put_t(wit_ref, 0, tail[RT_WI - RT_VB:RT_WI - RT_VB + WI_ROWS] * (IDX_HEADS ** -0.5))


def _inproj(x2, gn, wn, wt, ghn, ght):
    m = x2.shape[0]
    nt = m // BLK
    tpb = ROWS_IN // BLK
    row = lambda n: pl.BlockSpec((ROWS_IN, n), lambda i: (i, 0))
    full = lambda a: pl.BlockSpec(a.shape, lambda i: (0,) * a.ndim)
    tile = lambda n: pl.BlockSpec((tpb, n, BLK), lambda i: (i, 0, 0))
    out_shapes = [
        jax.ShapeDtypeStruct((m, WIDTH), BF16),
        jax.ShapeDtypeStruct((m, KN), BF16),
        jax.ShapeDtypeStruct((m, WIDTH), F32),
        jax.ShapeDtypeStruct((m, WIDTH), F32),
        jax.ShapeDtypeStruct((m, D_MODEL), F32),
        jax.ShapeDtypeStruct((m, D_MODEL), F32),
        jax.ShapeDtypeStruct((nt, 1, WIDTH), F32),
        jax.ShapeDtypeStruct((nt, WIDTH, BLK), BF16),
        jax.ShapeDtypeStruct((nt, WIDTH, BLK), BF16),
        jax.ShapeDtypeStruct((nt, HEADS * VROWS, BLK), BF16),
        jax.ShapeDtypeStruct((nt, QI, BLK), BF16),
        jax.ShapeDtypeStruct((nt, VROWS, BLK), BF16),
        jax.ShapeDtypeStruct((nt, WI_ROWS, BLK), F32),
    ]
    out_specs = [row(WIDTH), row(KN), row(WIDTH), row(WIDTH), row(D_MODEL), row(D_MODEL),
                 pl.BlockSpec((tpb, 1, WIDTH), lambda i: (i, 0, 0)),
                 tile(WIDTH), tile(WIDTH), tile(HEADS * VROWS), tile(QI), tile(VROWS), tile(WI_ROWS)]
    return pl.pallas_call(
        _inproj_kernel,
        grid=(nt // tpb,),
        in_specs=[row(D_MODEL), full(gn), full(wn), full(wt), full(ghn), full(ght)],
        out_specs=out_specs,
        out_shape=out_shapes,
        compiler_params=pltpu.CompilerParams(
            dimension_semantics=("arbitrary",), vmem_limit_bytes=VMEM_LIMIT),
        name="inproj",
    )(x2, gn, wn, wt, ghn, ght)


def _build_bias_tiles(rel_ref, bias_ref, head0):
    kpos = lax.broadcasted_iota(jnp.int32, (BLK, BLK), 0)
    qpos = lax.broadcasted_iota(jnp.int32, (BLK, BLK), 1)
    for kind in range(2):
        dist = qpos - kpos + BLK * kind
        for h in range(HEADS):
            t = jnp.full((BLK, BLK), rel_ref[0, head0 + h], F32)
            for k in range(1, REL_BUCKETS):
                t = jnp.where(dist >= _STARTS[k], rel_ref[k, head0 + h], t)
            if kind == 0:
                t = jnp.where(dist >= 0, t, NEG)
            bias_ref[3 * h + kind] = t
    for h in range(HEADS):
        bias_ref[3 * h + 2] = jnp.full((BLK, BLK), rel_ref[REL_BUCKETS - 1, head0 + h], F32)


def _store_masked_queries(qt_ref, qm_ref):
    rowh = lax.broadcasted_iota(jnp.int32, (LANES, BLK), 0)
    for p in range(HEADS // 2):
        q2 = qt_ref[0, LANES * p:LANES * (p + 1), :]
        for hh in range(2):
            qm_ref[2 * p + hh] = jnp.where((rowh >= HEAD_DIM) == bool(hh), q2, jnp.zeros_like(q2))


def _flash_exact(npair, score_fn, pv_fn, s_ref, acc_ref, sel_fn=None):
    def stage(t, ms):
        out = []
        for h in range(HEADS):
            m_new = ms[h]
            for u in range(2):
                j = 2 * t + u
                s = score_fn(j, h)
                s_ref[h, u * BLK:(u + 1) * BLK, :] = s
                cand = jnp.max(s, axis=0, keepdims=True)
                if sel_fn is not None:
                    cand = jnp.where(sel_fn(j, h), cand, M_INIT)
                m_new = jnp.maximum(m_new, cand)
            out.append(m_new)
        return tuple(out)

    def finish(t, ms_old, ms_new):
        for h in range(HEADS):
            acc = jnp.exp(ms_old[h] - ms_new[h]) * acc_ref[h]
            for u in range(2):
                j = 2 * t + u
                p = jnp.exp(s_ref[h, u * BLK:(u + 1) * BLK, :] - ms_new[h])
                pv = pv_fn(j, h, p.astype(BF16))
                acc = acc + (pv if sel_fn is None else jnp.where(sel_fn(j, h), pv, 0.0))
            acc_ref[h] = acc

    def step(t, ms):
        m_new = stage(t, ms)
        finish(t, ms, m_new)
        return m_new

    lax.fori_loop(0, npair, step, tuple(jnp.full((1, BLK), M_INIT, F32) for _ in range(HEADS)))


def _flash_bounded(pairs, score_fn, pv_fn, p_ref, acc_ref, sel_fn=None, side_fn=None, pv_scale=None,
                   width=2):
    def step(t, c):
        for h in range(HEADS):
            for u in range(width):
                p_ref[width * h + u] = jnp.exp(score_fn(width * t + u, h)).astype(BF16)
        if side_fn is not None:
            side_fn(t)
        for h in range(HEADS):
            new = None
            for u in range(width):
                j = width * t + u
                pv = pv_fn(j, h, p_ref[width * h + u])
                pv = pv if sel_fn is None else jnp.where(sel_fn(j, h), pv, 0.0)
                new = pv if new is None else new + pv
            acc_ref[h] = acc_ref[h] + (new if pv_scale is None else new * pv_scale[h])
        return c

    lax.fori_loop(pairs[0], pairs[1], step, 0)


def _logit_bound(q_gain, k_gain, bias):
    return (1.01 * HEAD_DIM ** 0.5 * jnp.max(jnp.abs(q_gain)) * jnp.max(jnp.abs(k_gain))
            + jnp.max(jnp.abs(bias)))


def _write_output(acc_ref, sg_ref, o_ref):
    def head_out(h):
        return acc_ref[h, 0:HEAD_DIM, :] / acc_ref[h, HEAD_DIM:HEAD_DIM + 1, :]

    for p in range(HEADS // 2):
        ot = jnp.concatenate([head_out(2 * p), head_out(2 * p + 1)], axis=0)
        cs = slice(LANES * p, LANES * (p + 1))
        o_ref[0, :, cs] = (ot.T * sg_ref[0, :, cs]).astype(BF16)


def _bit_planes(u):
    x = [u[8 * a:8 * a + 8, :] for a in range(KEY_BITS)]
    j, m = 16, 0x0000FFFF
    while j:
        for k in range(KEY_BITS):
            if not k & j:
                t = (x[k] ^ jnp.right_shift(x[k + j], j)) & m
                x[k] = x[k] ^ t
                x[k + j] = x[k + j] ^ jnp.left_shift(t, j)
        j >>= 1
        m ^= (m << j) & 0xFFFFFFFF
    return x


def _attn_kernel(rel_ref, bounded_ref,
                 qat_ref, ka_ref, vat_ref, kmean_ref, sga_ref,
                 qt_ref, qit_ref, wit_ref, kn_ref, vt_ref, sg_ref,
                 oa_ref, o_ref,
                 biasa_ref, bias_ref, qma_ref, qm_ref, acca_ref, acc_ref, s_ref, p_ref,
                 key_ref, plane_ref, alive_ref, madd_ref, *, topk):
    b = pl.program_id(0)
    i = pl.program_id(1)
    nblk = i + 1
    nb = key_ref.shape[0]
    bounded_a = bounded_ref[0] == 1
    bounded_b = bounded_ref[1] == 1

    @pl.when((b == 0) & (i == 0))
    def _():
        _build_bias_tiles(rel_ref, biasa_ref, 0)
        _build_bias_tiles(rel_ref, bias_ref, HEADS)
        plane_ref[...] = jnp.zeros_like(plane_ref)

    _store_masked_queries(qat_ref, qma_ref)
    _store_masked_queries(qt_ref, qm_ref)
    acca_ref[...] = jnp.zeros_like(acca_ref)
    acc_ref[...] = jnp.zeros_like(acc_ref)

    nidx = lax.broadcasted_iota(jnp.int32, (nb, BLK), 0)
    sel_bits = []
    for h in range(HEADS):
        cs = slice(LANES * (h // 2), LANES * (h // 2 + 1))
        gate = _dot(kmean_ref[0, :, cs].astype(BF16), qma_ref[h])
        g = jnp.where(nidx < i, gate, -jnp.inf)
        bits = jnp.left_shift(1, jnp.full((1, BLK), i, jnp.int32))
        for _ in range(MOBA_TOPK):
            mx = jnp.max(g, axis=0, keepdims=True)
            am = jnp.min(jnp.where(g == mx, nidx, nb), axis=0, keepdims=True)
            bits = bits | jnp.where(mx > -jnp.inf, jnp.left_shift(1, am), 0)
            g = jnp.where(nidx == am, -jnp.inf, g)
        sel_bits.append(bits)

    def score_a(j, h):
        rows = pl.ds(pl.multiple_of(j * BLK, BLK), BLK)
        cs = slice(LANES * (h // 2), LANES * (h // 2 + 1))
        return _dot(ka_ref[0, rows, cs], qma_ref[h]) + biasa_ref[3 * h + jnp.clip(i - j, 0, 2)]

    def pv_a(j, h, p):
        return _dot(vat_ref[0, j, h * VROWS:(h + 1) * VROWS, :], p)

    def sel_a(j, h):
        return (jnp.right_shift(sel_bits[h], j) & 1) == 1

    kpos = lax.broadcasted_iota(jnp.int32, (BLK, BLK), 0)
    qpos = lax.broadcasted_iota(jnp.int32, (BLK, BLK), 1)

    qit = qit_ref[0]
    wit = wit_ref[0]
    qims = [jnp.where((kpos // IDX_DIM) == h, qit, jnp.zeros_like(qit)) for h in range(IDX_HEADS)]
    whs = [wit[h:h + 1, :] for h in range(IDX_HEADS)]

    def score_block(j, causal):
        rows = pl.ds(pl.multiple_of(j * BLK, BLK), BLK)
        ki4 = kn_ref[0, rows, KB2:KN]
        score = jnp.zeros((BLK, BLK), F32)
        for h in range(IDX_HEADS):
            score = score + jnp.maximum(_dot(ki4, qims[h]), 0.0) * whs[h]
        bits = lax.bitcast_convert_type(score, jnp.int32)
        key = bits ^ (jnp.right_shift(bits, 31) & 0x7FFFFFFF)
        if causal:
            key = jnp.where(kpos <= qpos, key, INT_MIN)
        key_ref[j] = key
        planes = _bit_planes(key ^ INT_MIN)
        for r in range(KEY_BITS):
            plane_ref[j, r] = planes[r]

    for extra in range(3):
        key_ref[jnp.minimum(nblk + extra, nb - 1)] = jnp.full((BLK, BLK), INT_MIN, jnp.int32)

    def score_pair(t, c=0):
        score_block(2 * t, False)
        score_block(2 * t + 1, False)
        return c

    def score_single(_, c):
        score_block(i - 1, False)
        return c

    npair = (nblk + 1) // 2
    full_pairs = i // 2

    def score_quad(t):
        score_pair(2 * t)
        score_pair(2 * t + 1)

    @pl.when(bounded_a)
    def _():
        full_quads = full_pairs // 2
        _flash_bounded((0, full_quads), score_a, pv_a, p_ref, acca_ref, sel_a, side_fn=score_quad,
                       width=4)
        _flash_bounded((2 * full_quads, full_pairs), score_a, pv_a, p_ref, acca_ref, sel_a,
                       side_fn=score_pair)
        _flash_bounded((full_pairs, npair), score_a, pv_a, p_ref, acca_ref, sel_a)

    @pl.when(jnp.logical_not(bounded_a))
    def _():
        _flash_exact(npair, score_a, pv_a, s_ref, acca_ref, sel_a)
        lax.fori_loop(0, full_pairs, score_pair, 0)

    _write_output(acca_ref, sga_ref, oa_ref)
    lax.fori_loop(0, i % 2, score_single, 0)
    score_block(i, True)

    for j in range(nb):
        alive_ref[j] = jnp.where(j < nblk, jnp.full((8, BLK), -1, jnp.int32), 0)

    def radix_round(tiles, r, carry):
        above, thr_u = carry
        ones_here = jnp.zeros((8, BLK), jnp.int32)
        for j in range(tiles):
            ones_here = ones_here + lax.population_count(alive_ref[j] & plane_ref[j, r])
        ones_here = jnp.sum(ones_here, axis=0, keepdims=True)
        take = (above + ones_here) >= topk
        flip = jnp.where(take, 0, -1)
        for j in range(tiles):
            alive_ref[j] = alive_ref[j] & (plane_ref[j, r] ^ flip)
        return (jnp.where(take, above, above + ones_here),
                thr_u | jnp.where(take, jnp.left_shift(1, KEY_BITS - 1 - r), 0))

    def radix_select(tiles):
        zero = jnp.zeros((1, BLK), jnp.int32)
        return lambda: lax.fori_loop(0, KEY_BITS, functools.partial(radix_round, tiles), (zero, zero))

    quarter = nb // 4
    above, thr_u = lax.switch((nblk - 1) // quarter,
                              [radix_select(quarter * (n + 1)) for n in range(4)])
    thr = thr_u ^ INT_MIN

    need = jnp.where(thr > INT_MIN, topk - above, 0).astype(F32)
    lower = jnp.where(qpos <= kpos, 1.0, 0.0).astype(BF16)

    def mask_body(t, carry):
        for j in [4 * t + u for u in range(4)]:
            k = key_ref[j]
            eq = k == thr
            pref = _dot(lower, jnp.where(eq, 1.0, 0.0).astype(BF16))
            sel = (k > thr) | (eq & ((carry + pref) <= need))
            madd_ref[j] = jnp.where(sel, 0.0, NEG)
            carry = carry + pref[BLK - 1:BLK, :]
        return carry

    lax.fori_loop(0, (nblk + 3) // 4, mask_body, jnp.zeros((1, BLK), F32))

    def score(j, h):
        rows = pl.ds(pl.multiple_of(j * BLK, BLK), BLK)
        return (_dot(kn_ref[0, rows, 0:KB2], qm_ref[h])
                + bias_ref[3 * h + jnp.clip(i - j, 0, 2)] + madd_ref[j])

    def pv(j, h, p):
        return _dot(vt_ref[0, j], p)

    far_pairs = jnp.maximum(i - 1, 0) // 2

    def score_far(j, h):
        rows = pl.ds(pl.multiple_of(j * BLK, BLK), BLK)
        return _dot(kn_ref[0, rows, 0:KB2], qm_ref[h]) + madd_ref[j]

    @pl.when(bounded_b)
    def _():
        far_factor = [jnp.exp(jnp.full((1, BLK), rel_ref[REL_BUCKETS - 1, HEADS + h], F32))
                      for h in range(HEADS)]
        far_quads = far_pairs // 2
        _flash_bounded((0, far_quads), score_far, pv, p_ref, acc_ref, pv_scale=far_factor, width=4)
        _flash_bounded((2 * far_quads, far_pairs), score_far, pv, p_ref, acc_ref, pv_scale=far_factor)
        _flash_bounded((far_pairs, npair), score, pv, p_ref, acc_ref)

    @pl.when(jnp.logical_not(bounded_b))
    def _():
        _flash_exact(npair, score, pv, s_ref, acc_ref)

    _write_output(acc_ref, sg_ref, o_ref)


def _attention(rel_bias, bounded, qat, ka, vat, kmean, sga, qbt, qit, wit, kn, vbt, sgb):
    bsz, s, _ = kn.shape
    nb = s // BLK
    topk = min(DSA_TOPK_MAX, s // 4)
    smem = pl.BlockSpec(memory_space=pltpu.SMEM)
    qspec = pl.BlockSpec((1, BLK, WIDTH), lambda b, i: (b, i, 0))
    tile = lambda n: pl.BlockSpec((1, n, BLK), lambda b, i: (b * nb + i, 0, 0))
    row = lambda *shape: pl.BlockSpec((1,) + shape, lambda b, i: (b,) + (0,) * len(shape),
                                      pipeline_mode=pl.Buffered(1))
    out = jax.ShapeDtypeStruct((bsz, s, WIDTH), BF16)
    return pl.pallas_call(
        functools.partial(_attn_kernel, topk=topk),
        grid=(bsz, nb),
        in_specs=[smem, smem,
                  tile(WIDTH), row(s, WIDTH), row(nb, HEADS * VROWS, BLK), row(nb, WIDTH), qspec,
                  tile(WIDTH), tile(QI), tile(WI_ROWS), row(s, KN), row(nb, VROWS, BLK), qspec],
        out_specs=[qspec, qspec],
        out_shape=[out, out],
        scratch_shapes=[pltpu.VMEM((3 * HEADS, BLK, BLK), F32),
                        pltpu.VMEM((3 * HEADS, BLK, BLK), F32),
                        pltpu.VMEM((HEADS, LANES, BLK), BF16),
                        pltpu.VMEM((HEADS, LANES, BLK), BF16),
                        pltpu.VMEM((HEADS, VROWS, BLK), F32),
                        pltpu.VMEM((HEADS, VROWS, BLK), F32),
                        pltpu.VMEM((HEADS, 2 * BLK, BLK), F32),
                        pltpu.VMEM((4 * HEADS, BLK, BLK), BF16),
                        pltpu.VMEM((nb, BLK, BLK), jnp.int32),
                        pltpu.VMEM((nb, KEY_BITS, 8, BLK), jnp.int32),
                        pltpu.VMEM((nb, 8, BLK), jnp.int32),
                        pltpu.VMEM((nb, BLK, BLK), F32)],
        compiler_params=pltpu.CompilerParams(
            dimension_semantics=("arbitrary", "arbitrary"), vmem_limit_bytes=VMEM_LIMIT),
        name="attention",
    )(rel_bias, bounded, qat, ka, vat.reshape(bsz, nb, HEADS * VROWS, BLK), kmean, sga,
      qbt, qit, wit, kn, vbt.reshape(bsz, nb, VROWS, BLK), sgb)


def _outproj_kernel(x_ref, ya_ref, yb_ref, sma_ref, smb_ref, wa_ref, wb_ref, wo_ref, o_ref):
    merged = (sma_ref[...] * _dot(ya_ref[...], wa_ref[...])
              + smb_ref[...] * _dot(yb_ref[...], wb_ref[...]))
    o_ref[...] = x_ref[...] + _dot(merged.astype(BF16), wo_ref[...])


def _outproj(x2, ya, yb, sma, smb, wa, wb, wo):
    m = x2.shape[0]
    row = lambda n: pl.BlockSpec((ROWS_OUT, n), lambda i: (i, 0))
    full = lambda a: pl.BlockSpec(a.shape, lambda i: (0,) * a.ndim)
    return pl.pallas_call(
        _outproj_kernel,
        grid=(m // ROWS_OUT,),
        in_specs=[row(D_MODEL), row(WIDTH), row(WIDTH), row(D_MODEL), row(D_MODEL),
                  full(wa), full(wb), full(wo)],
        out_specs=row(D_MODEL),
        out_shape=jax.ShapeDtypeStruct((m, D_MODEL), F32),
        compiler_params=pltpu.CompilerParams(
            dimension_semantics=("arbitrary",), vmem_limit_bytes=VMEM_LIMIT),
        name="outproj",
    )(x2, ya, yb, sma, smb, wa, wb, wo)


def _pack_w_in(w):
    o = 0
    parts = {}
    for name, n in (("qa", WIDTH), ("ka", WIDTH), ("va", WIDTH), ("ga", WIDTH), ("qb", WIDTH),
                    ("kb", HEAD_DIM), ("vb", HEAD_DIM), ("gb", WIDTH), ("qi", IDX_HEADS * IDX_DIM),
                    ("ki", IDX_DIM), ("wi", IDX_HEADS), ("ma", D_MODEL), ("mb", D_MODEL)):
        parts[name] = w[:, o:o + n]
        o += n
    wn = jnp.concatenate([parts["ka"], parts["kb"], parts["kb"],
                          parts["ki"], parts["ki"], parts["ki"], parts["ki"],
                          parts["ga"], parts["gb"], parts["ma"], parts["mb"]], axis=1).astype(BF16)
    pad = jnp.zeros((w.shape[0], N_TR - RT_WI - IDX_HEADS), w.dtype)
    wt = jnp.concatenate([parts["qa"], parts["qb"], parts["va"], parts["qi"], parts["vb"],
                          parts["wi"], pad], axis=1).T.astype(BF16)
    assert wn.shape[1] == N_NAT and wt.shape[0] == N_TR
    return wn, wt


def _layer(x2, bsz, s, gn, w_in, qn_a, kn_a, qn_b, kn_b, w_br_a, w_br_b, w_out, rel_bias):
    wn, wt = _pack_w_in(w_in)
    ghn = jnp.concatenate([jnp.tile(kn_a, HEADS), jnp.tile(kn_b, 2)])[None, :]
    ght = jnp.concatenate([jnp.tile(qn_a, HEADS), jnp.tile(qn_b, HEADS)])[:, None]
    (ka, kn, sga, sgb, sma, smb, kmean, qat, qbt, vat, qit, vbt, wit) = _inproj(
        x2, gn[None, :], wn, wt, ghn, ght)
    r3 = lambda a: a.reshape(bsz, s, a.shape[-1])
    bounds = jnp.stack([_logit_bound(qn_a, kn_a, rel_bias[:, :HEADS]),
                        _logit_bound(qn_b, kn_b, rel_bias[:, HEADS:])])
    ya, yb = _attention(rel_bias, (bounds <= MAX_PLAIN_LOGIT).astype(jnp.int32),
                        qat, r3(ka), vat, kmean.reshape(bsz, s // BLK, WIDTH), r3(sga),
                        qbt, qit, wit, r3(kn), vbt, r3(sgb))
    return _outproj(x2, ya.reshape(-1, WIDTH), yb.reshape(-1, WIDTH), sma, smb,
                    w_br_a.astype(BF16), w_br_b.astype(BF16), w_out.astype(BF16))


def kernel(x, norm_g, w_in, q_norm_a, k_norm_a, q_norm_b, k_norm_b,
           w_branch_a, w_branch_b, w_out, rel_bias):
    bsz, s, d = x.shape
    assert d == D_MODEL and s % (4 * BLK) == 0 and s // BLK <= 31
    h = x.reshape(bsz * s, d)
    for l in range(norm_g.shape[0]):
        h = _layer(h, bsz, s, norm_g[l], w_in[l], q_norm_a[l], k_norm_a[l], q_norm_b[l],
                   k_norm_b[l], w_branch_a[l], w_branch_b[l], w_out[l], rel_bias)
    return h.reshape(bsz, s, d)
```

```python
import functools
import math

import numpy as np
import jax
import jax.numpy as jnp
from jax import lax
from jax.experimental import pallas as pl
from jax.experimental.pallas import tpu as pltpu

D_MODEL = 1024
HEADS = 8
HEAD_DIM = 64
WIDTH = HEADS * HEAD_DIM
BLK = 256
MOBA_TOPK = 3
IDX_HEADS = 4
IDX_DIM = 64
DSA_TOPK_MAX = 256
REL_BUCKETS = 32
REL_MAX_DIST = 128
EPS = 1e-6
NEG = -1e30
M_INIT = -1e20
MAX_PLAIN_LOGIT = 40.0
INT_MIN = -2 ** 31
KEY_BITS = 32
LANES = 128
VMEM_LIMIT = 56 * 1024 * 1024

F32 = jnp.float32
BF16 = jnp.bfloat16

CN_KA, CN_KB2, CN_KI4, CN_GA, CN_GB = 0, 512, 640, 896, 1408
N_NAT = 1920
RT_QA, RT_QB, RT_VA, RT_QI, RT_VB, RT_WI = 0, 512, 1024, 1536, 1792, 1856
N_TR = 1872
WI_ROWS = 8
KB2 = 2 * HEAD_DIM
QI = IDX_HEADS * IDX_DIM
KN = KB2 + QI
ROWS_IN = 512
ROWS_OUT = 512
ONES_ROWS = 16
VROWS = HEAD_DIM + ONES_ROWS


def _bucket_starts():
    n = np.arange(0, 2 * REL_MAX_DIST + 2)
    max_exact = REL_BUCKETS // 2
    nf = np.maximum(n, 1).astype(np.float64)
    v = np.log(nf / max_exact) / math.log(REL_MAX_DIST / max_exact) * (REL_BUCKETS - max_exact)
    large = np.minimum(max_exact + np.trunc(v).astype(np.int64), REL_BUCKETS - 1)
    bucket = np.where(n < max_exact, n, large)
    frac = np.abs(v - np.round(v))[(n > max_exact) & (n < REL_MAX_DIST)]
    assert frac.min() > 1e-3
    return [int(np.argmax(bucket >= k)) for k in range(REL_BUCKETS)]


_STARTS = _bucket_starts()


def _dot(a, b):
    return jnp.dot(a, b, preferred_element_type=F32)


def _inproj_kernel(x_ref, gn_ref, wn_ref, wt_ref, ghn_ref, ght_ref,
                   ka_ref, kn_ref, sga_ref, sgb_ref, kmean_ref,
                   qat_ref, qbt_ref, vat_ref, qit_ref, vbt_ref, wit_ref):
    x = x_ref[...]
    r = lax.rsqrt(jnp.mean(x * x, axis=-1, keepdims=True) + EPS)
    hf = (x * r) * gn_ref[...]
    h = hf.astype(BF16)
    ht = hf.T.astype(BF16)

    gr = lax.broadcasted_iota(jnp.int32, (BLK, BLK), 0) // HEAD_DIM
    gc = lax.broadcasted_iota(jnp.int32, (BLK, BLK), 1) // HEAD_DIM
    group = jnp.where(gr == gc, 1.0, 0.0).astype(BF16)

    def proj(c0, n):
        return _dot(h, wn_ref[:, c0:c0 + n])

    def proj_t(r0, n):
        return _dot(wt_ref[r0:r0 + n, :], ht)

    def headnorm(y, c0):
        n = y.shape[-1]
        sq = y * y
        hi = sq.astype(BF16)
        lo = (sq - hi.astype(F32)).astype(BF16)
        g = group[:n, :n]
        ss = _dot(hi, g) + _dot(lo, g)
        return (y * lax.rsqrt(ss * (1.0 / HEAD_DIM) + EPS)) * ghn_ref[:, c0:c0 + n]

    def headnorm_t(yt, r0):
        ss = jnp.sum(yt * yt, axis=0, keepdims=True)
        return (yt * lax.rsqrt(ss * (1.0 / HEAD_DIM) + EPS)) * ght_ref[r0:r0 + HEAD_DIM, :]

    ka = proj(CN_KA, WIDTH)
    for c in range(0, WIDTH, BLK):
        kn = headnorm(ka[:, c:c + BLK], CN_KA + c)
        ka_ref[:, c:c + BLK] = kn.astype(BF16)
        for t in range(kn.shape[0] // BLK):
            kmean_ref[t, :, c:c + BLK] = jnp.mean(kn[t * BLK:(t + 1) * BLK], axis=0, keepdims=True)
    sga_ref[...] = jax.nn.silu(proj(CN_GA, WIDTH))
    sgb_ref[...] = jax.nn.silu(proj(CN_GB, WIDTH))
    small = proj(CN_KB2, KN)
    kn_ref[:, 0:KB2] = headnorm(small[:, 0:KB2], CN_KB2).astype(BF16)
    kn_ref[:, KB2:KN] = small[:, KB2:KN].astype(BF16)

    def put_t(ref, r0, val):
        for t in range(ROWS_IN // BLK):
            ref[t, r0:r0 + val.shape[0], :] = val[:, t * BLK:(t + 1) * BLK]

    scale = HEAD_DIM ** -0.5
    ones = jnp.ones((ONES_ROWS, ROWS_IN), BF16)
    qat = proj_t(RT_QA, WIDTH)
    qbt = proj_t(RT_QB, WIDTH)
    vat = proj_t(RT_VA, WIDTH)
    for hd in range(HEADS):
        r0 = hd * HEAD_DIM
        put_t(qat_ref, r0, (headnorm_t(qat[r0:r0 + HEAD_DIM], RT_QA + r0) * scale).astype(BF16))
        put_t(qbt_ref, r0, (headnorm_t(qbt[r0:r0 + HEAD_DIM], RT_QB + r0) * scale).astype(BF16))
        put_t(vat_ref, hd * VROWS, vat[r0:r0 + HEAD_DIM].astype(BF16))
        put_t(vat_ref, hd * VROWS + HEAD_DIM, ones)
    put_t(qit_ref, 0, (proj_t(RT_QI, QI) * (IDX_DIM ** -0.5)).astype(BF16))
    tail = proj_t(RT_VB, N_TR - RT_VB)
    put_t(vbt_ref, 0, tail[0:HEAD_DIM].astype(BF16))
    put_t(vbt_ref, HEAD_DIM, ones)
    put_t(wit_ref, 0, tail[RT_WI - RT_VB:RT_WI - RT_VB + WI_ROWS] * (IDX_HEADS ** -0.5))


def _inproj(x2, gn, wn, wt, ghn, ght):
    m = x2.shape[0]
    nt = m // BLK
    tpb = ROWS_IN // BLK
    row = lambda n: pl.BlockSpec((ROWS_IN, n), lambda i: (i, 0))
    full = lambda a: pl.BlockSpec(a.shape, lambda i: (0,) * a.ndim)
    tile = lambda n: pl.BlockSpec((tpb, n, BLK), lambda i: (i, 0, 0))
    out_shapes = [
        jax.ShapeDtypeStruct((m, WIDTH), BF16),
        jax.ShapeDtypeStruct((m, KN), BF16),
        jax.ShapeDtypeStruct((m, WIDTH), F32),
        jax.ShapeDtypeStruct((m, WIDTH), F32),
        jax.ShapeDtypeStruct((nt, 1, WIDTH), F32),
        jax.ShapeDtypeStruct((nt, WIDTH, BLK), BF16),
        jax.ShapeDtypeStruct((nt, WIDTH, BLK), BF16),
        jax.ShapeDtypeStruct((nt, HEADS * VROWS, BLK), BF16),
        jax.ShapeDtypeStruct((nt, QI, BLK), BF16),
        jax.ShapeDtypeStruct((nt, VROWS, BLK), BF16),
        jax.ShapeDtypeStruct((nt, WI_ROWS, BLK), F32),
    ]
    out_specs = [row(WIDTH), row(KN), row(WIDTH), row(WIDTH),
                 pl.BlockSpec((tpb, 1, WIDTH), lambda i: (i, 0, 0)),
                 tile(WIDTH), tile(WIDTH), tile(HEADS * VROWS), tile(QI), tile(VROWS), tile(WI_ROWS)]
    return pl.pallas_call(
        _inproj_kernel,
        grid=(nt // tpb,),
        in_specs=[row(D_MODEL), full(gn), full(wn), full(wt), full(ghn), full(ght)],
        out_specs=out_specs,
        out_shape=out_shapes,
        compiler_params=pltpu.CompilerParams(
            dimension_semantics=("arbitrary",), vmem_limit_bytes=VMEM_LIMIT),
        name="inproj",
    )(x2, gn, wn, wt, ghn, ght)


def _build_bias_tiles(rel_ref, bias_ref, head0):
    kpos = lax.broadcasted_iota(jnp.int32, (BLK, BLK), 0)
    qpos = lax.broadcasted_iota(jnp.int32, (BLK, BLK), 1)
    for kind in range(2):
        dist = qpos - kpos + BLK * kind
        for h in range(HEADS):
            t = jnp.full((BLK, BLK), rel_ref[0, head0 + h], F32)
            for k in range(1, REL_BUCKETS):
                t = jnp.where(dist >= _STARTS[k], rel_ref[k, head0 + h], t)
            if kind == 0:
                t = jnp.where(dist >= 0, t, NEG)
            bias_ref[3 * h + kind] = t
    for h in range(HEADS):
        bias_ref[3 * h + 2] = jnp.full((BLK, BLK), rel_ref[REL_BUCKETS - 1, head0 + h], F32)


def _store_masked_queries(qt_ref, qm_ref):
    rowh = lax.broadcasted_iota(jnp.int32, (LANES, BLK), 0)
    for p in range(HEADS // 2):
        q2 = qt_ref[0, LANES * p:LANES * (p + 1), :]
        for hh in range(2):
            qm_ref[2 * p + hh] = jnp.where((rowh >= HEAD_DIM) == bool(hh), q2, jnp.zeros_like(q2))


def _flash_exact(npair, score_fn, pv_fn, s_ref, acc_ref, sel_fn=None):
    def stage(t, ms):
        out = []
        for h in range(HEADS):
            m_new = ms[h]
            for u in range(2):
                j = 2 * t + u
                s = score_fn(j, h)
                s_ref[h, u * BLK:(u + 1) * BLK, :] = s
                cand = jnp.max(s, axis=0, keepdims=True)
                if sel_fn is not None:
                    cand = jnp.where(sel_fn(j, h), cand, M_INIT)
                m_new = jnp.maximum(m_new, cand)
            out.append(m_new)
        return tuple(out)

    def finish(t, ms_old, ms_new):
        for h in range(HEADS):
            acc = jnp.exp(ms_old[h] - ms_new[h]) * acc_ref[h]
            for u in range(2):
                j = 2 * t + u
                p = jnp.exp(s_ref[h, u * BLK:(u + 1) * BLK, :] - ms_new[h])
                pv = pv_fn(j, h, p.astype(BF16))
                acc = acc + (pv if sel_fn is None else jnp.where(sel_fn(j, h), pv, 0.0))
            acc_ref[h] = acc

    def step(t, ms):
        m_new = stage(t, ms)
        finish(t, ms, m_new)
        return m_new

    lax.fori_loop(0, npair, step, tuple(jnp.full((1, BLK), M_INIT, F32) for _ in range(HEADS)))


def _flash_bounded(pairs, score_fn, pv_fn, p_ref, acc_ref, sel_fn=None, side_fn=None, pv_scale=None,
                   width=2):
    def step(t, c):
        for h in range(HEADS):
            for u in range(width):
                p_ref[width * h + u] = jnp.exp(score_fn(width * t + u, h)).astype(BF16)
        if side_fn is not None:
            side_fn(t)
        for h in range(HEADS):
            new = None
            for u in range(width):
                j = width * t + u
                pv = pv_fn(j, h, p_ref[width * h + u])
                pv = pv if sel_fn is None else jnp.where(sel_fn(j, h), pv, 0.0)
                new = pv if new is None else new + pv
            acc_ref[h] = acc_ref[h] + (new if pv_scale is None else new * pv_scale[h])
        return c

    lax.fori_loop(pairs[0], pairs[1], step, 0)


def _logit_bound(q_gain, k_gain, bias):
    return (1.01 * HEAD_DIM ** 0.5 * jnp.max(jnp.abs(q_gain)) * jnp.max(jnp.abs(k_gain))
            + jnp.max(jnp.abs(bias)))


def _write_output(acc_ref, sg_ref, o_ref):
    def head_out(h):
        return acc_ref[h, 0:HEAD_DIM, :] / acc_ref[h, HEAD_DIM:HEAD_DIM + 1, :]

    for p in range(HEADS // 2):
        ot = jnp.concatenate([head_out(2 * p), head_out(2 * p + 1)], axis=0)
        cs = slice(LANES * p, LANES * (p + 1))
        o_ref[0, :, cs] = (ot.T * sg_ref[0, :, cs]).astype(BF16)


def _bit_planes(u):
    x = [u[8 * a:8 * a + 8, :] for a in range(KEY_BITS)]
    j, m = 16, 0x0000FFFF
    while j:
        for k in range(KEY_BITS):
            if not k & j:
                t = (x[k] ^ jnp.right_shift(x[k + j], j)) & m
                x[k] = x[k] ^ t
                x[k + j] = x[k + j] ^ jnp.left_shift(t, j)
        j >>= 1
        m ^= (m << j) & 0xFFFFFFFF
    return x


def _attn_kernel(rel_ref, bounded_ref,
                 qat_ref, ka_ref, vat_ref, kmean_ref, sga_ref,
                 qt_ref, qit_ref, wit_ref, kn_ref, vt_ref, sg_ref,
                 oa_ref, o_ref,
                 biasa_ref, bias_ref, qma_ref, qm_ref, acca_ref, acc_ref, s_ref, p_ref,
                 key_ref, plane_ref, alive_ref, madd_ref, *, topk):
    b = pl.program_id(0)
    i = pl.program_id(1)
    nblk = i + 1
    nb = key_ref.shape[0]
    bounded_a = bounded_ref[0] == 1
    bounded_b = bounded_ref[1] == 1

    @pl.when((b == 0) & (i == 0))
    def _():
        _build_bias_tiles(rel_ref, biasa_ref, 0)
        _build_bias_tiles(rel_ref, bias_ref, HEADS)
        plane_ref[...] = jnp.zeros_like(plane_ref)

    _store_masked_queries(qat_ref, qma_ref)
    _store_masked_queries(qt_ref, qm_ref)
    acca_ref[...] = jnp.zeros_like(acca_ref)
    acc_ref[...] = jnp.zeros_like(acc_ref)

    nidx = lax.broadcasted_iota(jnp.int32, (nb, BLK), 0)
    sel_bits = []
    for h in range(HEADS):
        cs = slice(LANES * (h // 2), LANES * (h // 2 + 1))
        gate = _dot(kmean_ref[0, :, cs].astype(BF16), qma_ref[h])
        g = jnp.where(nidx < i, gate, -jnp.inf)
        bits = jnp.left_shift(1, jnp.full((1, BLK), i, jnp.int32))
        for _ in range(MOBA_TOPK):
            mx = jnp.max(g, axis=0, keepdims=True)
            am = jnp.min(jnp.where(g == mx, nidx, nb), axis=0, keepdims=True)
            bits = bits | jnp.where(mx > -jnp.inf, jnp.left_shift(1, am), 0)
            g = jnp.where(nidx == am, -jnp.inf, g)
        sel_bits.append(bits)

    def score_a(j, h):
        rows = pl.ds(pl.multiple_of(j * BLK, BLK), BLK)
        cs = slice(LANES * (h // 2), LANES * (h // 2 + 1))
        return _dot(ka_ref[0, rows, cs], qma_ref[h]) + biasa_ref[3 * h + jnp.clip(i - j, 0, 2)]

    def pv_a(j, h, p):
        return _dot(vat_ref[0, j, h * VROWS:(h + 1) * VROWS, :], p)

    def sel_a(j, h):
        return (jnp.right_shift(sel_bits[h], j) & 1) == 1

    kpos = lax.broadcasted_iota(jnp.int32, (BLK, BLK), 0)
    qpos = lax.broadcasted_iota(jnp.int32, (BLK, BLK), 1)

    qit = qit_ref[0]
    wit = wit_ref[0]
    qims = [jnp.where((kpos // IDX_DIM) == h, qit, jnp.zeros_like(qit)) for h in range(IDX_HEADS)]
    whs = [wit[h:h + 1, :] for h in range(IDX_HEADS)]

    def score_block(j, causal):
        rows = pl.ds(pl.multiple_of(j * BLK, BLK), BLK)
        ki4 = kn_ref[0, rows, KB2:KN]
        score = jnp.zeros((BLK, BLK), F32)
        for h in range(IDX_HEADS):
            score = score + jnp.maximum(_dot(ki4, qims[h]), 0.0) * whs[h]
        bits = lax.bitcast_convert_type(score, jnp.int32)
        key = bits ^ (jnp.right_shift(bits, 31) & 0x7FFFFFFF)
        if causal:
            key = jnp.where(kpos <= qpos, key, INT_MIN)
        key_ref[j] = key
        planes = _bit_planes(key ^ INT_MIN)
        for r in range(KEY_BITS):
            plane_ref[j, r] = planes[r]

    for extra in range(3):
        key_ref[jnp.minimum(nblk + extra, nb - 1)] = jnp.full((BLK, BLK), INT_MIN, jnp.int32)

    def score_pair(t, c=0):
        score_block(2 * t, False)
        score_block(2 * t + 1, False)
        return c

    def score_single(_, c):
        score_block(i - 1, False)
        return c

    npair = (nblk + 1) // 2
    full_pairs = i // 2

    def score_quad(t):
        score_pair(2 * t)
        score_pair(2 * t + 1)

    @pl.when(bounded_a)
    def _():
        full_quads = full_pairs // 2
        _flash_bounded((0, full_quads), score_a, pv_a, p_ref, acca_ref, sel_a, side_fn=score_quad,
                       width=4)
        _flash_bounded((2 * full_quads, full_pairs), score_a, pv_a, p_ref, acca_ref, sel_a,
                       side_fn=score_pair)
        _flash_bounded((full_pairs, npair), score_a, pv_a, p_ref, acca_ref, sel_a)

    @pl.when(jnp.logical_not(bounded_a))
    def _():
        _flash_exact(npair, score_a, pv_a, s_ref, acca_ref, sel_a)
        lax.fori_loop(0, full_pairs, score_pair, 0)

    _write_output(acca_ref, sga_ref, oa_ref)
    lax.fori_loop(0, i % 2, score_single, 0)
    score_block(i, True)

    for j in range(nb):
        alive_ref[j] = jnp.where(j < nblk, jnp.full((8, BLK), -1, jnp.int32), 0)

    def radix_round(tiles, r, carry):
        above, thr_u = carry
        ones_here = jnp.zeros((8, BLK), jnp.int32)
        for j in range(tiles):
            ones_here = ones_here + lax.population_count(alive_ref[j] & plane_ref[j, r])
        ones_here = jnp.sum(ones_here, axis=0, keepdims=True)
        take = (above + ones_here) >= topk
        flip = jnp.where(take, 0, -1)
        for j in range(tiles):
            alive_ref[j] = alive_ref[j] & (plane_ref[j, r] ^ flip)
        return (jnp.where(take, above, above + ones_here),
                thr_u | jnp.where(take, jnp.left_shift(1, KEY_BITS - 1 - r), 0))

    def radix_select(tiles):
        zero = jnp.zeros((1, BLK), jnp.int32)
        return lambda: lax.fori_loop(0, KEY_BITS, functools.partial(radix_round, tiles), (zero, zero))

    quarter = nb // 4
    above, thr_u = lax.switch((nblk - 1) // quarter,
                              [radix_select(quarter * (n + 1)) for n in range(4)])
    thr = thr_u ^ INT_MIN

    need = jnp.where(thr > INT_MIN, topk - above, 0).astype(F32)
    lower = jnp.where(qpos <= kpos, 1.0, 0.0).astype(BF16)

    def mask_body(t, carry):
        for j in [4 * t + u for u in range(4)]:
            k = key_ref[j]
            eq = k == thr
            pref = _dot(lower, jnp.where(eq, 1.0, 0.0).astype(BF16))
            sel = (k > thr) | (eq & ((carry + pref) <= need))
            madd_ref[j] = jnp.where(sel, 0.0, NEG)
            carry = carry + pref[BLK - 1:BLK, :]
        return carry

    lax.fori_loop(0, (nblk + 3) // 4, mask_body, jnp.zeros((1, BLK), F32))

    def score(j, h):
        rows = pl.ds(pl.multiple_of(j * BLK, BLK), BLK)
        return (_dot(kn_ref[0, rows, 0:KB2], qm_ref[h])
                + bias_ref[3 * h + jnp.clip(i - j, 0, 2)] + madd_ref[j])

    def pv(j, h, p):
        return _dot(vt_ref[0, j], p)

    far_pairs = jnp.maximum(i - 1, 0) // 2

    def score_far(j, h):
        rows = pl.ds(pl.multiple_of(j * BLK, BLK), BLK)
        return _dot(kn_ref[0, rows, 0:KB2], qm_ref[h]) + madd_ref[j]

    @pl.when(bounded_b)
    def _():
        far_factor = [jnp.exp(jnp.full((1, BLK), rel_ref[REL_BUCKETS - 1, HEADS + h], F32))
                      for h in range(HEADS)]
        far_quads = far_pairs // 2
        _flash_bounded((0, far_quads), score_far, pv, p_ref, acc_ref, pv_scale=far_factor, width=4)
        _flash_bounded((2 * far_quads, far_pairs), score_far, pv, p_ref, acc_ref, pv_scale=far_factor)
        _flash_bounded((far_pairs, npair), score, pv, p_ref, acc_ref)

    @pl.when(jnp.logical_not(bounded_b))
    def _():
        _flash_exact(npair, score, pv, s_ref, acc_ref)

    _write_output(acc_ref, sg_ref, o_ref)


def _attention(rel_bias, bounded, qat, ka, vat, kmean, sga, qbt, qit, wit, kn, vbt, sgb):
    bsz, s, _ = kn.shape
    nb = s // BLK
    topk = min(DSA_TOPK_MAX, s // 4)
    smem = pl.BlockSpec(memory_space=pltpu.SMEM)
    qspec = pl.BlockSpec((1, BLK, WIDTH), lambda b, i: (b, i, 0))
    tile = lambda n: pl.BlockSpec((1, n, BLK), lambda b, i: (b * nb + i, 0, 0))
    row = lambda *shape: pl.BlockSpec((1,) + shape, lambda b, i: (b,) + (0,) * len(shape),
                                      pipeline_mode=pl.Buffered(1))
    out = jax.ShapeDtypeStruct((bsz, s, WIDTH), BF16)
    return pl.pallas_call(
        functools.partial(_attn_kernel, topk=topk),
        grid=(bsz, nb),
        in_specs=[smem, smem,
                  tile(WIDTH), row(s, WIDTH), row(nb, HEADS * VROWS, BLK), row(nb, WIDTH), qspec,
                  tile(WIDTH), tile(QI), tile(WI_ROWS), row(s, KN), row(nb, VROWS, BLK), qspec],
        out_specs=[qspec, qspec],
        out_shape=[out, out],
        scratch_shapes=[pltpu.VMEM((3 * HEADS, BLK, BLK), F32),
                        pltpu.VMEM((3 * HEADS, BLK, BLK), F32),
                        pltpu.VMEM((HEADS, LANES, BLK), BF16),
                        pltpu.VMEM((HEADS, LANES, BLK), BF16),
                        pltpu.VMEM((HEADS, VROWS, BLK), F32),
                        pltpu.VMEM((HEADS, VROWS, BLK), F32),
                        pltpu.VMEM((HEADS, 2 * BLK, BLK), F32),
                        pltpu.VMEM((4 * HEADS, BLK, BLK), BF16),
                        pltpu.VMEM((nb, BLK, BLK), jnp.int32),
                        pltpu.VMEM((nb, KEY_BITS, 8, BLK), jnp.int32),
                        pltpu.VMEM((nb, 8, BLK), jnp.int32),
                        pltpu.VMEM((nb, BLK, BLK), F32)],
        compiler_params=pltpu.CompilerParams(
            dimension_semantics=("arbitrary", "arbitrary"), vmem_limit_bytes=VMEM_LIMIT),
        name="attention",
    )(rel_bias, bounded, qat, ka, vat.reshape(bsz, nb, HEADS * VROWS, BLK), kmean, sga,
      qbt, qit, wit, kn, vbt.reshape(bsz, nb, VROWS, BLK), sgb)


def _outproj_kernel(x_ref, gn_ref, ya_ref, yb_ref, wm_ref, wa_ref, wb_ref, wo_ref, o_ref):
    x = x_ref[...]
    r = lax.rsqrt(jnp.mean(x * x, axis=-1, keepdims=True) + EPS)
    h = ((x * r) * gn_ref[...]).astype(BF16)
    gate_a = jax.nn.sigmoid(_dot(h, wm_ref[:, N_NAT:N_NAT + D_MODEL]))
    gate_b = jax.nn.sigmoid(_dot(h, wm_ref[:, N_NAT + D_MODEL:N_NAT + 2 * D_MODEL]))
    merged = gate_a * _dot(ya_ref[...], wa_ref[...]) + gate_b * _dot(yb_ref[...], wb_ref[...])
    o_ref[...] = x + _dot(merged.astype(BF16), wo_ref[...])


def _outproj(x2, gn, ya, yb, wm, wa, wb, wo):
    m = x2.shape[0]
    row = lambda n: pl.BlockSpec((ROWS_OUT, n), lambda i: (i, 0))
    full = lambda a: pl.BlockSpec(a.shape, lambda i: (0,) * a.ndim)
    return pl.pallas_call(
        _outproj_kernel,
        grid=(m // ROWS_OUT,),
        in_specs=[row(D_MODEL), full(gn), row(WIDTH), row(WIDTH),
                  full(wm), full(wa), full(wb), full(wo)],
        out_specs=row(D_MODEL),
        out_shape=jax.ShapeDtypeStruct((m, D_MODEL), F32),
        compiler_params=pltpu.CompilerParams(
            dimension_semantics=("arbitrary",), vmem_limit_bytes=VMEM_LIMIT),
        name="outproj",
    )(x2, gn, ya, yb, wm, wa, wb, wo)


def _pack_w_in(w):
    o = 0
    parts = {}
    for name, n in (("qa", WIDTH), ("ka", WIDTH), ("va", WIDTH), ("ga", WIDTH), ("qb", WIDTH),
                    ("kb", HEAD_DIM), ("vb", HEAD_DIM), ("gb", WIDTH), ("qi", IDX_HEADS * IDX_DIM),
                    ("ki", IDX_DIM), ("wi", IDX_HEADS), ("ma", D_MODEL), ("mb", D_MODEL)):
        parts[name] = w[:, o:o + n]
        o += n
    wn = jnp.concatenate([parts["ka"], parts["kb"], parts["kb"],
                          parts["ki"], parts["ki"], parts["ki"], parts["ki"],
                          parts["ga"], parts["gb"], parts["ma"], parts["mb"]], axis=1).astype(BF16)
    pad = jnp.zeros((w.shape[0], N_TR - RT_WI - IDX_HEADS), w.dtype)
    wt = jnp.concatenate([parts["qa"], parts["qb"], parts["va"], parts["qi"], parts["vb"],
                          parts["wi"], pad], axis=1).T.astype(BF16)
    assert wn.shape[1] == N_NAT + 2 * D_MODEL and wt.shape[0] == N_TR
    return wn, wt


def _layer(x2, bsz, s, gn, w_in, qn_a, kn_a, qn_b, kn_b, w_br_a, w_br_b, w_out, rel_bias):
    wn, wt = _pack_w_in(w_in)
    ghn = jnp.concatenate([jnp.tile(kn_a, HEADS), jnp.tile(kn_b, 2)])[None, :]
    ght = jnp.concatenate([jnp.tile(qn_a, HEADS), jnp.tile(qn_b, HEADS)])[:, None]
    (ka, kn, sga, sgb, kmean, qat, qbt, vat, qit, vbt, wit) = _inproj(
        x2, gn[None, :], wn, wt, ghn, ght)
    r3 = lambda a: a.reshape(bsz, s, a.shape[-1])
    bounds = jnp.stack([_logit_bound(qn_a, kn_a, rel_bias[:, :HEADS]),
                        _logit_bound(qn_b, kn_b, rel_bias[:, HEADS:])])
    ya, yb = _attention(rel_bias, (bounds <= MAX_PLAIN_LOGIT).astype(jnp.int32),
                        qat, r3(ka), vat, kmean.reshape(bsz, s // BLK, WIDTH), r3(sga),
                        qbt, qit, wit, r3(kn), vbt, r3(sgb))
    return _outproj(x2, gn[None, :], ya.reshape(-1, WIDTH), yb.reshape(-1, WIDTH), wn,
                    w_br_a.astype(BF16), w_br_b.astype(BF16), w_out.astype(BF16))


def kernel(x, norm_g, w_in, q_norm_a, k_norm_a, q_norm_b, k_norm_b,
           w_branch_a, w_branch_b, w_out, rel_bias):
    bsz, s, d = x.shape
    assert d == D_MODEL and s % (4 * BLK) == 0 and s // BLK <= 31
    h = x.reshape(bsz * s, d)
    for l in range(norm_g.shape[0]):
        h = _layer(h, bsz, s, norm_g[l], w_in[l], q_norm_a[l], k_norm_a[l], q_norm_b[l],
                   k_norm_b[l], w_branch_a[l], w_branch_b[l], w_out[l], rel_bias)
    return h.reshape(bsz, s, d)
```
